```python
import jax, jax.numpy as jnp
from jax import lax
import numpy as np

D_MODEL = 4096
BATCH = 2
SEQ = 8192
DEPTH = 2

RWKV_WIDTH = D_MODEL // 2
RWKV_HEAD = 64
RWKV_HEADS = RWKV_WIDTH // RWKV_HEAD
CONV_WIDTH = D_MODEL - RWKV_WIDTH
CONV_TAPS = 31
MIX_WIDTH = RWKV_WIDTH + CONV_WIDTH
IN_COLS = 3 * RWKV_WIDTH + 2 * CONV_WIDTH
DECAY_RANK = max(32, int(round(1.8 * RWKV_WIDTH ** 0.5 / 32)) * 32)
AAA_RANK = max(32, int(round(1.8 * RWKV_WIDTH ** 0.5 / 32)) * 32)
MV_RANK = max(32, int(round(1.3 * RWKV_WIDTH ** 0.5 / 32)) * 32)
GATE_RANK = max(32, int(round(0.6 * RWKV_WIDTH ** 0.8 / 32)) * 32)
D_FF_DENSE = 7 * D_MODEL // 2
D_FF_EXPERT = 7 * D_MODEL // 8
N_EXPERTS = 8
TOP_K = 2
RMS_EPS = 1e-6
GN_EPS = 64e-5
LN_EPS = 1e-5

kernel_name = 'hybrid_rwkv7_conformer_moe_block'


def rms_norm(x, g):
    xf = x.astype(jnp.float32)
    y = xf * lax.rsqrt(jnp.mean(xf * xf, axis=-1, keepdims=True) + RMS_EPS)
    return (y * g.astype(jnp.float32)).astype(x.dtype)


def layer_norm(x, g, b):
    xf = x.astype(jnp.float32)
    mu = jnp.mean(xf, axis=-1, keepdims=True)
    var = jnp.mean(jnp.square(xf - mu), axis=-1, keepdims=True)
    y = (xf - mu) * lax.rsqrt(var + LN_EPS)
    return (y * g.astype(jnp.float32) + b.astype(jnp.float32)).astype(x.dtype)


def modulate(xn, shift, scale):
    return xn * (1.0 + scale[:, None, :]) + shift[:, None, :]


def token_shift(h):
    return jnp.concatenate([jnp.zeros_like(h[:, :1]), h[:, :-1]], axis=1)


def wkv7_scan(r, w, k, v, a, b):
    bsz, _, nh, n = r.shape

    def step(state, inp):
        r_t, w_t, k_t, v_t, a_t, b_t = inp
        sa = jnp.einsum('bhij,bhj->bhi', state, a_t)
        state = (state * w_t[:, :, None, :] + sa[..., :, None] * b_t[:, :, None, :]
                 + v_t[..., :, None] * k_t[:, :, None, :])
        y_t = jnp.einsum('bhij,bhj->bhi', state, r_t)
        return state, y_t

    xs = tuple(jnp.moveaxis(t.astype(jnp.float32), 1, 0) for t in (r, w, k, v, a, b))
    state0 = jnp.zeros((bsz, nh, n, n), jnp.float32)
    _, ys = lax.scan(step, state0, xs)
    return jnp.moveaxis(ys, 0, 1)


def causal_depthwise_conv(u, w, bias):
    taps, ch = w.shape
    y = lax.conv_general_dilated(u, w[:, None, :].astype(u.dtype), window_strides=(1,),
                                 padding=[(taps - 1, 0)],
                                 dimension_numbers=('NWC', 'WIO', 'NWC'),
                                 feature_group_count=ch)
    return y + bias


def hybrid_mixer(xm, v_first, w_in, mu_rkv, mu_w, mu_a, mu_g, w0, w1, w2, a0, a1, a2,
                 g1, g2, k_k, k_a, r_k, lnx_g, lnx_b, conv_w, conv_b, conv_ln_g,
                 conv_ln_b, w_out, mu_v, v0, v1, v2):
    bsz, s, _ = xm.shape
    RW, CW, H, N = RWKV_WIDTH, CONV_WIDTH, RWKV_HEADS, RWKV_HEAD
    h = xm @ w_in

    h_rkv = h[..., :3 * RW]
    h_rkv = h_rkv + (token_shift(h_rkv) - h_rkv) * mu_rkv
    r = h_rkv[..., :RW]
    k = h_rkv[..., RW:2 * RW]
    v = h_rkv[..., 2 * RW:]
    xx = token_shift(xm) - xm
    xw = xm + xx * mu_w
    xa = xm + xx * mu_a
    xg = xm + xx * mu_g
    w_log = -jax.nn.softplus(-(w0 + jnp.tanh(xw @ w1) @ w2)) - 0.5
    decay = jnp.exp(-jnp.exp(w_log.astype(jnp.float32)))
    a = jax.nn.sigmoid(a0 + (xa @ a1) @ a2)
    g = jax.nn.sigmoid(xg @ g1) @ g2
    if mu_v is None:
        v_first = v
    else:
        xv = xm + xx * mu_v
        v = v + (v_first - v) * jax.nn.sigmoid(v0 + (xv @ v1) @ v2)

    def heads(t):
        return t.reshape(bsz, s, H, N)

    rh, kh, vh, ah, dh = heads(r), heads(k), heads(v), heads(a), heads(decay)
    kkf = (kh * k_k.reshape(H, N)).astype(jnp.float32)
    kk = kkf / jnp.maximum(jnp.sqrt(jnp.sum(kkf * kkf, axis=-1, keepdims=True)), 1e-12)
    kh = kh * (1.0 + (ah - 1.0) * k_a.reshape(H, N))
    y = wkv7_scan(rh, dh, kh, vh, -kk, kk * ah.astype(jnp.float32))
    mu = jnp.mean(y, axis=-1, keepdims=True)
    var = jnp.mean(jnp.square(y - mu), axis=-1, keepdims=True)
    yn = ((y - mu) * lax.rsqrt(var + GN_EPS)).reshape(bsz, s, RW)
    yn = (yn * lnx_g.astype(jnp.float32) + lnx_b.astype(jnp.float32)).astype(xm.dtype)
    bonus = (jnp.sum(rh * kh * r_k, axis=-1, keepdims=True) * vh).reshape(bsz, s, RW)
    y_rwkv = (yn + bonus) * g

    h_glu = h[..., 3 * RW:]
    u = h_glu[..., :CW] * jax.nn.sigmoid(h_glu[..., CW:])
    u = causal_depthwise_conv(u, conv_w, conv_b)
    u = jax.nn.silu(layer_norm(u, conv_ln_g, conv_ln_b))

    out = jnp.concatenate([y_rwkv, u], axis=-1) @ w_out
    return out, v_first


def swiglu(x, w_gate, w_up, w_down):
    return (jax.nn.silu(x @ w_gate) * (x @ w_up)) @ w_down


def moe_swiglu(x, router_w, router_b, w_gate, w_up, w_down):
    bsz, s, d = x.shape
    xt = x.reshape(bsz * s, d)
    logits = (xt @ router_w + router_b).astype(jnp.float32)
    top_val, top_idx = lax.top_k(logits, TOP_K)
    top_w = jax.nn.softmax(top_val, axis=-1)
    combine = jnp.sum(jax.nn.one_hot(top_idx, N_EXPERTS, dtype=jnp.float32) * top_w[..., None],
                      axis=1).astype(x.dtype)
    out = jnp.zeros_like(xt)
    for e in range(N_EXPERTS):
        out = out + combine[:, e:e + 1] * swiglu(xt, w_gate[e], w_up[e], w_down[e])
    return out.reshape(bsz, s, d)


def _normal(k, shape, scale):
    return jax.random.normal(k, shape, jnp.float32) * scale


def _gain(k, shape):
    return 1.0 + 0.05 * jax.random.normal(k, shape, jnp.float32)


def _mixer_params(keys, prefix, value_residual):
    D, RW, CW = D_MODEL, RWKV_WIDTH, CONV_WIDTH
    p = {}
    p[prefix + 'ada_w'] = _normal(next(keys), (D, 6 * D), 0.5 * D ** -0.5)
    p[prefix + 'ada_b'] = _normal(next(keys), (6 * D,), 0.02)
    p[prefix + 'mix_pre_g'] = _gain(next(keys), (D,))
    p[prefix + 'mix_post_g'] = _gain(next(keys), (D,))
    p[prefix + 'w_in'] = _normal(next(keys), (D, IN_COLS), D ** -0.5)
    p[prefix + 'mu_rkv'] = jax.random.uniform(next(keys), (3 * RW,), jnp.float32)
    p[prefix + 'mu_w'] = jax.random.uniform(next(keys), (D,), jnp.float32)
    p[prefix + 'mu_a'] = jax.random.uniform(next(keys), (D,), jnp.float32)
    p[prefix + 'mu_g'] = jax.random.uniform(next(keys), (D,), jnp.float32)
    p[prefix + 'w0'] = jax.random.uniform(next(keys), (RW,), jnp.float32, minval=-6.0, maxval=1.0)
    p[prefix + 'w1'] = _normal(next(keys), (D, DECAY_RANK), D ** -0.5)
    p[prefix + 'w2'] = _normal(next(keys), (DECAY_RANK, RW), 0.1 * DECAY_RANK ** -0.5)
    p[prefix + 'a0'] = _normal(next(keys), (RW,), 0.5)
    p[prefix + 'a1'] = _normal(next(keys), (D, AAA_RANK), D ** -0.5)
    p[prefix + 'a2'] = _normal(next(keys), (AAA_RANK, RW), AAA_RANK ** -0.5)
    p[prefix + 'g1'] = _normal(next(keys), (D, GATE_RANK), D ** -0.5)
    p[prefix + 'g2'] = _normal(next(keys), (GATE_RANK, RW), GATE_RANK ** -0.5)
    p[prefix + 'k_k'] = 0.85 + 0.1 * jax.random.normal(next(keys), (RW,), jnp.float32)
    p[prefix + 'k_a'] = 1.0 + 0.1 * jax.random.normal(next(keys), (RW,), jnp.float32)
    p[prefix + 'r_k'] = _normal(next(keys), (RWKV_HEADS, RWKV_HEAD), 0.1)
    p[prefix + 'lnx_g'] = _gain(next(keys), (RW,))
    p[prefix + 'lnx_b'] = _normal(next(keys), (RW,), 0.02)
    p[prefix + 'conv_w'] = _normal(next(keys), (CONV_TAPS, CW), CONV_TAPS ** -0.5)
    p[prefix + 'conv_b'] = _normal(next(keys), (CW,), 0.02)
    p[prefix + 'conv_ln_g'] = _gain(next(keys), (CW,))
    p[prefix + 'conv_ln_b'] = _normal(next(keys), (CW,), 0.02)
    p[prefix + 'w_out'] = _normal(next(keys), (MIX_WIDTH, D), MIX_WIDTH ** -0.5)
    if value_residual:
        p[prefix + 'mu_v'] = jax.random.uniform(next(keys), (D,), jnp.float32)
        p[prefix + 'v0'] = _normal(next(keys), (RW,), 0.5)
        p[prefix + 'v1'] = _normal(next(keys), (D, MV_RANK), D ** -0.5)
        p[prefix + 'v2'] = _normal(next(keys), (MV_RANK, RW), MV_RANK ** -0.5)
    return p


def setup_inputs(seed: int = 0) -> dict:
    key = jax.random.key(seed)
    keys = iter(jax.random.split(key, 128))
    D = D_MODEL
    p = {}
    p['x'] = jax.random.normal(next(keys), (BATCH, SEQ, D), jnp.float32)
    p['c'] = jax.random.normal(next(keys), (BATCH, D), jnp.float32)
    p.update(_mixer_params(keys, 'l0_', False))
    p['l0_ffn_pre_g'] = _gain(next(keys), (D,))
    p['l0_ffn_post_g'] = _gain(next(keys), (D,))
    p['l0_ffn_w_gate'] = _normal(next(keys), (D, D_FF_DENSE), D ** -0.5)
    p['l0_ffn_w_up'] = _normal(next(keys), (D, D_FF_DENSE), D ** -0.5)
    p['l0_ffn_w_down'] = _normal(next(keys), (D_FF_DENSE, D), D_FF_DENSE ** -0.5)
    p.update(_mixer_params(keys, 'l1_', True))
    p['l1_ffn_pre_g'] = _gain(next(keys), (D,))
    p['l1_ffn_post_g'] = _gain(next(keys), (D,))
    p['l1_router_w'] = _normal(next(keys), (D, N_EXPERTS), D ** -0.5)
    p['l1_router_b'] = _normal(next(keys), (N_EXPERTS,), 0.01)
    p['l1_moe_w_gate'] = _normal(next(keys), (N_EXPERTS, D, D_FF_EXPERT), D ** -0.5)
    p['l1_moe_w_up'] = _normal(next(keys), (N_EXPERTS, D, D_FF_EXPERT), D ** -0.5)
    p['l1_moe_w_down'] = _normal(next(keys), (N_EXPERTS, D_FF_EXPERT, D), D_FF_EXPERT ** -0.5)
    return p


def reference(x, c,
              l0_ada_w, l0_ada_b, l0_mix_pre_g, l0_mix_post_g, l0_w_in, l0_mu_rkv, l0_mu_w,
              l0_mu_a, l0_mu_g, l0_w0, l0_w1, l0_w2, l0_a0, l0_a1, l0_a2, l0_g1, l0_g2,
              l0_k_k, l0_k_a, l0_r_k, l0_lnx_g, l0_lnx_b, l0_conv_w, l0_conv_b, l0_conv_ln_g,
              l0_conv_ln_b, l0_w_out,
              l0_ffn_pre_g, l0_ffn_post_g, l0_ffn_w_gate, l0_ffn_w_up, l0_ffn_w_down,
              l1_ada_w, l1_ada_b, l1_mix_pre_g, l1_mix_post_g, l1_w_in, l1_mu_rkv, l1_mu_w,
              l1_mu_a, l1_mu_g, l1_w0, l1_w1, l1_w2, l1_a0, l1_a1, l1_a2, l1_g1, l1_g2,
              l1_k_k, l1_k_a, l1_r_k, l1_lnx_g, l1_lnx_b, l1_conv_w, l1_conv_b, l1_conv_ln_g,
              l1_conv_ln_b, l1_w_out, l1_mu_v, l1_v0, l1_v1, l1_v2,
              l1_ffn_pre_g, l1_ffn_post_g, l1_router_w, l1_router_b, l1_moe_w_gate,
              l1_moe_w_up, l1_moe_w_down):
    layers = [
        dict(ada_w=l0_ada_w, ada_b=l0_ada_b, mix_pre_g=l0_mix_pre_g, mix_post_g=l0_mix_post_g,
             ffn_pre_g=l0_ffn_pre_g, ffn_post_g=l0_ffn_post_g,
             mixer=dict(w_in=l0_w_in, mu_rkv=l0_mu_rkv, mu_w=l0_mu_w, mu_a=l0_mu_a, mu_g=l0_mu_g,
                        w0=l0_w0, w1=l0_w1, w2=l0_w2, a0=l0_a0, a1=l0_a1, a2=l0_a2,
                        g1=l0_g1, g2=l0_g2, k_k=l0_k_k, k_a=l0_k_a, r_k=l0_r_k,
                        lnx_g=l0_lnx_g, lnx_b=l0_lnx_b, conv_w=l0_conv_w, conv_b=l0_conv_b,
                        conv_ln_g=l0_conv_ln_g, conv_ln_b=l0_conv_ln_b, w_out=l0_w_out,
                        mu_v=None, v0=None, v1=None, v2=None),
             ffn=(l0_ffn_w_gate, l0_ffn_w_up, l0_ffn_w_down)),
        dict(ada_w=l1_ada_w, ada_b=l1_ada_b, mix_pre_g=l1_mix_pre_g, mix_post_g=l1_mix_post_g,
             ffn_pre_g=l1_ffn_pre_g, ffn_post_g=l1_ffn_post_g,
             mixer=dict(w_in=l1_w_in, mu_rkv=l1_mu_rkv, mu_w=l1_mu_w, mu_a=l1_mu_a, mu_g=l1_mu_g,
                        w0=l1_w0, w1=l1_w1, w2=l1_w2, a0=l1_a0, a1=l1_a1, a2=l1_a2,
                        g1=l1_g1, g2=l1_g2, k_k=l1_k_k, k_a=l1_k_a, r_k=l1_r_k,
                        lnx_g=l1_lnx_g, lnx_b=l1_lnx_b, conv_w=l1_conv_w, conv_b=l1_conv_b,
                        conv_ln_g=l1_conv_ln_g, conv_ln_b=l1_conv_ln_b, w_out=l1_w_out,
                        mu_v=l1_mu_v, v0=l1_v0, v1=l1_v1, v2=l1_v2),
             ffn=(l1_router_w, l1_router_b, l1_moe_w_gate, l1_moe_w_up, l1_moe_w_down)),
    ]
    v_first = None
    for layer in range(DEPTH):
        lp = layers[layer]
        mod = jax.nn.silu(c) @ lp['ada_w'] + lp['ada_b']
        sh_m, sc_m, gt_m, sh_f, sc_f, gt_f = jnp.split(mod, 6, axis=-1)
        xm = modulate(rms_norm(x, lp['mix_pre_g']), sh_m, sc_m)
        y, v_first = hybrid_mixer(xm, v_first, **lp['mixer'])
        x = x + gt_m[:, None, :] * rms_norm(y, lp['mix_post_g'])
        xf = modulate(rms_norm(x, lp['ffn_pre_g']), sh_f, sc_f)
        if layer % 2 == 0:
            y = swiglu(xf, *lp['ffn'])
        else:
            y = moe_swiglu(xf, *lp['ffn'])
        x = x + gt_f[:, None, :] * rms_norm(y, lp['ffn_post_g'])
    return x
```

```python
import functools

import jax
import jax.numpy as jnp
from jax import lax
from jax.experimental import pallas as pl
from jax.experimental.pallas import tpu as pltpu

F32 = jnp.float32
BF16 = jnp.bfloat16

HEAD = 64
HEADS_PER_GROUP = 4
GROUP_LANES = HEAD * HEADS_PER_GROUP
CHUNK = 64
LANE = 128
SUBLANE = 8
TOP_K = 2
RMS_EPS = 1e-6
GN_EPS = 64e-5
LN_EPS = 1e-5
VMEM_LIMIT_BYTES = 56 * 1024 * 1024


def _params(*sem):
    return pltpu.CompilerParams(dimension_semantics=sem, vmem_limit_bytes=VMEM_LIMIT_BYTES)


def _round_up(n, m):
    return (n + m - 1) // m * m


def _pick(n, pref):
    if n <= pref:
        return n
    t = pref
    while t >= LANE:
        if n % t == 0:
            return t
        t -= LANE
    return n


def _dot(a, b):
    return jnp.dot(a, b, preferred_element_type=F32)


def _rms(x, g):
    return x * lax.rsqrt(jnp.mean(x * x, axis=-1, keepdims=True) + RMS_EPS) * g


def _ada_kernel(c_ref, w_ref, b_ref, o_ref):
    c = c_ref[...]
    s = c * jax.nn.sigmoid(c)
    o_ref[...] = jnp.dot(s, w_ref[...], preferred_element_type=F32,
                         precision=lax.Precision.HIGHEST) + b_ref[...]


def ada_modulation(c, ada_w, ada_b):
    bsz, d = c.shape
    n = ada_w.shape[1]
    rows = _round_up(bsz, SUBLANE)
    cp = jnp.zeros((rows, d), F32).at[:bsz].set(c)
    tn = _pick(n, 512)
    out = pl.pallas_call(
        _ada_kernel,
        grid=(n // tn,),
        in_specs=[pl.BlockSpec((rows, d), lambda j: (0, 0)),
                  pl.BlockSpec((d, tn), lambda j: (0, j)),
                  pl.BlockSpec((1, tn), lambda j: (0, j))],
        out_specs=pl.BlockSpec((rows, tn), lambda j: (0, j)),
        out_shape=jax.ShapeDtypeStruct((rows, n), F32),
        compiler_params=_params("arbitrary"),
        name="ada_matvec",
    )(cp, ada_w, ada_b.reshape(1, n))
    return out[:bsz]


def _norm_mod_kernel(x_ref, g_ref, sh_ref, sc_ref, o_ref):
    xn = _rms(x_ref[0], g_ref[...])
    o_ref[0] = (xn * (1.0 + sc_ref[0]) + sh_ref[0]).astype(o_ref.dtype)


def norm_modulate(x, g, shift, scale):
    bsz, s, d = x.shape
    tm = _pick(s, 512)
    vec = pl.BlockSpec((1, 1, d), lambda b, i: (b, 0, 0))
    return pl.pallas_call(
        _norm_mod_kernel,
        grid=(bsz, s // tm),
        in_specs=[pl.BlockSpec((1, tm, d), lambda b, i: (b, i, 0)),
                  pl.BlockSpec((1, d), lambda b, i: (0, 0)), vec, vec],
        out_specs=pl.BlockSpec((1, tm, d), lambda b, i: (b, i, 0)),
        out_shape=jax.ShapeDtypeStruct((bsz, s, d), BF16),
        compiler_params=_params("arbitrary", "arbitrary"),
        name="norm_modulate",
    )(x, g.reshape(1, d), shift.reshape(bsz, 1, d), scale.reshape(bsz, 1, d))


def _mm_kernel(*refs, n_seg):
    o_ref = refs[2 * n_seg]
    acc = _dot(refs[0][...], refs[n_seg][...])
    for s in range(1, n_seg):
        acc = acc + _dot(refs[s][...], refs[n_seg + s][...])
    o_ref[...] = acc.astype(o_ref.dtype)


def matmul(a_list, w_list, out_dtype, tm_pref=1024, tn_pref=1024, name="matmul"):
    m = a_list[0].shape[0]
    n = w_list[0].shape[1]
    tm = _pick(m, tm_pref)
    tn = _pick(n, tn_pref)
    n_seg = len(a_list)
    in_specs = [pl.BlockSpec((tm, a.shape[1]), lambda i, j: (i, 0)) for a in a_list]
    in_specs += [pl.BlockSpec((w.shape[0], tn), lambda i, j: (0, j)) for w in w_list]
    return pl.pallas_call(
        functools.partial(_mm_kernel, n_seg=n_seg),
        grid=(m // tm, n // tn),
        in_specs=in_specs,
        out_specs=pl.BlockSpec((tm, tn), lambda i, j: (i, j)),
        out_shape=jax.ShapeDtypeStruct((m, n), out_dtype),
        compiler_params=_params("arbitrary", "arbitrary"),
        name=name,
    )(*a_list, *w_list)


def _top2_combine(logits, n_experts):
    lane = lax.broadcasted_iota(jnp.int32, logits.shape, 1)
    neg = jnp.float32(-jnp.inf)
    lg = jnp.where(lane < n_experts, logits, neg)
    v1 = jnp.max(lg, axis=-1, keepdims=True)
    i1 = jnp.min(jnp.where(lg == v1, lane, LANE), axis=-1, keepdims=True)
    lg2 = jnp.where(lane == i1, neg, lg)
    v2 = jnp.max(lg2, axis=-1, keepdims=True)
    i2 = jnp.min(jnp.where(lg2 == v2, lane, LANE), axis=-1, keepdims=True)
    e2 = jnp.exp(v2 - v1)
    w1 = 1.0 / (1.0 + e2)
    w2 = e2 / (1.0 + e2)
    return jnp.where(lane == i1, w1, 0.0) + jnp.where(lane == i2, w2, 0.0)


def _epilogue_kernel(*refs, has_next, n_experts):
    y_ref, x_ref, post_g_ref, gate_ref = refs[:4]
    pos = 4
    if has_next:
        pre_g_ref, sh_ref, sc_ref = refs[pos:pos + 3]
        pos += 3
    if n_experts:
        rw_ref, rb_ref = refs[pos:pos + 2]
        pos += 2
    outs = refs[pos:]
    x1 = x_ref[0] + gate_ref[0] * _rms(y_ref[0], post_g_ref[...])
    outs[0][0] = x1
    if has_next:
        xf = _rms(x1, pre_g_ref[...]) * (1.0 + sc_ref[0]) + sh_ref[0]
        outs[1][0] = xf.astype(outs[1].dtype)
        if n_experts:
            logits = jnp.dot(xf, rw_ref[...], preferred_element_type=F32,
                             precision=lax.Precision.HIGHEST) + rb_ref[...]
            outs[2][0] = _top2_combine(logits, n_experts)


def sublayer_epilogue(y, x, post_g, gate, nxt=None, router=None):
    bsz, s, d = x.shape
    tm = _pick(s, 256)
    blk = pl.BlockSpec((1, tm, d), lambda b, i: (b, i, 0))
    row = pl.BlockSpec((1, d), lambda b, i: (0, 0))
    vec = pl.BlockSpec((1, 1, d), lambda b, i: (b, 0, 0))
    args = [y, x, post_g.reshape(1, d), gate.reshape(bsz, 1, d)]
    in_specs = [blk, blk, row, vec]
    out_shape = [jax.ShapeDtypeStruct((bsz, s, d), F32)]
    out_specs = [blk]
    n_experts = 0
    if nxt is not None:
        pre_g, shift, scale = nxt
        args += [pre_g.reshape(1, d), shift.reshape(bsz, 1, d), scale.reshape(bsz, 1, d)]
        in_specs += [row, vec, vec]
        out_shape.append(jax.ShapeDtypeStruct((bsz, s, d), BF16))
        out_specs.append(blk)
        if router is not None:
            router_w, router_b = router
            n_experts = router_w.shape[1]
            rw = jnp.zeros((d, LANE), F32).at[:, :n_experts].set(router_w)
            rb = jnp.zeros((1, LANE), F32).at[0, :n_experts].set(router_b)
            args += [rw, rb]
            in_specs += [pl.BlockSpec((d, LANE), lambda b, i: (0, 0)),
                         pl.BlockSpec((1, LANE), lambda b, i: (0, 0))]
            out_shape.append(jax.ShapeDtypeStruct((bsz, s, LANE), F32))
            out_specs.append(pl.BlockSpec((1, tm, LANE), lambda b, i: (b, i, 0)))
    return pl.pallas_call(
        functools.partial(_epilogue_kernel, has_next=nxt is not None, n_experts=n_experts),
        grid=(bsz, s // tm),
        in_specs=in_specs,
        out_specs=out_specs,
        out_shape=out_shape,
        compiler_params=_params("arbitrary", "arbitrary"),
        name="sublayer_epilogue",
    )(*args)


def _ffn_kernel(x_ref, wg_ref, wu_ref, wd_ref, o_ref):
    @pl.when(pl.program_id(1) == 0)
    def _():
        o_ref[...] = jnp.zeros_like(o_ref)

    x = x_ref[...]
    hg = _dot(x, wg_ref[...])
    hu = _dot(x, wu_ref[...])
    h = (hg * jax.nn.sigmoid(hg) * hu).astype(BF16)
    o_ref[...] += _dot(h, wd_ref[...])


def dense_swiglu(x, w_gate, w_up, w_down):
    m, d = x.shape
    f = w_gate.shape[1]
    tm = _pick(m, 512)
    tf = _pick(f, 512)
    return pl.pallas_call(
        _ffn_kernel,
        grid=(m // tm, f // tf),
        in_specs=[pl.BlockSpec((tm, d), lambda i, j: (i, 0)),
                  pl.BlockSpec((d, tf), lambda i, j: (0, j)),
                  pl.BlockSpec((d, tf), lambda i, j: (0, j)),
                  pl.BlockSpec((tf, d), lambda i, j: (j, 0))],
        out_specs=pl.BlockSpec((tm, d), lambda i, j: (i, 0)),
        out_shape=jax.ShapeDtypeStruct((m, d), F32),
        compiler_params=_params("arbitrary", "arbitrary"),
        name="dense_swiglu",
    )(x, w_gate, w_up, w_down)


def _moe_kernel(x_ref, comb_ref, wg_ref, wu_ref, wd_ref, o_ref):
    e = pl.program_id(1)

    @pl.when((e == 0) & (pl.program_id(2) == 0))
    def _():
        o_ref[...] = jnp.zeros_like(o_ref)

    comb = comb_ref[...]
    lane = lax.broadcasted_iota(jnp.int32, comb.shape, 1)
    c = jnp.sum(jnp.where(lane == e, comb, 0.0), axis=-1, keepdims=True)
    x = x_ref[...]
    hg = _dot(x, wg_ref[...])
    hu = _dot(x, wu_ref[...])
    h = (hg * jax.nn.sigmoid(hg) * hu * c).astype(BF16)
    o_ref[...] += _dot(h, wd_ref[...])


def moe_swiglu(x, comb, w_gate, w_up, w_down):
    m, d = x.shape
    n_e, _, f = w_gate.shape
    tm = _pick(m, 512)
    tf = _pick(f, 512)
    return pl.pallas_call(
        _moe_kernel,
        grid=(m // tm, n_e, f // tf),
        in_specs=[pl.BlockSpec((tm, d), lambda i, e, j: (i, 0)),
                  pl.BlockSpec((tm, LANE), lambda i, e, j: (i, 0)),
                  pl.BlockSpec((None, d, tf), lambda i, e, j: (e, 0, j)),
                  pl.BlockSpec((None, d, tf), lambda i, e, j: (e, 0, j)),
                  pl.BlockSpec((None, tf, d), lambda i, e, j: (e, j, 0))],
        out_specs=pl.BlockSpec((tm, d), lambda i, e, j: (i, 0)),
        out_shape=jax.ShapeDtypeStruct((m, d), F32),
        compiler_params=_params("arbitrary", "arbitrary", "arbitrary"),
        name="moe_swiglu",
    )(x, comb, w_gate, w_up, w_down)


def _shift_rows(x, carry_row):
    rolled = pltpu.roll(x, 1, axis=0)
    row = lax.broadcasted_iota(jnp.int32, x.shape, 0)
    return jnp.where(row == 0, carry_row, rolled)


def _head_sums(x, seg):
    parts = [_dot(x[:, c:c + GROUP_LANES].astype(BF16), seg)
             for c in range(0, x.shape[1], GROUP_LANES)]
    return parts[0] if len(parts) == 1 else jnp.concatenate(parts, axis=1)


def _prep_kernel(*refs, has_vres, tm, low):
    (r_ref, k_ref, v_ref, p_ref) = refs[:4]
    pos = 4
    if has_vres:
        vfirst_ref = refs[pos]
        pos += 1
    (mu_ref, w0_ref, w2_ref, a0_ref, a2_ref, g2_ref) = refs[pos:pos + 6]
    pos += 6
    if has_vres:
        v0_ref, v2_ref = refs[pos:pos + 2]
        pos += 2
    (kk_ref, ka_ref, rk_ref, seg_ref, tri_ref) = refs[pos:pos + 5]
    pos += 5
    (rt_ref, kt_ref, at_ref, bt_ref, vb_ref, g_ref, bonus_ref, wl_ref) = refs[pos:pos + 8]
    pos += 8
    if not has_vres:
        vf_ref = refs[pos]
        pos += 1
    carry_rkv, carry_p = refs[pos:pos + 2]
    rw = r_ref.shape[-1]
    n_low = p_ref.shape[-1] // 2

    @pl.when(pl.program_id(1) == 0)
    def _():
        carry_rkv[...] = jnp.zeros_like(carry_rkv)
        carry_p[...] = jnp.zeros_like(carry_p)

    def lerp_prev(ref, idx):
        cur = ref[0]
        prev = _shift_rows(cur, carry_rkv[0:1, idx * rw:(idx + 1) * rw])
        carry_rkv[0:1, idx * rw:(idx + 1) * rw] = cur[tm - 1:tm, :]
        return cur + (prev - cur) * mu_ref[idx:idx + 1, :]

    r = lerp_prev(r_ref, 0)
    k = lerp_prev(k_ref, 1)
    v = lerp_prev(v_ref, 2)

    p = p_ref[0]
    p_b = p[:, n_low:]
    lowr = p[:, :n_low] + _shift_rows(p_b, carry_p[0:1, :])
    carry_p[0:1, :] = p_b[tm - 1:tm, :]
    o_w, o_a, o_g, o_v = low

    def low_slice(o):
        return lowr[:, o[0]:o[0] + o[1]]

    zw = w0_ref[...] + _dot(jnp.tanh(low_slice(o_w)).astype(BF16), w2_ref[...])
    logw = -jnp.exp(jnp.float32(-0.5)) * jax.nn.sigmoid(zw)
    asig = jax.nn.sigmoid(a0_ref[...] + _dot(low_slice(o_a).astype(BF16), a2_ref[...]))
    g_ref[0] = _dot(jax.nn.sigmoid(low_slice(o_g)).astype(BF16), g2_ref[...])
    if has_vres:
        vgate = jax.nn.sigmoid(v0_ref[...] + _dot(low_slice(o_v).astype(BF16), v2_ref[...]))
        v = v + (vfirst_ref[0] - v) * vgate
    else:
        vf_ref[0] = v

    seg = seg_ref[...]
    kk = k * kk_ref[...]
    kk = kk / jnp.maximum(jnp.sqrt(_head_sums(kk * kk, seg)), 1e-12)
    k = k * (1.0 + (asig - 1.0) * ka_ref[...])
    bonus_ref[0] = _head_sums(r * k * rk_ref[...], seg) * v

    cum = jnp.dot(tri_ref[...], logw, preferred_element_type=F32, precision=lax.Precision.HIGHEST)
    e_pos = jnp.exp(cum)
    e_neg = jnp.exp(-cum)
    rt_ref[0] = (r * e_pos).astype(BF16)
    kt_ref[0] = (k * e_neg).astype(BF16)
    bt_ref[0] = (kk * asig * e_neg).astype(BF16)
    at_ref[0] = (-kk * jnp.exp(cum - logw)).astype(BF16)
    vb_ref[0] = v.astype(BF16)
    ends = [cum[c * CHUNK + CHUNK - 1:c * CHUNK + CHUNK, :] for c in range(tm // CHUNK)]
    if tm // CHUNK < SUBLANE:
        ends.append(jnp.zeros((SUBLANE - tm // CHUNK, rw), F32))
    wl_ref[0, 0] = jnp.exp(jnp.concatenate(ends, axis=0))


def rwkv_prep(h3, rw, low, low_blk, lp, vfirst, tm):
    bsz, s, cols = h3.shape
    n_low2 = cols // (low_blk + 1)
    assert n_low2 * (low_blk + 1) == cols and tm % CHUNK == 0 and tm // CHUNK <= SUBLANE
    has_vres = vfirst is not None
    blk = lambda j: pl.BlockSpec((1, tm, rw), lambda b, i, j=j: (b, i, j))
    full = lambda a: pl.BlockSpec(a.shape, lambda b, i: (0,) * a.ndim)
    row = lambda a: a.reshape(1, -1)
    args = [h3, h3, h3, h3]
    in_specs = [blk(0), blk(1), blk(2),
                pl.BlockSpec((1, tm, n_low2), lambda b, i: (b, i, low_blk))]
    if has_vres:
        args.append(vfirst)
        in_specs.append(blk(0))
    seg = (lax.broadcasted_iota(jnp.int32, (GROUP_LANES, GROUP_LANES), 0) // HEAD ==
           lax.broadcasted_iota(jnp.int32, (GROUP_LANES, GROUP_LANES), 1) // HEAD).astype(BF16)
    ti = lax.broadcasted_iota(jnp.int32, (tm, tm), 0)
    tj = lax.broadcasted_iota(jnp.int32, (tm, tm), 1)
    tri = ((ti >= tj) & (ti // CHUNK == tj // CHUNK)).astype(F32)
    small = [lp['mu_rkv'].reshape(3, rw), row(lp['w0']), lp['w2p'], row(lp['a0']), lp['a2p'],
             lp['g2p']]
    if has_vres:
        small += [row(lp['v0']), lp['v2p']]
    small += [row(lp['k_k']), row(lp['k_a']), row(lp['r_k']), seg, tri]
    args += small
    in_specs += [full(a) for a in small]
    tok = lambda dt: jax.ShapeDtypeStruct((bsz, s, rw), dt)
    out_shape = [tok(BF16)] * 5 + [tok(F32), tok(F32),
                                   jax.ShapeDtypeStruct((bsz, s // tm, SUBLANE, rw), F32)]
    out_specs = [blk(0)] * 7 + [pl.BlockSpec((1, 1, SUBLANE, rw), lambda b, i: (b, i, 0, 0))]
    if not has_vres:
        out_shape.append(tok(F32))
        out_specs.append(blk(0))
    return pl.pallas_call(
        functools.partial(_prep_kernel, has_vres=has_vres, tm=tm, low=low),
        grid=(bsz, s // tm),
        in_specs=in_specs,
        out_specs=out_specs,
        out_shape=out_shape,
        scratch_shapes=[pltpu.VMEM((SUBLANE, 3 * rw), F32), pltpu.VMEM((SUBLANE, n_low2 // 2), F32)],
        compiler_params=_params("arbitrary", "arbitrary"),
        name="rwkv_prep",
    )(*args)


def _expand_heads(x, head_mask):
    return jnp.where(head_mask, jnp.concatenate([x] * HEADS_PER_GROUP, axis=0),
                     jnp.zeros((), x.dtype))


def _scan_kernel(rt_ref, kt_ref, at_ref, bt_ref, v_ref, g_ref, bonus_ref, wl_ref,
                 lng_ref, lnb_ref, o_ref, state_ref, *, n_chunks):
    rows = HEADS_PER_GROUP * CHUNK
    ri = lax.broadcasted_iota(jnp.int32, (rows, rows), 0)
    ci = lax.broadcasted_iota(jnp.int32, (rows, rows), 1)
    strict = ri > ci
    incl = ri >= ci
    eye = (ri == ci).astype(F32)
    hm_r = lax.broadcasted_iota(jnp.int32, (rows, GROUP_LANES), 0) // CHUNK
    hm_c = lax.broadcasted_iota(jnp.int32, (rows, GROUP_LANES), 1) // HEAD
    head_mask = hm_r == hm_c
    avg = jnp.where(ri // HEAD == ci // HEAD, 1.0 / HEAD, 0.0).astype(BF16)[:GROUP_LANES, :GROUP_LANES]
    nt = (((1,), (1,)), ((), ()))
    tn = (((0,), (0,)), ((), ()))

    @pl.when(pl.program_id(2) == 0)
    def _():
        state_ref[...] = jnp.zeros_like(state_ref)

    for c in range(n_chunks):
        sl = slice(c * CHUNK, (c + 1) * CHUNK)
        a_bd = _expand_heads(at_ref[0, sl, :], head_mask)
        r_bd = _expand_heads(rt_ref[0, sl, :], head_mask)
        b_bd = _expand_heads(bt_ref[0, sl, :], head_mask)
        k_bd = _expand_heads(kt_ref[0, sl, :], head_mask)
        v_bd = _expand_heads(v_ref[0, sl, :], head_mask)
        ar = jnp.concatenate([a_bd, r_bd], axis=0)
        bk = jnp.concatenate([b_bd, k_bd], axis=0)
        p = lax.dot_general(ar, bk, nt, preferred_element_type=F32)
        a_ab = jnp.where(strict, p[:rows, :rows], 0.0)
        a_ak = jnp.where(strict, p[:rows, rows:], 0.0).astype(BF16)
        m_rb = jnp.where(incl, p[rows:, :rows], 0.0).astype(BF16)
        m_rk = jnp.where(incl, p[rows:, rows:], 0.0).astype(BF16)
        t_inv = eye + a_ab
        pw = a_ab.astype(BF16)
        step = 2
        while step < CHUNK:
            pw_f = _dot(pw, pw)
            pw = pw_f.astype(BF16)
            t_inv = t_inv + _dot(t_inv.astype(BF16), pw)
            step *= 2
        t_inv = t_inv.astype(BF16)

        state = state_ref[...]
        ars = lax.dot_general(ar, state.astype(BF16), nt, preferred_element_type=F32)
        x0 = ars[:rows] + _dot(a_ak, v_bd)
        u = _dot(t_inv, x0.astype(BF16)).astype(BF16)
        uv = jnp.concatenate([u, v_bd], axis=0)
        y_bd = ars[rows:] + _dot(jnp.concatenate([m_rb, m_rk], axis=1), uv)
        state_ref[...] = ((state + lax.dot_general(uv, bk, tn, preferred_element_type=F32))
                          * wl_ref[0, 0, c:c + 1, :])
        y = y_bd[0:CHUNK]
        for h in range(1, HEADS_PER_GROUP):
            y = y + y_bd[h * CHUNK:(h + 1) * CHUNK]
        dlt = y - _dot(y.astype(BF16), avg)
        var = _dot((dlt * dlt).astype(BF16), avg)
        yn = dlt * lax.rsqrt(var + GN_EPS) * lng_ref[...] + lnb_ref[...]
        o_ref[0, sl, :] = ((yn + bonus_ref[0, sl, :]) * g_ref[0, sl, :]).astype(o_ref.dtype)


def wkv_scan(rt, kt, at, bt, vb, g, bonus, wl, lnx_g, lnx_b, tm):
    bsz, s, rw = rt.shape
    n_groups = rw // GROUP_LANES
    blk = pl.BlockSpec((1, tm, GROUP_LANES), lambda b, hg, i: (b, i, hg))
    vec = pl.BlockSpec((1, GROUP_LANES), lambda b, hg, i: (0, hg))
    return pl.pallas_call(
        functools.partial(_scan_kernel, n_chunks=tm // CHUNK),
        grid=(bsz, n_groups, s // tm),
        in_specs=[blk] * 7 + [pl.BlockSpec((1, 1, SUBLANE, GROUP_LANES),
                                           lambda b, hg, i: (b, i, 0, hg)), vec, vec],
        out_specs=blk,
        out_shape=jax.ShapeDtypeStruct((bsz, s, rw), BF16),
        scratch_shapes=[pltpu.VMEM((GROUP_LANES, GROUP_LANES), F32)],
        compiler_params=_params("arbitrary", "arbitrary", "arbitrary"),
        name="wkv_scan",
    )(rt, kt, at, bt, vb, g, bonus, wl, lnx_g.reshape(1, rw), lnx_b.reshape(1, rw))


def _conv_kernel(val_ref, gate_ref, w_ref, b_ref, lg_ref, lb_ref, o_ref, ext_ref, *, tm, taps, halo):
    @pl.when(pl.program_id(1) == 0)
    def _():
        ext_ref[0:halo, :] = jnp.zeros((halo, ext_ref.shape[1]), F32)

    u = val_ref[0] * jax.nn.sigmoid(gate_ref[0])
    ext_ref[halo:halo + tm, :] = u
    acc = jnp.zeros_like(u) + b_ref[...]
    for j in range(taps):
        off = halo - (taps - 1) + j
        acc = acc + w_ref[j:j + 1, :] * ext_ref[off:off + tm, :]
    ext_ref[0:halo, :] = u[tm - halo:tm, :]
    mu = jnp.mean(acc, axis=-1, keepdims=True)
    d = acc - mu
    var = jnp.mean(d * d, axis=-1, keepdims=True)
    z = d * lax.rsqrt(var + LN_EPS) * lg_ref[...] + lb_ref[...]
    o_ref[0] = (z * jax.nn.sigmoid(z)).astype(o_ref.dtype)


def conv_group(h3, rw, cw, conv_w, conv_b, ln_g, ln_b):
    bsz, s, _ = h3.shape
    taps = conv_w.shape[0]
    halo = _round_up(taps - 1, SUBLANE)
    tm = _pick(s, 128)
    assert (3 * rw) % cw == 0 and tm >= halo
    c0 = (3 * rw) // cw
    row = pl.BlockSpec((1, cw), lambda b, i: (0, 0))
    return pl.pallas_call(
        functools.partial(_conv_kernel, tm=tm, taps=taps, halo=halo),
        grid=(bsz, s // tm),
        in_specs=[pl.BlockSpec((1, tm, cw), lambda b, i: (b, i, c0)),
                  pl.BlockSpec((1, tm, cw), lambda b, i: (b, i, c0 + 1)),
                  pl.BlockSpec((taps, cw), lambda b, i: (0, 0)), row, row, row],
        out_specs=pl.BlockSpec((1, tm, cw), lambda b, i: (b, i, 0)),
        out_shape=jax.ShapeDtypeStruct((bsz, s, cw), BF16),
        scratch_shapes=[pltpu.VMEM((halo + tm, cw), F32)],
        compiler_params=_params("arbitrary", "arbitrary"),
        name="conv_group",
    )(h3, h3, conv_w, conv_b.reshape(1, cw), ln_g.reshape(1, cw), ln_b.reshape(1, cw))


def _pad_cols(w, n):
    return jnp.pad(w, ((0, 0), (0, n - w.shape[1])))


def _pad_rows(w, n):
    return jnp.pad(w, ((0, n - w.shape[0]), (0, 0)))


def _mixer_weights(lp, has_vres):
    names = [('w', 'mu_w', 'w1', 'w2'), ('a', 'mu_a', 'a1', 'a2'), ('g', 'mu_g', 'g1', 'g2')]
    if has_vres:
        names.append(('v', 'mu_v', 'v1', 'v2'))
    cur, prev, low, off = [], [], [], 0
    out = dict(lp)
    for tag, mu, w1, w2 in names:
        rank = lp[w1].shape[1]
        rpad = _round_up(rank, LANE)
        cur.append(_pad_cols((1.0 - lp[mu])[:, None] * lp[w1], rpad))
        prev.append(_pad_cols(lp[mu][:, None] * lp[w1], rpad))
        out[w2 + 'p'] = _pad_rows(lp[w2], rpad).astype(BF16)
        low.append((off, rpad))
        off += rpad
    if not has_vres:
        low.append((0, 0))
    in_cols = lp['w_in'].shape[1]
    low_start = _round_up(in_cols, 2 * off)
    out['w_big'] = jnp.concatenate([_pad_cols(lp['w_in'], low_start)] + cur + prev,
                                   axis=1).astype(BF16)
    return out, tuple(low), low_start // (2 * off)


def _mixer(xm, x, lp, mods, vfirst, nxt, router):
    bsz, s, d = x.shape
    rw = lp['w0'].shape[0]
    cw = lp['conv_b'].shape[0]
    has_vres = vfirst is not None
    lw, low, low_blk = _mixer_weights(lp, has_vres)
    n_cols = lw['w_big'].shape[1]
    h = matmul([xm.reshape(bsz * s, d)], [lw['w_big']], F32, tn_pref=n_cols // (low_blk + 1),
               name="in_projection")
    h3 = h.reshape(bsz, s, n_cols)
    tm = _pick(s, 256)
    outs = rwkv_prep(h3, rw, low, low_blk, lw, vfirst, tm)
    rt, kt, at, bt, vb, g, bonus, wl = outs[:8]
    v_out = vfirst if has_vres else outs[8]
    y_rwkv = wkv_scan(rt, kt, at, bt, vb, g, bonus, wl, lp['lnx_g'], lp['lnx_b'], tm)
    u = conv_group(h3, rw, cw, lp['conv_w'], lp['conv_b'], lp['conv_ln_g'], lp['conv_ln_b'])
    w_out = lp['w_out'].astype(BF16)
    o = matmul([y_rwkv.reshape(bsz * s, rw), u.reshape(bsz * s, cw)], [w_out[:rw], w_out[rw:]],
               F32, name="out_projection")
    res = sublayer_epilogue(o.reshape(bsz, s, d), x, lp['mix_post_g'], mods['gt_m'], nxt, router)
    return res, v_out


def _split_layer(p, prefix, has_vres):
    keys = ['ada_w', 'ada_b', 'mix_pre_g', 'mix_post_g', 'w_in', 'mu_rkv', 'mu_w', 'mu_a', 'mu_g',
            'w0', 'w1', 'w2', 'a0', 'a1', 'a2', 'g1', 'g2', 'k_k', 'k_a', 'r_k', 'lnx_g', 'lnx_b',
            'conv_w', 'conv_b', 'conv_ln_g', 'conv_ln_b', 'w_out', 'ffn_pre_g', 'ffn_post_g']
    if has_vres:
        keys += ['mu_v', 'v0', 'v1', 'v2']
    return {k: p[prefix + k] for k in keys}


def _forward(p):
    x = p['x']
    c = p['c']
    bsz, s, d = x.shape
    layers = [_split_layer(p, 'l0_', False), _split_layer(p, 'l1_', True)]
    mods = []
    for lp in layers:
        mod = ada_modulation(c, lp['ada_w'], lp['ada_b'])
        mods.append(dict(zip(['sh_m', 'sc_m', 'gt_m', 'sh_f', 'sc_f', 'gt_f'],
                             jnp.split(mod, 6, axis=-1))))

    lp, md = layers[0], mods[0]
    xm = norm_modulate(x, lp['mix_pre_g'], md['sh_m'], md['sc_m'])
    (x, xf), vfirst = _mixer(xm, x, lp, md, None, (lp['ffn_pre_g'], md['sh_f'], md['sc_f']), None)
    y = dense_swiglu(xf.reshape(bsz * s, d), p['l0_ffn_w_gate'].astype(BF16),
                     p['l0_ffn_w_up'].astype(BF16), p['l0_ffn_w_down'].astype(BF16))
    nlp, nmd = layers[1], mods[1]
    x, xm = sublayer_epilogue(y.reshape(bsz, s, d), x, lp['ffn_post_g'], md['gt_f'],
                              (nlp['mix_pre_g'], nmd['sh_m'], nmd['sc_m']))

    lp, md = layers[1], mods[1]
    (x, xf, comb), _ = _mixer(xm, x, lp, md, vfirst, (lp['ffn_pre_g'], md['sh_f'], md['sc_f']),
                              (p['l1_router_w'], p['l1_router_b']))
    y = moe_swiglu(xf.reshape(bsz * s, d), comb.reshape(bsz * s, LANE),
                   p['l1_moe_w_gate'].astype(BF16), p['l1_moe_w_up'].astype(BF16),
                   p['l1_moe_w_down'].astype(BF16))
    (x,) = sublayer_epilogue(y.reshape(bsz, s, d), x, lp['ffn_post_g'], md['gt_f'])
    return x


_ARG_NAMES = (
    'x', 'c',
    'l0_ada_w', 'l0_ada_b', 'l0_mix_pre_g', 'l0_mix_post_g', 'l0_w_in', 'l0_mu_rkv', 'l0_mu_w',
    'l0_mu_a', 'l0_mu_g', 'l0_w0', 'l0_w1', 'l0_w2', 'l0_a0', 'l0_a1', 'l0_a2', 'l0_g1', 'l0_g2',
    'l0_k_k', 'l0_k_a', 'l0_r_k', 'l0_lnx_g', 'l0_lnx_b', 'l0_conv_w', 'l0_conv_b', 'l0_conv_ln_g',
    'l0_conv_ln_b', 'l0_w_out',
    'l0_ffn_pre_g', 'l0_ffn_post_g', 'l0_ffn_w_gate', 'l0_ffn_w_up', 'l0_ffn_w_down',
    'l1_ada_w', 'l1_ada_b', 'l1_mix_pre_g', 'l1_mix_post_g', 'l1_w_in', 'l1_mu_rkv', 'l1_mu_w',
    'l1_mu_a', 'l1_mu_g', 'l1_w0', 'l1_w1', 'l1_w2', 'l1_a0', 'l1_a1', 'l1_a2', 'l1_g1', 'l1_g2',
    'l1_k_k', 'l1_k_a', 'l1_r_k', 'l1_lnx_g', 'l1_lnx_b', 'l1_conv_w', 'l1_conv_b', 'l1_conv_ln_g',
    'l1_conv_ln_b', 'l1_w_out', 'l1_mu_v', 'l1_v0', 'l1_v1', 'l1_v2',
    'l1_ffn_pre_g', 'l1_ffn_post_g', 'l1_router_w', 'l1_router_b', 'l1_moe_w_gate',
    'l1_moe_w_up', 'l1_moe_w_down')


def kernel(x, c, l0_ada_w, l0_ada_b, l0_mix_pre_g, l0_mix_post_g, l0_w_in, l0_mu_rkv, l0_mu_w, l0_mu_a, l0_mu_g, l0_w0, l0_w1, l0_w2, l0_a0, l0_a1, l0_a2, l0_g1, l0_g2, l0_k_k, l0_k_a, l0_r_k, l0_lnx_g, l0_lnx_b, l0_conv_w, l0_conv_b, l0_conv_ln_g, l0_conv_ln_b, l0_w_out, l0_ffn_pre_g, l0_ffn_post_g, l0_ffn_w_gate, l0_ffn_w_up, l0_ffn_w_down, l1_ada_w, l1_ada_b, l1_mix_pre_g, l1_mix_post_g, l1_w_in, l1_mu_rkv, l1_mu_w, l1_mu_a, l1_mu_g, l1_w0, l1_w1, l1_w2, l1_a0, l1_a1, l1_a2, l1_g1, l1_g2, l1_k_k, l1_k_a, l1_r_k, l1_lnx_g, l1_lnx_b, l1_conv_w, l1_conv_b, l1_conv_ln_g, l1_conv_ln_b, l1_w_out, l1_mu_v, l1_v0, l1_v1, l1_v2, l1_ffn_pre_g, l1_ffn_post_g, l1_router_w, l1_router_b, l1_moe_w_gate, l1_moe_w_up, l1_moe_w_down):
    args = (x, c, l0_ada_w, l0_ada_b, l0_mix_pre_g, l0_mix_post_g, l0_w_in, l0_mu_rkv, l0_mu_w, l0_mu_a, l0_mu_g, l0_w0, l0_w1, l0_w2, l0_a0, l0_a1, l0_a2, l0_g1, l0_g2, l0_k_k, l0_k_a, l0_r_k, l0_lnx_g, l0_lnx_b, l0_conv_w, l0_conv_b, l0_conv_ln_g, l0_conv_ln_b, l0_w_out, l0_ffn_pre_g, l0_ffn_post_g, l0_ffn_w_gate, l0_ffn_w_up, l0_ffn_w_down, l1_ada_w, l1_ada_b, l1_mix_pre_g, l1_mix_post_g, l1_w_in, l1_mu_rkv, l1_mu_w, l1_mu_a, l1_mu_g, l1_w0, l1_w1, l1_w2, l1_a0, l1_a1, l1_a2, l1_g1, l1_g2, l1_k_k, l1_k_a, l1_r_k, l1_lnx_g, l1_lnx_b, l1_conv_w, l1_conv_b, l1_conv_ln_g, l1_conv_ln_b, l1_w_out, l1_mu_v, l1_v0, l1_v1, l1_v2, l1_ffn_pre_g, l1_ffn_post_g, l1_router_w, l1_router_b, l1_moe_w_gate, l1_moe_w_up, l1_moe_w_down)
    return _forward(dict(zip(_ARG_NAMES, args)))
```

```python
import functools

import jax
import jax.numpy as jnp
from jax import lax
from jax.experimental import pallas as pl
from jax.experimental.pallas import tpu as pltpu

F32 = jnp.float32
BF16 = jnp.bfloat16

HEAD = 64
HEADS_PER_GROUP = 4
GROUP_LANES = HEAD * HEADS_PER_GROUP
CHUNK = 64
SCAN_GROUPS = 2
LANE = 128
SUBLANE = 8
TOP_K = 2
RMS_EPS = 1e-6
GN_EPS = 64e-5
LN_EPS = 1e-5
VMEM_LIMIT_BYTES = 56 * 1024 * 1024


def _params(*sem):
    return pltpu.CompilerParams(dimension_semantics=sem, vmem_limit_bytes=VMEM_LIMIT_BYTES)


def _round_up(n, m):
    return (n + m - 1) // m * m


def _pick(n, pref):
    if n <= pref:
        return n
    t = pref
    while t >= LANE:
        if n % t == 0:
            return t
        t -= LANE
    return n


def _dot(a, b):
    return jnp.dot(a, b, preferred_element_type=F32)


def _rms(x, g):
    return x * lax.rsqrt(jnp.mean(x * x, axis=-1, keepdims=True) + RMS_EPS) * g


def _ada_kernel(c_ref, w_ref, b_ref, o_ref):
    c = c_ref[...]
    s = c * jax.nn.sigmoid(c)
    o_ref[...] = jnp.dot(s, w_ref[...], preferred_element_type=F32,
                         precision=lax.Precision.HIGHEST) + b_ref[...]


def ada_modulation(c, ada_w, ada_b):
    bsz, d = c.shape
    n = ada_w.shape[1]
    rows = _round_up(bsz, SUBLANE)
    cp = jnp.zeros((rows, d), F32).at[:bsz].set(c)
    tn = _pick(n, 512)
    out = pl.pallas_call(
        _ada_kernel,
        grid=(n // tn,),
        in_specs=[pl.BlockSpec((rows, d), lambda j: (0, 0)),
                  pl.BlockSpec((d, tn), lambda j: (0, j)),
                  pl.BlockSpec((1, tn), lambda j: (0, j))],
        out_specs=pl.BlockSpec((rows, tn), lambda j: (0, j)),
        out_shape=jax.ShapeDtypeStruct((rows, n), F32),
        compiler_params=_params("arbitrary"),
        name="ada_matvec",
    )(cp, ada_w, ada_b.reshape(1, n))
    return out[:bsz]


def _norm_mod_kernel(x_ref, g_ref, sh_ref, sc_ref, o_ref):
    xn = _rms(x_ref[0], g_ref[...])
    o_ref[0] = (xn * (1.0 + sc_ref[0]) + sh_ref[0]).astype(o_ref.dtype)


def norm_modulate(x, g, shift, scale):
    bsz, s, d = x.shape
    tm = _pick(s, 512)
    vec = pl.BlockSpec((1, 1, d), lambda b, i: (b, 0, 0))
    return pl.pallas_call(
        _norm_mod_kernel,
        grid=(bsz, s // tm),
        in_specs=[pl.BlockSpec((1, tm, d), lambda b, i: (b, i, 0)),
                  pl.BlockSpec((1, d), lambda b, i: (0, 0)), vec, vec],
        out_specs=pl.BlockSpec((1, tm, d), lambda b, i: (b, i, 0)),
        out_shape=jax.ShapeDtypeStruct((bsz, s, d), BF16),
        compiler_params=_params("arbitrary", "arbitrary"),
        name="norm_modulate",
    )(x, g.reshape(1, d), shift.reshape(bsz, 1, d), scale.reshape(bsz, 1, d))


def _mm_kernel(*refs, n_seg):
    o_ref = refs[2 * n_seg]
    acc = _dot(refs[0][...], refs[n_seg][...])
    for s in range(1, n_seg):
        acc = acc + _dot(refs[s][...], refs[n_seg + s][...])
    o_ref[...] = acc.astype(o_ref.dtype)


def matmul(a_list, w_list, out_dtype, tm_pref=1024, tn_pref=1024, name="matmul"):
    m = a_list[0].shape[0]
    n = w_list[0].shape[1]
    tm = _pick(m, tm_pref)
    tn = _pick(n, tn_pref)
    n_seg = len(a_list)
    in_specs = [pl.BlockSpec((tm, a.shape[1]), lambda i, j: (i, 0)) for a in a_list]
    in_specs += [pl.BlockSpec((w.shape[0], tn), lambda i, j: (0, j)) for w in w_list]
    return pl.pallas_call(
        functools.partial(_mm_kernel, n_seg=n_seg),
        grid=(m // tm, n // tn),
        in_specs=in_specs,
        out_specs=pl.BlockSpec((tm, tn), lambda i, j: (i, j)),
        out_shape=jax.ShapeDtypeStruct((m, n), out_dtype),
        compiler_params=_params("arbitrary", "arbitrary"),
        name=name,
    )(*a_list, *w_list)


ROUTE_E1, ROUTE_E2, ROUTE_W1, ROUTE_W2 = 0, 1, 2, 3


def _top2_route(logits, n_experts):
    lane = lax.broadcasted_iota(jnp.int32, logits.shape, 1)
    neg = jnp.float32(-jnp.inf)
    lg = jnp.where(lane < n_experts, logits, neg)
    v1 = jnp.max(lg, axis=-1, keepdims=True)
    i1 = jnp.min(jnp.where(lg == v1, lane, LANE), axis=-1, keepdims=True)
    lg2 = jnp.where(lane == i1, neg, lg)
    v2 = jnp.max(lg2, axis=-1, keepdims=True)
    i2 = jnp.min(jnp.where(lg2 == v2, lane, LANE), axis=-1, keepdims=True)
    e2 = jnp.exp(v2 - v1)
    w1 = 1.0 / (1.0 + e2)
    w2 = e2 / (1.0 + e2)
    rec = jnp.where(lane == ROUTE_E1, i1.astype(F32), 0.0)
    rec = jnp.where(lane == ROUTE_E2, i2.astype(F32), rec)
    rec = jnp.where(lane == ROUTE_W1, w1, rec)
    return jnp.where(lane == ROUTE_W2, w2, rec)


def _epilogue_kernel(*refs, has_next, n_experts):
    y_ref, x_ref, post_g_ref, gate_ref = refs[:4]
    pos = 4
    if has_next:
        pre_g_ref, sh_ref, sc_ref = refs[pos:pos + 3]
        pos += 3
    if n_experts:
        rw_ref, rb_ref = refs[pos:pos + 2]
        pos += 2
    outs = refs[pos:]
    x1 = x_ref[0] + gate_ref[0] * _rms(y_ref[0], post_g_ref[...])
    outs[0][0] = x1
    if has_next:
        xf = _rms(x1, pre_g_ref[...]) * (1.0 + sc_ref[0]) + sh_ref[0]
        outs[1][0] = xf.astype(outs[1].dtype)
        if n_experts:
            logits = jnp.dot(xf, rw_ref[...], preferred_element_type=F32,
                             precision=lax.Precision.HIGHEST) + rb_ref[...]
            outs[2][0] = _top2_route(logits, n_experts)


def sublayer_epilogue(y, x, post_g, gate, nxt=None, router=None):
    bsz, s, d = x.shape
    tm = _pick(s, 256)
    blk = pl.BlockSpec((1, tm, d), lambda b, i: (b, i, 0))
    row = pl.BlockSpec((1, d), lambda b, i: (0, 0))
    vec = pl.BlockSpec((1, 1, d), lambda b, i: (b, 0, 0))
    args = [y, x, post_g.reshape(1, d), gate.reshape(bsz, 1, d)]
    in_specs = [blk, blk, row, vec]
    out_shape = [jax.ShapeDtypeStruct((bsz, s, d), F32)]
    out_specs = [blk]
    n_experts = 0
    if nxt is not None:
        pre_g, shift, scale = nxt
        args += [pre_g.reshape(1, d), shift.reshape(bsz, 1, d), scale.reshape(bsz, 1, d)]
        in_specs += [row, vec, vec]
        out_shape.append(jax.ShapeDtypeStruct((bsz, s, d), BF16 if router is None else F32))
        out_specs.append(blk)
        if router is not None:
            router_w, router_b = router
            n_experts = router_w.shape[1]
            rw = jnp.zeros((d, LANE), F32).at[:, :n_experts].set(router_w)
            rb = jnp.zeros((1, LANE), F32).at[0, :n_experts].set(router_b)
            args += [rw, rb]
            in_specs += [pl.BlockSpec((d, LANE), lambda b, i: (0, 0)),
                         pl.BlockSpec((1, LANE), lambda b, i: (0, 0))]
            out_shape.append(jax.ShapeDtypeStruct((bsz, s, LANE), F32))
            out_specs.append(pl.BlockSpec((1, tm, LANE), lambda b, i: (b, i, 0)))
    return pl.pallas_call(
        functools.partial(_epilogue_kernel, has_next=nxt is not None, n_experts=n_experts),
        grid=(bsz, s // tm),
        in_specs=in_specs,
        out_specs=out_specs,
        out_shape=out_shape,
        compiler_params=_params("arbitrary", "arbitrary"),
        name="sublayer_epilogue",
    )(*args)


def _ffn_kernel(x_ref, wg_ref, wu_ref, wd_ref, o_ref):
    @pl.when(pl.program_id(1) == 0)
    def _():
        o_ref[...] = jnp.zeros_like(o_ref)

    x = x_ref[...]
    hg = _dot(x, wg_ref[...])
    hu = _dot(x, wu_ref[...])
    h = (hg * jax.nn.sigmoid(hg) * hu).astype(BF16)
    o_ref[...] += _dot(h, wd_ref[...])


def dense_swiglu(x, w_gate, w_up, w_down):
    m, d = x.shape
    f = w_gate.shape[1]
    tm = _pick(m, 512)
    tf = _pick(f, 512)
    return pl.pallas_call(
        _ffn_kernel,
        grid=(m // tm, f // tf),
        in_specs=[pl.BlockSpec((tm, d), lambda i, j: (i, 0)),
                  pl.BlockSpec((d, tf), lambda i, j: (0, j)),
                  pl.BlockSpec((d, tf), lambda i, j: (0, j)),
                  pl.BlockSpec((tf, d), lambda i, j: (j, 0))],
        out_specs=pl.BlockSpec((tm, d), lambda i, j: (i, 0)),
        out_shape=jax.ShapeDtypeStruct((m, d), F32),
        compiler_params=_params("arbitrary", "arbitrary"),
        name="dense_swiglu",
    )(x, w_gate, w_up, w_down)


MOE_ROW_TILE = 512
DMA_ROWS = 256


def _rank_kernel(route_ref, tri_ref, rank_ref, count_ref, carry_ref):
    @pl.when(pl.program_id(0) == 0)
    def _():
        carry_ref[...] = jnp.zeros_like(carry_ref)

    route = route_ref[...]
    lane = lax.broadcasted_iota(jnp.int32, route.shape, 1)
    lane_f = lane.astype(F32)
    oh1 = lane_f == route[:, ROUTE_E1:ROUTE_E1 + 1]
    oh2 = lane_f == route[:, ROUTE_E2:ROUTE_E2 + 1]
    cnt = jnp.where(oh1 | oh2, 1.0, 0.0)
    before = _dot(tri_ref[...], cnt.astype(BF16)) + carry_ref[0:1, :]
    r1 = jnp.sum(jnp.where(oh1, before, 0.0), axis=-1, keepdims=True)
    r2 = jnp.sum(jnp.where(oh2, before, 0.0), axis=-1, keepdims=True)
    rank_ref[...] = jnp.where(lane == ROUTE_E1, r1, jnp.where(lane == ROUTE_E2, r2, 0.0))
    total = carry_ref[0:1, :] + jnp.sum(cnt, axis=0, keepdims=True)
    carry_ref[0:1, :] = total
    count_ref[...] = jnp.broadcast_to(total, count_ref.shape)


def expert_ranks(route):
    t = route.shape[0]
    tm = _pick(t, 256)
    ti = lax.broadcasted_iota(jnp.int32, (tm, tm), 0)
    tj = lax.broadcasted_iota(jnp.int32, (tm, tm), 1)
    tri = (ti > tj).astype(BF16)
    return pl.pallas_call(
        _rank_kernel,
        grid=(t // tm,),
        in_specs=[pl.BlockSpec((tm, LANE), lambda i: (i, 0)),
                  pl.BlockSpec((tm, tm), lambda i: (0, 0))],
        out_specs=[pl.BlockSpec((tm, LANE), lambda i: (i, 0)),
                   pl.BlockSpec((SUBLANE, LANE), lambda i: (0, 0))],
        out_shape=[jax.ShapeDtypeStruct((t, LANE), F32),
                   jax.ShapeDtypeStruct((SUBLANE, LANE), F32)],
        scratch_shapes=[pltpu.VMEM((SUBLANE, LANE), F32)],
        compiler_params=_params("arbitrary"),
        name="expert_ranks",
    )(route, tri)


def _row_copies(n_rows, make_copy):
    def start(r, carry):
        for cp in make_copy(r):
            cp.start()
        return carry

    def wait(r, carry):
        for cp in make_copy(r):
            cp.wait()
        return carry

    lax.fori_loop(0, n_rows, start, 0)
    lax.fori_loop(0, n_rows, wait, 0)


def _dispatch_kernel(pos1_ref, pos2_ref, x_ref, init_ref, o_ref, sem):
    del init_ref

    def make_copy(r):
        src = x_ref.at[pl.ds(r, 1)]
        return [pltpu.make_async_copy(src, o_ref.at[pl.ds(pos_ref[0, r], 1)], sem)
                for pos_ref in (pos1_ref, pos2_ref)]

    _row_copies(x_ref.shape[0], make_copy)


def moe_dispatch(x, pos1, pos2, n_rows):
    t, d = x.shape
    rows = pos1.shape[2]
    smem = pl.BlockSpec((None, 1, rows), lambda i: (i, 0, 0), memory_space=pltpu.SMEM)
    return pl.pallas_call(
        _dispatch_kernel,
        grid=(t // rows,),
        in_specs=[smem, smem, pl.BlockSpec((rows, d), lambda i: (i, 0)),
                  pl.BlockSpec(memory_space=pl.ANY)],
        out_specs=pl.BlockSpec(memory_space=pl.ANY),
        out_shape=jax.ShapeDtypeStruct((n_rows, d), F32),
        scratch_shapes=[pltpu.SemaphoreType.DMA(())],
        input_output_aliases={3: 0},
        compiler_params=_params("arbitrary"),
        name="moe_dispatch",
    )(pos1, pos2, x, jnp.zeros((n_rows, d), F32))


def _moe_kernel(tile_expert_ref, n_used_ref, x_ref, wg_ref, wu_ref, wd_ref, o_ref, xb_ref):
    del tile_expert_ref

    @pl.when(pl.program_id(1) == 0)
    def _():
        o_ref[...] = jnp.zeros_like(o_ref)
        xb_ref[...] = x_ref[...].astype(BF16)

    @pl.when(pl.program_id(0) < n_used_ref[0])
    def _():
        x = xb_ref[...]
        hg = _dot(x, wg_ref[...])
        hu = _dot(x, wu_ref[...])
        h = (hg * jax.nn.sigmoid(hg) * hu).astype(BF16)
        o_ref[...] += _dot(h, wd_ref[...])


def moe_grouped_swiglu(xs, tile_expert, n_used, w_gate, w_up, w_down):
    n_rows, d = xs.shape
    f = w_gate.shape[2]
    tf = _pick(f, 256)
    live = lambda j, k, te, nu: jnp.where(j < nu[0], k, 0)
    grid_spec = pltpu.PrefetchScalarGridSpec(
        num_scalar_prefetch=2,
        grid=(n_rows // MOE_ROW_TILE, f // tf),
        in_specs=[pl.BlockSpec((MOE_ROW_TILE, d), lambda j, k, te, nu: (j, 0)),
                  pl.BlockSpec((None, d, tf), lambda j, k, te, nu: (te[j], 0, live(j, k, te, nu))),
                  pl.BlockSpec((None, d, tf), lambda j, k, te, nu: (te[j], 0, live(j, k, te, nu))),
                  pl.BlockSpec((None, tf, d), lambda j, k, te, nu: (te[j], live(j, k, te, nu), 0))],
        out_specs=pl.BlockSpec((MOE_ROW_TILE, d), lambda j, k, te, nu: (j, 0)),
        scratch_shapes=[pltpu.VMEM((MOE_ROW_TILE, d), BF16)])
    return pl.pallas_call(
        _moe_kernel,
        grid_spec=grid_spec,
        out_shape=jax.ShapeDtypeStruct((n_rows, d), F32),
        compiler_params=_params("arbitrary", "arbitrary"),
        name="moe_grouped_swiglu",
    )(tile_expert, n_used, xs, w_gate, w_up, w_down)


def _combine_kernel(pos1_ref, pos2_ref, route_ref, x_ref, post_g_ref, gate_ref, ys_ref, o_ref,
                    buf1, buf2, sem):
    def make_copy(r):
        return [pltpu.make_async_copy(ys_ref.at[pl.ds(pos_ref[0, r], 1)], buf.at[pl.ds(r, 1)], sem)
                for pos_ref, buf in ((pos1_ref, buf1), (pos2_ref, buf2))]

    _row_copies(buf1.shape[0], make_copy)
    route = route_ref[0]
    y = (route[:, ROUTE_W1:ROUTE_W1 + 1] * buf1[...] + route[:, ROUTE_W2:ROUTE_W2 + 1] * buf2[...])
    o_ref[0] = x_ref[0] + gate_ref[0] * _rms(y, post_g_ref[...])


def moe_combine_epilogue(ys, pos1, pos2, route, x, post_g, gate):
    bsz, s, d = x.shape
    rows = pos1.shape[2]
    n_i = s // rows
    smem = pl.BlockSpec((None, 1, rows), lambda b, i: (b * n_i + i, 0, 0),
                        memory_space=pltpu.SMEM)
    blk = pl.BlockSpec((1, rows, d), lambda b, i: (b, i, 0))
    return pl.pallas_call(
        _combine_kernel,
        grid=(bsz, n_i),
        in_specs=[smem, smem, pl.BlockSpec((1, rows, LANE), lambda b, i: (b, i, 0)), blk,
                  pl.BlockSpec((1, d), lambda b, i: (0, 0)),
                  pl.BlockSpec((1, 1, d), lambda b, i: (b, 0, 0)),
                  pl.BlockSpec(memory_space=pl.ANY)],
        out_specs=blk,
        out_shape=jax.ShapeDtypeStruct((bsz, s, d), F32),
        scratch_shapes=[pltpu.VMEM((rows, d), F32), pltpu.VMEM((rows, d), F32),
                        pltpu.SemaphoreType.DMA(())],
        compiler_params=_params("arbitrary", "arbitrary"),
        name="moe_combine_epilogue",
    )(pos1, pos2, route, x, post_g.reshape(1, d), gate.reshape(bsz, 1, d), ys)


def moe_sublayer(xf, route, x, post_g, gate, w_gate, w_up, w_down):
    bsz, s, d = x.shape
    t = bsz * s
    n_e = w_gate.shape[0]
    rows = _pick(s, DMA_ROWS)
    assert (TOP_K * t) % MOE_ROW_TILE == 0
    n_tiles = TOP_K * t // MOE_ROW_TILE + n_e
    route2 = route.reshape(t, LANE)
    rank, count = expert_ranks(route2)
    counts = count[0, :n_e].astype(jnp.int32)
    tiles_e = (counts + MOE_ROW_TILE - 1) // MOE_ROW_TILE
    tile_end = jnp.cumsum(tiles_e)
    row_start = (tile_end - tiles_e) * MOE_ROW_TILE
    e1 = route2[:, ROUTE_E1].astype(jnp.int32)
    e2 = route2[:, ROUTE_E2].astype(jnp.int32)
    pos1 = (row_start[e1] + rank[:, ROUTE_E1].astype(jnp.int32)).reshape(t // rows, 1, rows)
    pos2 = (row_start[e2] + rank[:, ROUTE_E2].astype(jnp.int32)).reshape(t // rows, 1, rows)
    tile_expert = jnp.minimum(jnp.searchsorted(tile_end, jnp.arange(n_tiles), side='right'),
                              n_e - 1).astype(jnp.int32)
    n_used = tile_end[n_e - 1:].astype(jnp.int32)
    xs = moe_dispatch(xf.reshape(t, d), pos1, pos2, n_tiles * MOE_ROW_TILE)
    ys = moe_grouped_swiglu(xs, tile_expert, n_used, w_gate, w_up, w_down)
    return moe_combine_epilogue(ys, pos1, pos2, route, x, post_g, gate)


def _shift_rows(x, carry_row):
    rolled = pltpu.roll(x, 1, axis=0)
    row = lax.broadcasted_iota(jnp.int32, x.shape, 0)
    return jnp.where(row == 0, carry_row, rolled)


def _head_sums(x, seg):
    parts = [_dot(x[:, c:c + GROUP_LANES].astype(BF16), seg)
             for c in range(0, x.shape[1], GROUP_LANES)]
    return parts[0] if len(parts) == 1 else jnp.concatenate(parts, axis=1)


def _prep_kernel(*refs, has_vres, tm, low):
    (r_ref, k_ref, v_ref, p_ref) = refs[:4]
    pos = 4
    if has_vres:
        vfirst_ref = refs[pos]
        pos += 1
    (mu_ref, w0_ref, w2_ref, a0_ref, a2_ref, g2_ref) = refs[pos:pos + 6]
    pos += 6
    if has_vres:
        v0_ref, v2_ref = refs[pos:pos + 2]
        pos += 2
    (kk_ref, ka_ref, rk_ref, seg_ref, tri_ref) = refs[pos:pos + 5]
    pos += 5
    (rt_ref, kt_ref, at_ref, bt_ref, vb_ref, g_ref, bonus_ref, wl_ref) = refs[pos:pos + 8]
    pos += 8
    if not has_vres:
        vf_ref = refs[pos]
        pos += 1
    carry_rkv, carry_p = refs[pos:pos + 2]
    rw = r_ref.shape[-1]
    n_low = p_ref.shape[-1] // 2

    @pl.when(pl.program_id(1) == 0)
    def _():
        carry_rkv[...] = jnp.zeros_like(carry_rkv)
        carry_p[...] = jnp.zeros_like(carry_p)

    def lerp_prev(ref, idx):
        cur = ref[0]
        prev = _shift_rows(cur, carry_rkv[0:1, idx * rw:(idx + 1) * rw])
        carry_rkv[0:1, idx * rw:(idx + 1) * rw] = cur[tm - 1:tm, :]
        return cur + (prev - cur) * mu_ref[idx:idx + 1, :]

    r = lerp_prev(r_ref, 0)
    k = lerp_prev(k_ref, 1)
    v = lerp_prev(v_ref, 2)

    p = p_ref[0]
    p_b = p[:, n_low:]
    lowr = p[:, :n_low] + _shift_rows(p_b, carry_p[0:1, :])
    carry_p[0:1, :] = p_b[tm - 1:tm, :]
    o_w, o_a, o_g, o_v = low

    def low_slice(o):
        return lowr[:, o[0]:o[0] + o[1]]

    zw = w0_ref[...] + _dot(jnp.tanh(low_slice(o_w)).astype(BF16), w2_ref[...])
    logw = -jnp.exp(jnp.float32(-0.5)) * jax.nn.sigmoid(zw)
    asig = jax.nn.sigmoid(a0_ref[...] + _dot(low_slice(o_a).astype(BF16), a2_ref[...]))
    g_ref[0] = _dot(jax.nn.sigmoid(low_slice(o_g)).astype(BF16), g2_ref[...])
    if has_vres:
        vgate = jax.nn.sigmoid(v0_ref[...] + _dot(low_slice(o_v).astype(BF16), v2_ref[...]))
        v = v + (vfirst_ref[0] - v) * vgate
    else:
        vf_ref[0] = v

    seg = seg_ref[...]
    kk = k * kk_ref[...]
    kk = kk / jnp.maximum(jnp.sqrt(_head_sums(kk * kk, seg)), 1e-12)
    k = k * (1.0 + (asig - 1.0) * ka_ref[...])
    bonus_ref[0] = _head_sums(r * k * rk_ref[...], seg) * v

    cum = jnp.dot(tri_ref[...], logw, preferred_element_type=F32, precision=lax.Precision.HIGHEST)
    e_pos = jnp.exp(cum)
    e_neg = jnp.exp(-cum)
    rt_ref[0] = (r * e_pos).astype(BF16)
    kt_ref[0] = (k * e_neg).astype(BF16)
    bt_ref[0] = (kk * asig * e_neg).astype(BF16)
    at_ref[0] = (-kk * jnp.exp(cum - logw)).astype(BF16)
    vb_ref[0] = v.astype(BF16)
    ends = [cum[c * CHUNK + CHUNK - 1:c * CHUNK + CHUNK, :] for c in range(tm // CHUNK)]
    if tm // CHUNK < SUBLANE:
        ends.append(jnp.zeros((SUBLANE - tm // CHUNK, rw), F32))
    wl_ref[0, 0] = jnp.exp(jnp.concatenate(ends, axis=0))


def rwkv_prep(h3, rw, low, low_blk, lp, vfirst, tm):
    bsz, s, cols = h3.shape
    n_low2 = cols // (low_blk + 1)
    assert n_low2 * (low_blk + 1) == cols and tm % CHUNK == 0 and tm // CHUNK <= SUBLANE
    has_vres = vfirst is not None
    blk = lambda j: pl.BlockSpec((1, tm, rw), lambda b, i, j=j: (b, i, j))
    full = lambda a: pl.BlockSpec(a.shape, lambda b, i: (0,) * a.ndim)
    row = lambda a: a.reshape(1, -1)
    args = [h3, h3, h3, h3]
    in_specs = [blk(0), blk(1), blk(2),
                pl.BlockSpec((1, tm, n_low2), lambda b, i: (b, i, low_blk))]
    if has_vres:
        args.append(vfirst)
        in_specs.append(blk(0))
    seg = (lax.broadcasted_iota(jnp.int32, (GROUP_LANES, GROUP_LANES), 0) // HEAD ==
           lax.broadcasted_iota(jnp.int32, (GROUP_LANES, GROUP_LANES), 1) // HEAD).astype(BF16)
    ti = lax.broadcasted_iota(jnp.int32, (tm, tm), 0)
    tj = lax.broadcasted_iota(jnp.int32, (tm, tm), 1)
    tri = ((ti >= tj) & (ti // CHUNK == tj // CHUNK)).astype(F32)
    small = [lp['mu_rkv'].reshape(3, rw), row(lp['w0']), lp['w2p'], row(lp['a0']), lp['a2p'],
             lp['g2p']]
    if has_vres:
        small += [row(lp['v0']), lp['v2p']]
    small += [row(lp['k_k']), row(lp['k_a']), row(lp['r_k']), seg, tri]
    args += small
    in_specs += [full(a) for a in small]
    tok = lambda dt: jax.ShapeDtypeStruct((bsz, s, rw), dt)
    out_shape = [tok(BF16)] * 5 + [tok(F32), tok(F32),
                                   jax.ShapeDtypeStruct((bsz, s // tm, SUBLANE, rw), F32)]
    out_specs = [blk(0)] * 7 + [pl.BlockSpec((1, 1, SUBLANE, rw), lambda b, i: (b, i, 0, 0))]
    if not has_vres:
        out_shape.append(tok(F32))
        out_specs.append(blk(0))
    return pl.pallas_call(
        functools.partial(_prep_kernel, has_vres=has_vres, tm=tm, low=low),
        grid=(bsz, s // tm),
        in_specs=in_specs,
        out_specs=out_specs,
        out_shape=out_shape,
        scratch_shapes=[pltpu.VMEM((SUBLANE, 3 * rw), F32), pltpu.VMEM((SUBLANE, n_low2 // 2), F32)],
        compiler_params=_params("arbitrary", "arbitrary"),
        name="rwkv_prep",
    )(*args)


def _expand_heads(x, head_mask):
    return jnp.where(head_mask, jnp.concatenate([x] * HEADS_PER_GROUP, axis=0),
                     jnp.zeros((), x.dtype))


def _scan_kernel(rt_ref, kt_ref, at_ref, bt_ref, v_ref, g_ref, bonus_ref, wl_ref,
                 lng_ref, lnb_ref, o_ref, state_ref, *, n_chunks, n_groups):
    rows = HEADS_PER_GROUP * CHUNK
    t_i = lax.broadcasted_iota(jnp.int32, (CHUNK, rows), 0)
    s_i = lax.broadcasted_iota(jnp.int32, (CHUNK, rows), 1) % CHUNK
    strict = t_i > s_i
    incl = t_i >= s_i
    eye = (t_i == s_i).astype(F32)
    ri = lax.broadcasted_iota(jnp.int32, (rows, GROUP_LANES), 0)
    ci = lax.broadcasted_iota(jnp.int32, (rows, GROUP_LANES), 1)
    head_mask = ri // CHUNK == ci // HEAD
    gi = lax.broadcasted_iota(jnp.int32, (GROUP_LANES, GROUP_LANES), 0)
    gj = lax.broadcasted_iota(jnp.int32, (GROUP_LANES, GROUP_LANES), 1)
    same_head = gi // HEAD == gj // HEAD
    avg = jnp.where(same_head, 1.0 / HEAD, 0.0).astype(BF16)
    nt = (((1,), (1,)), ((), ()))
    tn = (((0,), (0,)), ((), ()))
    bd = lambda x: _expand_heads(x.astype(BF16), head_mask)

    @pl.when(pl.program_id(2) == 0)
    def _():
        state_ref[...] = jnp.zeros_like(state_ref)

    chains = [(c, g) for c in range(n_chunks) for g in range(n_groups)]

    def window(ref, c, g):
        return ref[0, c * CHUNK:(c + 1) * CHUNK, g * GROUP_LANES:(g + 1) * GROUP_LANES]

    ops, a_ab, a_ak, m_rbk, t_inv, pw = {}, {}, {}, {}, {}, {}
    for ch in chains:
        a_t, r_t, b_t, k_t, v = (window(ref, *ch) for ref in (at_ref, rt_ref, bt_ref, kt_ref, v_ref))
        ops[ch] = (a_t, r_t, jnp.concatenate([b_t, k_t], axis=0), v, bd(v))
        p = lax.dot_general(jnp.concatenate([a_t, r_t], axis=0),
                            jnp.concatenate([bd(b_t), bd(k_t)], axis=0), nt,
                            preferred_element_type=F32)
        a_ab[ch] = jnp.where(strict, p[:CHUNK, :rows], 0.0)
        a_ak[ch] = jnp.where(strict, p[:CHUNK, rows:], 0.0).astype(BF16)
        m_rbk[ch] = jnp.where(jnp.concatenate([incl, incl], axis=1), p[CHUNK:], 0.0).astype(BF16)
    for ch in chains:
        t_inv[ch] = eye + a_ab[ch]
        pw[ch] = _dot(a_ab[ch].astype(BF16), bd(a_ab[ch]))
    step = 2
    while 2 * step < CHUNK:
        for ch in chains:
            both = _dot(jnp.concatenate([t_inv[ch], pw[ch]], axis=0).astype(BF16), bd(pw[ch]))
            t_inv[ch] = t_inv[ch] + both[:CHUNK]
            pw[ch] = both[CHUNK:]
        step *= 2
    taw = {}
    for ch in chains:
        t_fin = (t_inv[ch] + _dot(t_inv[ch].astype(BF16), bd(pw[ch]))).astype(BF16)
        a_t, _, _, _, v_bd = ops[ch]
        av = _dot(a_ak[ch], v_bd)
        taw[ch] = _dot(t_fin, jnp.concatenate([bd(a_t), bd(av)], axis=1))

    for c in range(n_chunks):
        ys = {}
        for g in range(n_groups):
            _, r_t, bk, v, v_bd = ops[c, g]
            lanes = slice(g * GROUP_LANES, (g + 1) * GROUP_LANES)
            state = state_ref[g]
            ars = lax.dot_general(
                jnp.concatenate([taw[c, g][:, :GROUP_LANES].astype(BF16), r_t], axis=0),
                state.astype(BF16), nt, preferred_element_type=F32)
            u = (ars[:CHUNK] + taw[c, g][:, GROUP_LANES:]).astype(BF16)
            upd = lax.dot_general(jnp.concatenate([u, v], axis=0), bk, tn,
                                  preferred_element_type=F32)
            state_ref[g] = (state + jnp.where(same_head, upd, 0.0)) * wl_ref[0, 0, c:c + 1, lanes]
            ys[g] = ars[CHUNK:] + _dot(m_rbk[c, g], jnp.concatenate([bd(u), v_bd], axis=0))
        for g in range(n_groups):
            lanes = slice(g * GROUP_LANES, (g + 1) * GROUP_LANES)
            y = ys[g]
            dlt = y - _dot(y.astype(BF16), avg)
            var = _dot((dlt * dlt).astype(BF16), avg)
            yn = dlt * lax.rsqrt(var + GN_EPS) * lng_ref[:, lanes] + lnb_ref[:, lanes]
            o_ref[0, c * CHUNK:(c + 1) * CHUNK, lanes] = (
                (yn + window(bonus_ref, c, g)) * window(g_ref, c, g)).astype(o_ref.dtype)


def wkv_scan(rt, kt, at, bt, vb, g, bonus, wl, lnx_g, lnx_b, tm):
    bsz, s, rw = rt.shape
    n_groups = SCAN_GROUPS if rw % (SCAN_GROUPS * GROUP_LANES) == 0 else 1
    width = n_groups * GROUP_LANES
    blk = pl.BlockSpec((1, tm, width), lambda b, hg, i: (b, i, hg))
    vec = pl.BlockSpec((1, width), lambda b, hg, i: (0, hg))
    return pl.pallas_call(
        functools.partial(_scan_kernel, n_chunks=tm // CHUNK, n_groups=n_groups),
        grid=(bsz, rw // width, s // tm),
        in_specs=[blk] * 7 + [pl.BlockSpec((1, 1, SUBLANE, width),
                                           lambda b, hg, i: (b, i, 0, hg)), vec, vec],
        out_specs=blk,
        out_shape=jax.ShapeDtypeStruct((bsz, s, rw), BF16),
        scratch_shapes=[pltpu.VMEM((n_groups, GROUP_LANES, GROUP_LANES), F32)],
        compiler_params=_params("arbitrary", "arbitrary", "arbitrary"),
        name="wkv_scan",
    )(rt, kt, at, bt, vb, g, bonus, wl, lnx_g.reshape(1, rw), lnx_b.reshape(1, rw))


def _conv_kernel(val_ref, gate_ref, w_ref, b_ref, lg_ref, lb_ref, o_ref, ext_ref, *, tm, taps, halo):
    @pl.when(pl.program_id(1) == 0)
    def _():
        ext_ref[0:halo, :] = jnp.zeros((halo, ext_ref.shape[1]), F32)

    u = val_ref[0] * jax.nn.sigmoid(gate_ref[0])
    ext_ref[halo:halo + tm, :] = u
    acc = jnp.zeros_like(u) + b_ref[...]
    for j in range(taps):
        off = halo - (taps - 1) + j
        acc = acc + w_ref[j:j + 1, :] * ext_ref[off:off + tm, :]
    ext_ref[0:halo, :] = u[tm - halo:tm, :]
    mu = jnp.mean(acc, axis=-1, keepdims=True)
    d = acc - mu
    var = jnp.mean(d * d, axis=-1, keepdims=True)
    z = d * lax.rsqrt(var + LN_EPS) * lg_ref[...] + lb_ref[...]
    o_ref[0] = (z * jax.nn.sigmoid(z)).astype(o_ref.dtype)


def conv_group(h3, rw, cw, conv_w, conv_b, ln_g, ln_b):
    bsz, s, _ = h3.shape
    taps = conv_w.shape[0]
    halo = _round_up(taps - 1, SUBLANE)
    tm = _pick(s, 128)
    assert (3 * rw) % cw == 0 and tm >= halo
    c0 = (3 * rw) // cw
    row = pl.BlockSpec((1, cw), lambda b, i: (0, 0))
    return pl.pallas_call(
        functools.partial(_conv_kernel, tm=tm, taps=taps, halo=halo),
        grid=(bsz, s // tm),
        in_specs=[pl.BlockSpec((1, tm, cw), lambda b, i: (b, i, c0)),
                  pl.BlockSpec((1, tm, cw), lambda b, i: (b, i, c0 + 1)),
                  pl.BlockSpec((taps, cw), lambda b, i: (0, 0)), row, row, row],
        out_specs=pl.BlockSpec((1, tm, cw), lambda b, i: (b, i, 0)),
        out_shape=jax.ShapeDtypeStruct((bsz, s, cw), BF16),
        scratch_shapes=[pltpu.VMEM((halo + tm, cw), F32)],
        compiler_params=_params("arbitrary", "arbitrary"),
        name="conv_group",
    )(h3, h3, conv_w, conv_b.reshape(1, cw), ln_g.reshape(1, cw), ln_b.reshape(1, cw))


def _pad_cols(w, n):
    return jnp.pad(w, ((0, 0), (0, n - w.shape[1])))


def _pad_rows(w, n):
    return jnp.pad(w, ((0, n - w.shape[0]), (0, 0)))


def _mixer_weights(lp, has_vres):
    names = [('w', 'mu_w', 'w1', 'w2'), ('a', 'mu_a', 'a1', 'a2'), ('g', 'mu_g', 'g1', 'g2')]
    if has_vres:
        names.append(('v', 'mu_v', 'v1', 'v2'))
    cur, prev, low, off = [], [], [], 0
    out = dict(lp)
    for tag, mu, w1, w2 in names:
        rank = lp[w1].shape[1]
        rpad = _round_up(rank, LANE)
        cur.append(_pad_cols((1.0 - lp[mu])[:, None] * lp[w1], rpad))
        prev.append(_pad_cols(lp[mu][:, None] * lp[w1], rpad))
        out[w2 + 'p'] = _pad_rows(lp[w2], rpad).astype(BF16)
        low.append((off, rpad))
        off += rpad
    if not has_vres:
        low.append((0, 0))
    in_cols = lp['w_in'].shape[1]
    low_start = _round_up(in_cols, 2 * off)
    out['w_big'] = jnp.concatenate([_pad_cols(lp['w_in'], low_start)] + cur + prev,
                                   axis=1).astype(BF16)
    return out, tuple(low), low_start // (2 * off)


def _mixer(xm, x, lp, mods, vfirst, nxt, router):
    bsz, s, d = x.shape
    rw = lp['w0'].shape[0]
    cw = lp['conv_b'].shape[0]
    has_vres = vfirst is not None
    lw, low, low_blk = _mixer_weights(lp, has_vres)
    n_cols = lw['w_big'].shape[1]
    h = matmul([xm.reshape(bsz * s, d)], [lw['w_big']], F32, tn_pref=n_cols // (low_blk + 1),
               name="in_projection")
    h3 = h.reshape(bsz, s, n_cols)
    tm = _pick(s, 256)
    outs = rwkv_prep(h3, rw, low, low_blk, lw, vfirst, tm)
    rt, kt, at, bt, vb, g, bonus, wl = outs[:8]
    v_out = vfirst if has_vres else outs[8]
    y_rwkv = wkv_scan(rt, kt, at, bt, vb, g, bonus, wl, lp['lnx_g'], lp['lnx_b'], tm)
    u = conv_group(h3, rw, cw, lp['conv_w'], lp['conv_b'], lp['conv_ln_g'], lp['conv_ln_b'])
    w_out = lp['w_out'].astype(BF16)
    o = matmul([y_rwkv.reshape(bsz * s, rw), u.reshape(bsz * s, cw)], [w_out[:rw], w_out[rw:]],
               F32, name="out_projection")
    res = sublayer_epilogue(o.reshape(bsz, s, d), x, lp['mix_post_g'], mods['gt_m'], nxt, router)
    return res, v_out


def _split_layer(p, prefix, has_vres):
    keys = ['ada_w', 'ada_b', 'mix_pre_g', 'mix_post_g', 'w_in', 'mu_rkv', 'mu_w', 'mu_a', 'mu_g',
            'w0', 'w1', 'w2', 'a0', 'a1', 'a2', 'g1', 'g2', 'k_k', 'k_a', 'r_k', 'lnx_g', 'lnx_b',
            'conv_w', 'conv_b', 'conv_ln_g', 'conv_ln_b', 'w_out', 'ffn_pre_g', 'ffn_post_g']
    if has_vres:
        keys += ['mu_v', 'v0', 'v1', 'v2']
    return {k: p[prefix + k] for k in keys}


def _forward(p):
    x = p['x']
    c = p['c']
    bsz, s, d = x.shape
    layers = [_split_layer(p, 'l0_', False), _split_layer(p, 'l1_', True)]
    mods = []
    for lp in layers:
        mod = ada_modulation(c, lp['ada_w'], lp['ada_b'])
        mods.append(dict(zip(['sh_m', 'sc_m', 'gt_m', 'sh_f', 'sc_f', 'gt_f'],
                             jnp.split(mod, 6, axis=-1))))

    lp, md = layers[0], mods[0]
    xm = norm_modulate(x, lp['mix_pre_g'], md['sh_m'], md['sc_m'])
    (x, xf), vfirst = _mixer(xm, x, lp, md, None, (lp['ffn_pre_g'], md['sh_f'], md['sc_f']), None)
    y = dense_swiglu(xf.reshape(bsz * s, d), p['l0_ffn_w_gate'].astype(BF16),
                     p['l0_ffn_w_up'].astype(BF16), p['l0_ffn_w_down'].astype(BF16))
    nlp, nmd = layers[1], mods[1]
    x, xm = sublayer_epilogue(y.reshape(bsz, s, d), x, lp['ffn_post_g'], md['gt_f'],
                              (nlp['mix_pre_g'], nmd['sh_m'], nmd['sc_m']))

    lp, md = layers[1], mods[1]
    (x, xf, route), _ = _mixer(xm, x, lp, md, vfirst, (lp['ffn_pre_g'], md['sh_f'], md['sc_f']),
                               (p['l1_router_w'], p['l1_router_b']))
    return moe_sublayer(xf, route, x, lp['ffn_post_g'], md['gt_f'],
                        p['l1_moe_w_gate'].astype(BF16), p['l1_moe_w_up'].astype(BF16),
                        p['l1_moe_w_down'].astype(BF16))


_ARG_NAMES = (
    'x', 'c',
    'l0_ada_w', 'l0_ada_b', 'l0_mix_pre_g', 'l0_mix_post_g', 'l0_w_in', 'l0_mu_rkv', 'l0_mu_w',
    'l0_mu_a', 'l0_mu_g', 'l0_w0', 'l0_w1', 'l0_w2', 'l0_a0', 'l0_a1', 'l0_a2', 'l0_g1', 'l0_g2',
    'l0_k_k', 'l0_k_a', 'l0_r_k', 'l0_lnx_g', 'l0_lnx_b', 'l0_conv_w', 'l0_conv_b', 'l0_conv_ln_g',
    'l0_conv_ln_b', 'l0_w_out',
    'l0_ffn_pre_g', 'l0_ffn_post_g', 'l0_ffn_w_gate', 'l0_ffn_w_up', 'l0_ffn_w_down',
    'l1_ada_w', 'l1_ada_b', 'l1_mix_pre_g', 'l1_mix_post_g', 'l1_w_in', 'l1_mu_rkv', 'l1_mu_w',
    'l1_mu_a', 'l1_mu_g', 'l1_w0', 'l1_w1', 'l1_w2', 'l1_a0', 'l1_a1', 'l1_a2', 'l1_g1', 'l1_g2',
    'l1_k_k', 'l1_k_a', 'l1_r_k', 'l1_lnx_g', 'l1_lnx_b', 'l1_conv_w', 'l1_conv_b', 'l1_conv_ln_g',
    'l1_conv_ln_b', 'l1_w_out', 'l1_mu_v', 'l1_v0', 'l1_v1', 'l1_v2',
    'l1_ffn_pre_g', 'l1_ffn_post_g', 'l1_router_w', 'l1_router_b', 'l1_moe_w_gate',
    'l1_moe_w_up', 'l1_moe_w_down')


def kernel(x, c, l0_ada_w, l0_ada_b, l0_mix_pre_g, l0_mix_post_g, l0_w_in, l0_mu_rkv, l0_mu_w, l0_mu_a, l0_mu_g, l0_w0, l0_w1, l0_w2, l0_a0, l0_a1, l0_a2, l0_g1, l0_g2, l0_k_k, l0_k_a, l0_r_k, l0_lnx_g, l0_lnx_b, l0_conv_w, l0_conv_b, l0_conv_ln_g, l0_conv_ln_b, l0_w_out, l0_ffn_pre_g, l0_ffn_post_g, l0_ffn_w_gate, l0_ffn_w_up, l0_ffn_w_down, l1_ada_w, l1_ada_b, l1_mix_pre_g, l1_mix_post_g, l1_w_in, l1_mu_rkv, l1_mu_w, l1_mu_a, l1_mu_g, l1_w0, l1_w1, l1_w2, l1_a0, l1_a1, l1_a2, l1_g1, l1_g2, l1_k_k, l1_k_a, l1_r_k, l1_lnx_g, l1_lnx_b, l1_conv_w, l1_conv_b, l1_conv_ln_g, l1_conv_ln_b, l1_w_out, l1_mu_v, l1_v0, l1_v1, l1_v2, l1_ffn_pre_g, l1_ffn_post_g, l1_router_w, l1_router_b, l1_moe_w_gate, l1_moe_w_up, l1_moe_w_down):
    args = (x, c, l0_ada_w, l0_ada_b, l0_mix_pre_g, l0_mix_post_g, l0_w_in, l0_mu_rkv, l0_mu_w, l0_mu_a, l0_mu_g, l0_w0, l0_w1, l0_w2, l0_a0, l0_a1, l0_a2, l0_g1, l0_g2, l0_k_k, l0_k_a, l0_r_k, l0_lnx_g, l0_lnx_b, l0_conv_w, l0_conv_b, l0_conv_ln_g, l0_conv_ln_b, l0_w_out, l0_ffn_pre_g, l0_ffn_post_g, l0_ffn_w_gate, l0_ffn_w_up, l0_ffn_w_down, l1_ada_w, l1_ada_b, l1_mix_pre_g, l1_mix_post_g, l1_w_in, l1_mu_rkv, l1_mu_w, l1_mu_a, l1_mu_g, l1_w0, l1_w1, l1_w2, l1_a0, l1_a1, l1_a2, l1_g1, l1_g2, l1_k_k, l1_k_a, l1_r_k, l1_lnx_g, l1_lnx_b, l1_conv_w, l1_conv_b, l1_conv_ln_g, l1_conv_ln_b, l1_w_out, l1_mu_v, l1_v0, l1_v1, l1_v2, l1_ffn_pre_g, l1_ffn_post_g, l1_router_w, l1_router_b, l1_moe_w_gate, l1_moe_w_up, l1_moe_w_down)
    return _forward(dict(zip(_ARG_NAMES, args)))
```

```python
import functools

import jax
import jax.numpy as jnp
from jax import lax
from jax.experimental import pallas as pl
from jax.experimental.pallas import tpu as pltpu

F32 = jnp.float32
BF16 = jnp.bfloat16

HEAD = 64
HEADS_PER_GROUP = 4
GROUP_LANES = HEAD * HEADS_PER_GROUP
CHUNK = 64
SCAN_GROUPS = 4
LANE = 128
SUBLANE = 8
TOP_K = 2
RMS_EPS = 1e-6
GN_EPS = 64e-5
LN_EPS = 1e-5
VMEM_LIMIT_BYTES = 56 * 1024 * 1024


def _params(*sem):
    return pltpu.CompilerParams(dimension_semantics=sem, vmem_limit_bytes=VMEM_LIMIT_BYTES)


def _round_up(n, m):
    return (n + m - 1) // m * m


def _pick(n, pref):
    if n <= pref:
        return n
    t = pref
    while t >= LANE:
        if n % t == 0:
            return t
        t -= LANE
    return n


def _dot(a, b):
    return jnp.dot(a, b, preferred_element_type=F32)


def _rms(x, g):
    return x * lax.rsqrt(jnp.mean(x * x, axis=-1, keepdims=True) + RMS_EPS) * g


def _ada_kernel(c_ref, w_ref, b_ref, o_ref):
    c = c_ref[...]
    s = c * jax.nn.sigmoid(c)
    o_ref[...] = jnp.dot(s, w_ref[...], preferred_element_type=F32,
                         precision=lax.Precision.HIGHEST) + b_ref[...]


def ada_modulation(c, ada_w, ada_b):
    bsz, d = c.shape
    n = ada_w.shape[1]
    rows = _round_up(bsz, SUBLANE)
    cp = jnp.zeros((rows, d), F32).at[:bsz].set(c)
    tn = _pick(n, 1024)
    out = pl.pallas_call(
        _ada_kernel,
        grid=(n // tn,),
        in_specs=[pl.BlockSpec((rows, d), lambda j: (0, 0)),
                  pl.BlockSpec((d, tn), lambda j: (0, j)),
                  pl.BlockSpec((1, tn), lambda j: (0, j))],
        out_specs=pl.BlockSpec((rows, tn), lambda j: (0, j)),
        out_shape=jax.ShapeDtypeStruct((rows, n), F32),
        compiler_params=_params("arbitrary"),
        name="ada_matvec",
    )(cp, ada_w, ada_b.reshape(1, n))
    return out[:bsz]


def _norm_mod_kernel(x_ref, g_ref, sh_ref, sc_ref, o_ref):
    xn = _rms(x_ref[0], g_ref[...])
    o_ref[0] = (xn * (1.0 + sc_ref[0]) + sh_ref[0]).astype(o_ref.dtype)


def norm_modulate(x, g, shift, scale):
    bsz, s, d = x.shape
    tm = _pick(s, 512)
    vec = pl.BlockSpec((1, 1, d), lambda b, i: (b, 0, 0))
    return pl.pallas_call(
        _norm_mod_kernel,
        grid=(bsz, s // tm),
        in_specs=[pl.BlockSpec((1, tm, d), lambda b, i: (b, i, 0)),
                  pl.BlockSpec((1, d), lambda b, i: (0, 0)), vec, vec],
        out_specs=pl.BlockSpec((1, tm, d), lambda b, i: (b, i, 0)),
        out_shape=jax.ShapeDtypeStruct((bsz, s, d), BF16),
        compiler_params=_params("arbitrary", "arbitrary"),
        name="norm_modulate",
    )(x, g.reshape(1, d), shift.reshape(bsz, 1, d), scale.reshape(bsz, 1, d))


def _mm_kernel(*refs, n_seg):
    o_ref = refs[2 * n_seg]
    acc = _dot(refs[0][...], refs[n_seg][...])
    for s in range(1, n_seg):
        acc = acc + _dot(refs[s][...], refs[n_seg + s][...])
    o_ref[...] = acc.astype(o_ref.dtype)


def matmul(a_list, w_list, out_dtype, tm_pref=1024, tn_pref=1024, name="matmul"):
    m = a_list[0].shape[0]
    n = w_list[0].shape[1]
    tm = _pick(m, tm_pref)
    tn = _pick(n, tn_pref)
    n_seg = len(a_list)
    in_specs = [pl.BlockSpec((tm, a.shape[1]), lambda i, j: (i, 0)) for a in a_list]
    in_specs += [pl.BlockSpec((w.shape[0], tn), lambda i, j: (0, j)) for w in w_list]
    return pl.pallas_call(
        functools.partial(_mm_kernel, n_seg=n_seg),
        grid=(m // tm, n // tn),
        in_specs=in_specs,
        out_specs=pl.BlockSpec((tm, tn), lambda i, j: (i, j)),
        out_shape=jax.ShapeDtypeStruct((m, n), out_dtype),
        compiler_params=_params("arbitrary", "arbitrary"),
        name=name,
    )(*a_list, *w_list)


ROUTE_E1, ROUTE_E2, ROUTE_W1, ROUTE_W2 = 0, 1, 2, 3


def _top2_route(logits, n_experts):
    lane = lax.broadcasted_iota(jnp.int32, logits.shape, 1)
    neg = jnp.float32(-jnp.inf)
    lg = jnp.where(lane < n_experts, logits, neg)
    v1 = jnp.max(lg, axis=-1, keepdims=True)
    i1 = jnp.min(jnp.where(lg == v1, lane, LANE), axis=-1, keepdims=True)
    lg2 = jnp.where(lane == i1, neg, lg)
    v2 = jnp.max(lg2, axis=-1, keepdims=True)
    i2 = jnp.min(jnp.where(lg2 == v2, lane, LANE), axis=-1, keepdims=True)
    e2 = jnp.exp(v2 - v1)
    w1 = 1.0 / (1.0 + e2)
    w2 = e2 / (1.0 + e2)
    rec = jnp.where(lane == ROUTE_E1, i1.astype(F32), 0.0)
    rec = jnp.where(lane == ROUTE_E2, i2.astype(F32), rec)
    rec = jnp.where(lane == ROUTE_W1, w1, rec)
    return jnp.where(lane == ROUTE_W2, w2, rec)


def _epilogue_kernel(*refs, has_next, n_experts):
    y_ref, x_ref, post_g_ref, gate_ref = refs[:4]
    pos = 4
    if has_next:
        pre_g_ref, sh_ref, sc_ref = refs[pos:pos + 3]
        pos += 3
    if n_experts:
        rw_ref, rb_ref = refs[pos:pos + 2]
        pos += 2
    outs = refs[pos:]
    x1 = x_ref[0] + gate_ref[0] * _rms(y_ref[0].astype(F32), post_g_ref[...])
    outs[0][0] = x1
    if has_next:
        xf = _rms(x1, pre_g_ref[...]) * (1.0 + sc_ref[0]) + sh_ref[0]
        outs[1][0] = xf.astype(outs[1].dtype)
        if n_experts:
            logits = jnp.dot(xf, rw_ref[...], preferred_element_type=F32,
                             precision=lax.Precision.HIGHEST) + rb_ref[...]
            outs[2][0] = _top2_route(logits, n_experts)


def sublayer_epilogue(y, x, post_g, gate, nxt=None, router=None):
    bsz, s, d = x.shape
    tm = _pick(s, 256)
    blk = pl.BlockSpec((1, tm, d), lambda b, i: (b, i, 0))
    row = pl.BlockSpec((1, d), lambda b, i: (0, 0))
    vec = pl.BlockSpec((1, 1, d), lambda b, i: (b, 0, 0))
    args = [y, x, post_g.reshape(1, d), gate.reshape(bsz, 1, d)]
    in_specs = [blk, blk, row, vec]
    out_shape = [jax.ShapeDtypeStruct((bsz, s, d), F32)]
    out_specs = [blk]
    n_experts = 0
    if nxt is not None:
        pre_g, shift, scale = nxt
        args += [pre_g.reshape(1, d), shift.reshape(bsz, 1, d), scale.reshape(bsz, 1, d)]
        in_specs += [row, vec, vec]
        out_shape.append(jax.ShapeDtypeStruct((bsz, s, d), BF16 if router is None else F32))
        out_specs.append(blk)
        if router is not None:
            router_w, router_b = router
            n_experts = router_w.shape[1]
            rw = jnp.zeros((d, LANE), F32).at[:, :n_experts].set(router_w)
            rb = jnp.zeros((1, LANE), F32).at[0, :n_experts].set(router_b)
            args += [rw, rb]
            in_specs += [pl.BlockSpec((d, LANE), lambda b, i: (0, 0)),
                         pl.BlockSpec((1, LANE), lambda b, i: (0, 0))]
            out_shape.append(jax.ShapeDtypeStruct((bsz, s, LANE), F32))
            out_specs.append(pl.BlockSpec((1, tm, LANE), lambda b, i: (b, i, 0)))
    return pl.pallas_call(
        functools.partial(_epilogue_kernel, has_next=nxt is not None, n_experts=n_experts),
        grid=(bsz, s // tm),
        in_specs=in_specs,
        out_specs=out_specs,
        out_shape=out_shape,
        compiler_params=_params("arbitrary", "arbitrary"),
        name="sublayer_epilogue",
    )(*args)


def _ffn_kernel(x_ref, wg_ref, wu_ref, wd_ref, o_ref, acc_ref):
    j = pl.program_id(1)

    @pl.when(j == 0)
    def _():
        acc_ref[...] = jnp.zeros_like(acc_ref)

    x = x_ref[...]
    hg = _dot(x, wg_ref[...])
    hu = _dot(x, wu_ref[...])
    h = (hg * jax.nn.sigmoid(hg) * hu).astype(BF16)
    acc_ref[...] += _dot(h, wd_ref[...])

    @pl.when(j == pl.num_programs(1) - 1)
    def _():
        o_ref[...] = acc_ref[...].astype(o_ref.dtype)


def dense_swiglu(x, w_gate, w_up, w_down):
    m, d = x.shape
    f = w_gate.shape[1]
    tm = _pick(m, 512)
    tf = _pick(f, 512)
    return pl.pallas_call(
        _ffn_kernel,
        grid=(m // tm, f // tf),
        in_specs=[pl.BlockSpec((tm, d), lambda i, j: (i, 0)),
                  pl.BlockSpec((d, tf), lambda i, j: (0, j)),
                  pl.BlockSpec((d, tf), lambda i, j: (0, j)),
                  pl.BlockSpec((tf, d), lambda i, j: (j, 0))],
        out_specs=pl.BlockSpec((tm, d), lambda i, j: (i, 0)),
        out_shape=jax.ShapeDtypeStruct((m, d), BF16),
        scratch_shapes=[pltpu.VMEM((tm, d), F32)],
        compiler_params=_params("arbitrary", "arbitrary"),
        name="dense_swiglu",
    )(x, w_gate, w_up, w_down)


MOE_ROW_TILE = 512
DMA_ROWS = 256


def _rank_kernel(route_ref, tri_ref, rank_ref, count_ref, carry_ref):
    @pl.when(pl.program_id(0) == 0)
    def _():
        carry_ref[...] = jnp.zeros_like(carry_ref)

    route = route_ref[...]
    lane = lax.broadcasted_iota(jnp.int32, route.shape, 1)
    lane_f = lane.astype(F32)
    oh1 = lane_f == route[:, ROUTE_E1:ROUTE_E1 + 1]
    oh2 = lane_f == route[:, ROUTE_E2:ROUTE_E2 + 1]
    cnt = jnp.where(oh1 | oh2, 1.0, 0.0)
    before = _dot(tri_ref[...], cnt.astype(BF16)) + carry_ref[0:1, :]
    r1 = jnp.sum(jnp.where(oh1, before, 0.0), axis=-1, keepdims=True)
    r2 = jnp.sum(jnp.where(oh2, before, 0.0), axis=-1, keepdims=True)
    rank_ref[...] = jnp.where(lane == ROUTE_E1, r1, jnp.where(lane == ROUTE_E2, r2, 0.0))
    total = carry_ref[0:1, :] + jnp.sum(cnt, axis=0, keepdims=True)
    carry_ref[0:1, :] = total
    count_ref[...] = jnp.broadcast_to(total, count_ref.shape)


def expert_ranks(route):
    t = route.shape[0]
    tm = _pick(t, 256)
    ti = lax.broadcasted_iota(jnp.int32, (tm, tm), 0)
    tj = lax.broadcasted_iota(jnp.int32, (tm, tm), 1)
    tri = (ti > tj).astype(BF16)
    return pl.pallas_call(
        _rank_kernel,
        grid=(t // tm,),
        in_specs=[pl.BlockSpec((tm, LANE), lambda i: (i, 0)),
                  pl.BlockSpec((tm, tm), lambda i: (0, 0))],
        out_specs=[pl.BlockSpec((tm, LANE), lambda i: (i, 0)),
                   pl.BlockSpec((SUBLANE, LANE), lambda i: (0, 0))],
        out_shape=[jax.ShapeDtypeStruct((t, LANE), F32),
                   jax.ShapeDtypeStruct((SUBLANE, LANE), F32)],
        scratch_shapes=[pltpu.VMEM((SUBLANE, LANE), F32)],
        compiler_params=_params("arbitrary"),
        name="expert_ranks",
    )(route, tri)


def _row_copies(n_rows, make_copy):
    def start(r, carry):
        for cp in make_copy(r):
            cp.start()
        return carry

    def wait(r, carry):
        for cp in make_copy(r):
            cp.wait()
        return carry

    lax.fori_loop(0, n_rows, start, 0)
    lax.fori_loop(0, n_rows, wait, 0)


def _dispatch_kernel(pos1_ref, pos2_ref, x_ref, init_ref, o_ref, sem):
    del init_ref

    def make_copy(r):
        src = x_ref.at[pl.ds(r, 1)]
        return [pltpu.make_async_copy(src, o_ref.at[pl.ds(pos_ref[0, r], 1)], sem)
                for pos_ref in (pos1_ref, pos2_ref)]

    _row_copies(x_ref.shape[0], make_copy)


def moe_dispatch(x, pos1, pos2, n_rows):
    t, d = x.shape
    rows = pos1.shape[2]
    smem = pl.BlockSpec((None, 1, rows), lambda i: (i, 0, 0), memory_space=pltpu.SMEM)
    return pl.pallas_call(
        _dispatch_kernel,
        grid=(t // rows,),
        in_specs=[smem, smem, pl.BlockSpec((rows, d), lambda i: (i, 0)),
                  pl.BlockSpec(memory_space=pl.ANY)],
        out_specs=pl.BlockSpec(memory_space=pl.ANY),
        out_shape=jax.ShapeDtypeStruct((n_rows, d), F32),
        scratch_shapes=[pltpu.SemaphoreType.DMA(())],
        input_output_aliases={3: 0},
        compiler_params=_params("arbitrary"),
        name="moe_dispatch",
    )(pos1, pos2, x, jnp.zeros((n_rows, d), F32))


def _moe_kernel(tile_expert_ref, n_used_ref, x_ref, wg_ref, wu_ref, wd_ref, o_ref, xb_ref):
    del tile_expert_ref

    @pl.when(pl.program_id(1) == 0)
    def _():
        o_ref[...] = jnp.zeros_like(o_ref)
        xb_ref[...] = x_ref[...].astype(BF16)

    @pl.when(pl.program_id(0) < n_used_ref[0])
    def _():
        x = xb_ref[...]
        hg = _dot(x, wg_ref[...])
        hu = _dot(x, wu_ref[...])
        h = (hg * jax.nn.sigmoid(hg) * hu).astype(BF16)
        o_ref[...] += _dot(h, wd_ref[...])


def moe_grouped_swiglu(xs, tile_expert, n_used, w_gate, w_up, w_down):
    n_rows, d = xs.shape
    f = w_gate.shape[2]
    tf = _pick(f, 256)
    live = lambda j, k, te, nu: jnp.where(j < nu[0], k, 0)
    grid_spec = pltpu.PrefetchScalarGridSpec(
        num_scalar_prefetch=2,
        grid=(n_rows // MOE_ROW_TILE, f // tf),
        in_specs=[pl.BlockSpec((MOE_ROW_TILE, d), lambda j, k, te, nu: (j, 0)),
                  pl.BlockSpec((None, d, tf), lambda j, k, te, nu: (te[j], 0, live(j, k, te, nu))),
                  pl.BlockSpec((None, d, tf), lambda j, k, te, nu: (te[j], 0, live(j, k, te, nu))),
                  pl.BlockSpec((None, tf, d), lambda j, k, te, nu: (te[j], live(j, k, te, nu), 0))],
        out_specs=pl.BlockSpec((MOE_ROW_TILE, d), lambda j, k, te, nu: (j, 0)),
        scratch_shapes=[pltpu.VMEM((MOE_ROW_TILE, d), BF16)])
    return pl.pallas_call(
        _moe_kernel,
        grid_spec=grid_spec,
        out_shape=jax.ShapeDtypeStruct((n_rows, d), F32),
        compiler_params=_params("arbitrary", "arbitrary"),
        name="moe_grouped_swiglu",
    )(tile_expert, n_used, xs, w_gate, w_up, w_down)


def _combine_kernel(pos1_ref, pos2_ref, route_ref, x_ref, post_g_ref, gate_ref, ys_ref, o_ref,
                    buf1, buf2, sem):
    def make_copy(r):
        return [pltpu.make_async_copy(ys_ref.at[pl.ds(pos_ref[0, r], 1)], buf.at[pl.ds(r, 1)], sem)
                for pos_ref, buf in ((pos1_ref, buf1), (pos2_ref, buf2))]

    _row_copies(buf1.shape[0], make_copy)
    route = route_ref[0]
    y = (route[:, ROUTE_W1:ROUTE_W1 + 1] * buf1[...] + route[:, ROUTE_W2:ROUTE_W2 + 1] * buf2[...])
    o_ref[0] = x_ref[0] + gate_ref[0] * _rms(y, post_g_ref[...])


def moe_combine_epilogue(ys, pos1, pos2, route, x, post_g, gate):
    bsz, s, d = x.shape
    rows = pos1.shape[2]
    n_i = s // rows
    smem = pl.BlockSpec((None, 1, rows), lambda b, i: (b * n_i + i, 0, 0),
                        memory_space=pltpu.SMEM)
    blk = pl.BlockSpec((1, rows, d), lambda b, i: (b, i, 0))
    return pl.pallas_call(
        _combine_kernel,
        grid=(bsz, n_i),
        in_specs=[smem, smem, pl.BlockSpec((1, rows, LANE), lambda b, i: (b, i, 0)), blk,
                  pl.BlockSpec((1, d), lambda b, i: (0, 0)),
                  pl.BlockSpec((1, 1, d), lambda b, i: (b, 0, 0)),
                  pl.BlockSpec(memory_space=pl.ANY)],
        out_specs=blk,
        out_shape=jax.ShapeDtypeStruct((bsz, s, d), F32),
        scratch_shapes=[pltpu.VMEM((rows, d), F32), pltpu.VMEM((rows, d), F32),
                        pltpu.SemaphoreType.DMA(())],
        compiler_params=_params("arbitrary", "arbitrary"),
        name="moe_combine_epilogue",
    )(pos1, pos2, route, x, post_g.reshape(1, d), gate.reshape(bsz, 1, d), ys)


def moe_sublayer(xf, route, x, post_g, gate, w_gate, w_up, w_down):
    bsz, s, d = x.shape
    t = bsz * s
    n_e = w_gate.shape[0]
    rows = _pick(s, DMA_ROWS)
    assert (TOP_K * t) % MOE_ROW_TILE == 0
    n_tiles = TOP_K * t // MOE_ROW_TILE + n_e
    route2 = route.reshape(t, LANE)
    rank, count = expert_ranks(route2)
    counts = count[0, :n_e].astype(jnp.int32)
    tiles_e = (counts + MOE_ROW_TILE - 1) // MOE_ROW_TILE
    tile_end = jnp.cumsum(tiles_e)
    row_start = (tile_end - tiles_e) * MOE_ROW_TILE
    e1 = route2[:, ROUTE_E1].astype(jnp.int32)
    e2 = route2[:, ROUTE_E2].astype(jnp.int32)
    pos1 = (row_start[e1] + rank[:, ROUTE_E1].astype(jnp.int32)).reshape(t // rows, 1, rows)
    pos2 = (row_start[e2] + rank[:, ROUTE_E2].astype(jnp.int32)).reshape(t // rows, 1, rows)
    tile_expert = jnp.minimum(jnp.searchsorted(tile_end, jnp.arange(n_tiles), side='right'),
                              n_e - 1).astype(jnp.int32)
    n_used = tile_end[n_e - 1:].astype(jnp.int32)
    xs = moe_dispatch(xf.reshape(t, d), pos1, pos2, n_tiles * MOE_ROW_TILE)
    ys = moe_grouped_swiglu(xs, tile_expert, n_used, w_gate, w_up, w_down)
    return moe_combine_epilogue(ys, pos1, pos2, route, x, post_g, gate)


def _shift_rows(x, carry_row):
    rolled = pltpu.roll(x, 1, axis=0)
    row = lax.broadcasted_iota(jnp.int32, x.shape, 0)
    return jnp.where(row == 0, carry_row, rolled)


def _head_sums(x, seg):
    parts = [_dot(x[:, c:c + GROUP_LANES].astype(BF16), seg)
             for c in range(0, x.shape[1], GROUP_LANES)]
    return parts[0] if len(parts) == 1 else jnp.concatenate(parts, axis=1)


def _prep_kernel(*refs, has_vres, tm, low):
    (r_ref, k_ref, v_ref, p_ref) = refs[:4]
    pos = 4
    if has_vres:
        vfirst_ref = refs[pos]
        pos += 1
    (mu_ref, w0_ref, w2_ref, a0_ref, a2_ref, g2_ref) = refs[pos:pos + 6]
    pos += 6
    if has_vres:
        v0_ref, v2_ref = refs[pos:pos + 2]
        pos += 2
    (kk_ref, ka_ref, rk_ref, seg_ref, tri_ref) = refs[pos:pos + 5]
    pos += 5
    (rt_ref, kt_ref, at_ref, bt_ref, vb_ref, g_ref, bonus_ref, wl_ref) = refs[pos:pos + 8]
    pos += 8
    if not has_vres:
        vf_ref = refs[pos]
        pos += 1
    carry_rkv, carry_p = refs[pos:pos + 2]
    rw = r_ref.shape[-1]
    n_low = p_ref.shape[-1] // 2

    @pl.when(pl.program_id(1) == 0)
    def _():
        carry_rkv[...] = jnp.zeros_like(carry_rkv)
        carry_p[...] = jnp.zeros_like(carry_p)

    def lerp_prev(ref, idx):
        cur = ref[0].astype(F32)
        prev = _shift_rows(cur, carry_rkv[0:1, idx * rw:(idx + 1) * rw])
        carry_rkv[0:1, idx * rw:(idx + 1) * rw] = cur[tm - 1:tm, :]
        return cur + (prev - cur) * mu_ref[idx:idx + 1, :]

    r = lerp_prev(r_ref, 0)
    k = lerp_prev(k_ref, 1)
    v = lerp_prev(v_ref, 2)

    p = p_ref[0].astype(F32)
    p_b = p[:, n_low:]
    lowr = p[:, :n_low] + _shift_rows(p_b, carry_p[0:1, :])
    carry_p[0:1, :] = p_b[tm - 1:tm, :]
    o_w, o_a, o_g, o_v = low

    def low_slice(o):
        return lowr[:, o[0]:o[0] + o[1]]

    zw = w0_ref[...] + _dot(jnp.tanh(low_slice(o_w)).astype(BF16), w2_ref[...])
    logw = -jnp.exp(jnp.float32(-0.5)) * jax.nn.sigmoid(zw)
    asig = jax.nn.sigmoid(a0_ref[...] + _dot(low_slice(o_a).astype(BF16), a2_ref[...]))
    g_ref[0] = _dot(jax.nn.sigmoid(low_slice(o_g)).astype(BF16), g2_ref[...]).astype(g_ref.dtype)
    if has_vres:
        vgate = jax.nn.sigmoid(v0_ref[...] + _dot(low_slice(o_v).astype(BF16), v2_ref[...]))
        v = v + (vfirst_ref[0] - v) * vgate
    else:
        vf_ref[0] = v

    seg = seg_ref[...]
    kk = k * kk_ref[...]
    kk = kk / jnp.maximum(jnp.sqrt(_head_sums(kk * kk, seg)), 1e-12)
    k = k * (1.0 + (asig - 1.0) * ka_ref[...])
    bonus_ref[0] = (_head_sums(r * k * rk_ref[...], seg) * v).astype(bonus_ref.dtype)

    cum = jnp.dot(tri_ref[...], logw, preferred_element_type=F32, precision=lax.Precision.HIGHEST)
    e_pos = jnp.exp(cum)
    e_neg = jnp.exp(-cum)
    rt_ref[0] = (r * e_pos).astype(BF16)
    kt_ref[0] = (k * e_neg).astype(BF16)
    bt_ref[0] = (kk * asig * e_neg).astype(BF16)
    at_ref[0] = (-kk * jnp.exp(cum - logw)).astype(BF16)
    vb_ref[0] = v.astype(BF16)
    ends = [cum[c * CHUNK + CHUNK - 1:c * CHUNK + CHUNK, :] for c in range(tm // CHUNK)]
    if tm // CHUNK < SUBLANE:
        ends.append(jnp.zeros((SUBLANE - tm // CHUNK, rw), F32))
    wl_ref[0, 0] = jnp.exp(jnp.concatenate(ends, axis=0))


def rwkv_prep(h3, rw, low, low_blk, lp, vfirst, tm):
    bsz, s, cols = h3.shape
    n_low2 = cols // (low_blk + 1)
    assert n_low2 * (low_blk + 1) == cols and tm % CHUNK == 0 and tm // CHUNK <= SUBLANE
    has_vres = vfirst is not None
    blk = lambda j: pl.BlockSpec((1, tm, rw), lambda b, i, j=j: (b, i, j))
    full = lambda a: pl.BlockSpec(a.shape, lambda b, i: (0,) * a.ndim)
    row = lambda a: a.reshape(1, -1)
    args = [h3, h3, h3, h3]
    in_specs = [blk(0), blk(1), blk(2),
                pl.BlockSpec((1, tm, n_low2), lambda b, i: (b, i, low_blk))]
    if has_vres:
        args.append(vfirst)
        in_specs.append(blk(0))
    seg = (lax.broadcasted_iota(jnp.int32, (GROUP_LANES, GROUP_LANES), 0) // HEAD ==
           lax.broadcasted_iota(jnp.int32, (GROUP_LANES, GROUP_LANES), 1) // HEAD).astype(BF16)
    ti = lax.broadcasted_iota(jnp.int32, (tm, tm), 0)
    tj = lax.broadcasted_iota(jnp.int32, (tm, tm), 1)
    tri = ((ti >= tj) & (ti // CHUNK == tj // CHUNK)).astype(F32)
    small = [lp['mu_rkv'].reshape(3, rw), row(lp['w0']), lp['w2p'], row(lp['a0']), lp['a2p'],
             lp['g2p']]
    if has_vres:
        small += [row(lp['v0']), lp['v2p']]
    small += [row(lp['k_k']), row(lp['k_a']), row(lp['r_k']), seg, tri]
    args += small
    in_specs += [full(a) for a in small]
    tok = lambda dt: jax.ShapeDtypeStruct((bsz, s, rw), dt)
    out_shape = [tok(BF16)] * 7 + [jax.ShapeDtypeStruct((bsz, s // tm, SUBLANE, rw), F32)]
    out_specs = [blk(0)] * 7 + [pl.BlockSpec((1, 1, SUBLANE, rw), lambda b, i: (b, i, 0, 0))]
    if not has_vres:
        out_shape.append(tok(F32))
        out_specs.append(blk(0))
    return pl.pallas_call(
        functools.partial(_prep_kernel, has_vres=has_vres, tm=tm, low=low),
        grid=(bsz, s // tm),
        in_specs=in_specs,
        out_specs=out_specs,
        out_shape=out_shape,
        scratch_shapes=[pltpu.VMEM((SUBLANE, 3 * rw), F32), pltpu.VMEM((SUBLANE, n_low2 // 2), F32)],
        compiler_params=_params("arbitrary", "arbitrary"),
        name="rwkv_prep",
    )(*args)


def _expand_heads(x, head_mask):
    return jnp.where(head_mask, jnp.concatenate([x] * HEADS_PER_GROUP, axis=0),
                     jnp.zeros((), x.dtype))


def _scan_kernel(rt_ref, kt_ref, at_ref, bt_ref, v_ref, g_ref, bonus_ref, wl_ref,
                 lng_ref, lnb_ref, o_ref, state_ref, *, n_chunks, n_groups):
    rows = HEADS_PER_GROUP * CHUNK
    t_i = lax.broadcasted_iota(jnp.int32, (CHUNK, rows), 0)
    s_i = lax.broadcasted_iota(jnp.int32, (CHUNK, rows), 1) % CHUNK
    strict = t_i > s_i
    incl = t_i >= s_i
    eye = (t_i == s_i).astype(F32)
    ri = lax.broadcasted_iota(jnp.int32, (rows, GROUP_LANES), 0)
    ci = lax.broadcasted_iota(jnp.int32, (rows, GROUP_LANES), 1)
    head_mask = ri // CHUNK == ci // HEAD
    gi = lax.broadcasted_iota(jnp.int32, (GROUP_LANES, GROUP_LANES), 0)
    gj = lax.broadcasted_iota(jnp.int32, (GROUP_LANES, GROUP_LANES), 1)
    same_head = gi // HEAD == gj // HEAD
    avg = jnp.where(same_head, 1.0 / HEAD, 0.0).astype(BF16)
    nt = (((1,), (1,)), ((), ()))
    tn = (((0,), (0,)), ((), ()))
    bd = lambda x: _expand_heads(x.astype(BF16), head_mask)

    @pl.when(pl.program_id(2) == 0)
    def _():
        state_ref[...] = jnp.zeros_like(state_ref)

    chains = [(c, g) for c in range(n_chunks) for g in range(n_groups)]

    def window(ref, c, g):
        return ref[0, c * CHUNK:(c + 1) * CHUNK, g * GROUP_LANES:(g + 1) * GROUP_LANES]

    ops, a_ab, a_ak, m_rbk, t_inv, pw = {}, {}, {}, {}, {}, {}
    for ch in chains:
        a_t, r_t, b_t, k_t, v = (window(ref, *ch) for ref in (at_ref, rt_ref, bt_ref, kt_ref, v_ref))
        ops[ch] = (a_t, r_t, jnp.concatenate([b_t, k_t], axis=0), v, bd(v))
        p = lax.dot_general(jnp.concatenate([a_t, r_t], axis=0),
                            jnp.concatenate([bd(b_t), bd(k_t)], axis=0), nt,
                            preferred_element_type=F32)
        a_ab[ch] = jnp.where(strict, p[:CHUNK, :rows], 0.0)
        a_ak[ch] = jnp.where(strict, p[:CHUNK, rows:], 0.0).astype(BF16)
        m_rbk[ch] = jnp.where(jnp.concatenate([incl, incl], axis=1), p[CHUNK:], 0.0).astype(BF16)
    for ch in chains:
        t_inv[ch] = eye + a_ab[ch]
        pw[ch] = _dot(a_ab[ch].astype(BF16), bd(a_ab[ch]))
    step = 2
    while 2 * step < CHUNK:
        for ch in chains:
            both = _dot(jnp.concatenate([t_inv[ch], pw[ch]], axis=0).astype(BF16), bd(pw[ch]))
            t_inv[ch] = t_inv[ch] + both[:CHUNK]
            pw[ch] = both[CHUNK:]
        step *= 2
    taw = {}
    for ch in chains:
        t_fin = (t_inv[ch] + _dot(t_inv[ch].astype(BF16), bd(pw[ch]))).astype(BF16)
        a_t, _, _, _, v_bd = ops[ch]
        av = _dot(a_ak[ch], v_bd)
        taw[ch] = _dot(t_fin, jnp.concatenate([bd(a_t), bd(av)], axis=1))

    ys = {}
    for c in range(n_chunks):
        for g in range(n_groups):
            _, r_t, bk, v, v_bd = ops[c, g]
            lanes = slice(g * GROUP_LANES, (g + 1) * GROUP_LANES)
            state = state_ref[g]
            ars = lax.dot_general(
                jnp.concatenate([taw[c, g][:, :GROUP_LANES].astype(BF16), r_t], axis=0),
                state.astype(BF16), nt, preferred_element_type=F32)
            u = (ars[:CHUNK] + taw[c, g][:, GROUP_LANES:]).astype(BF16)
            upd = lax.dot_general(jnp.concatenate([u, v], axis=0), bk, tn,
                                  preferred_element_type=F32)
            state_ref[g] = (state + jnp.where(same_head, upd, 0.0)) * wl_ref[0, 0, c:c + 1, lanes]
            ys[c, g] = ars[CHUNK:] + _dot(m_rbk[c, g], jnp.concatenate([bd(u), v_bd], axis=0))
    for g in range(n_groups):
        lanes = slice(g * GROUP_LANES, (g + 1) * GROUP_LANES)
        y = jnp.concatenate([ys[c, g] for c in range(n_chunks)], axis=0)
        dlt = y - _dot(y.astype(BF16), avg)
        var = _dot((dlt * dlt).astype(BF16), avg)
        yn = dlt * lax.rsqrt(var + GN_EPS) * lng_ref[:, lanes] + lnb_ref[:, lanes]
        o_ref[0, :, lanes] = ((yn + bonus_ref[0, :, lanes].astype(F32))
                              * g_ref[0, :, lanes].astype(F32)).astype(o_ref.dtype)


def wkv_scan(rt, kt, at, bt, vb, g, bonus, wl, lnx_g, lnx_b, tm):
    bsz, s, rw = rt.shape
    n_groups = SCAN_GROUPS if rw % (SCAN_GROUPS * GROUP_LANES) == 0 else 1
    width = n_groups * GROUP_LANES
    blk = pl.BlockSpec((1, tm, width), lambda b, hg, i: (b, i, hg))
    vec = pl.BlockSpec((1, width), lambda b, hg, i: (0, hg))
    return pl.pallas_call(
        functools.partial(_scan_kernel, n_chunks=tm // CHUNK, n_groups=n_groups),
        grid=(bsz, rw // width, s // tm),
        in_specs=[blk] * 7 + [pl.BlockSpec((1, 1, SUBLANE, width),
                                           lambda b, hg, i: (b, i, 0, hg)), vec, vec],
        out_specs=blk,
        out_shape=jax.ShapeDtypeStruct((bsz, s, rw), BF16),
        scratch_shapes=[pltpu.VMEM((n_groups, GROUP_LANES, GROUP_LANES), F32)],
        compiler_params=_params("arbitrary", "arbitrary", "arbitrary"),
        name="wkv_scan",
    )(rt, kt, at, bt, vb, g, bonus, wl, lnx_g.reshape(1, rw), lnx_b.reshape(1, rw))


CONV_ROWS = 64
CONV_LANES = 256


def _conv_kernel(val_ref, gate_ref, w_ref, b_ref, lg_ref, lb_ref, o_ref, win_ref, y_ref,
                 *, tm, taps, halo):
    keep = halo + SUBLANE

    @pl.when(pl.program_id(1) == 0)
    def _():
        win_ref[:, 0:keep, :] = jnp.zeros((SUBLANE, keep, win_ref.shape[2]), F32)

    @pl.when(pl.program_id(1) != 0)
    def _():
        win_ref[:, 0:keep, :] = win_ref[:, tm:tm + keep, :]

    u = val_ref[0].astype(F32) * jax.nn.sigmoid(gate_ref[0].astype(F32))
    for r in range(SUBLANE):
        win_ref[r, halo + r:halo + r + tm, :] = u
    for c0 in range(0, u.shape[1], CONV_LANES):
        lanes = slice(c0, min(c0 + CONV_LANES, u.shape[1]))
        for r0 in range(0, tm, CONV_ROWS):
            n = min(CONV_ROWS, tm - r0)
            acc = jnp.broadcast_to(b_ref[:, lanes], (n, lanes.stop - lanes.start))
            for j in range(taps):
                q, r = divmod(taps - 1 - j, SUBLANE)
                off = halo - SUBLANE * q + r0
                acc = acc + w_ref[j:j + 1, lanes] * win_ref[r, off:off + n, lanes]
            y_ref[r0:r0 + n, lanes] = acc
    acc = y_ref[...]
    mu = jnp.mean(acc, axis=-1, keepdims=True)
    d = acc - mu
    var = jnp.mean(d * d, axis=-1, keepdims=True)
    z = d * lax.rsqrt(var + LN_EPS) * lg_ref[...] + lb_ref[...]
    o_ref[0] = (z * jax.nn.sigmoid(z)).astype(o_ref.dtype)


def conv_group(h3, rw, cw, conv_w, conv_b, ln_g, ln_b):
    bsz, s, _ = h3.shape
    taps = conv_w.shape[0]
    halo = _round_up(taps - 1, SUBLANE)
    tm = _pick(s, 128)
    assert (3 * rw) % cw == 0 and tm >= halo + SUBLANE
    c0 = (3 * rw) // cw
    row = pl.BlockSpec((1, cw), lambda b, i: (0, 0))
    return pl.pallas_call(
        functools.partial(_conv_kernel, tm=tm, taps=taps, halo=halo),
        grid=(bsz, s // tm),
        in_specs=[pl.BlockSpec((1, tm, cw), lambda b, i: (b, i, c0)),
                  pl.BlockSpec((1, tm, cw), lambda b, i: (b, i, c0 + 1)),
                  pl.BlockSpec((taps, cw), lambda b, i: (0, 0)), row, row, row],
        out_specs=pl.BlockSpec((1, tm, cw), lambda b, i: (b, i, 0)),
        out_shape=jax.ShapeDtypeStruct((bsz, s, cw), BF16),
        scratch_shapes=[pltpu.VMEM((SUBLANE, halo + tm + SUBLANE, cw), F32),
                        pltpu.VMEM((tm, cw), F32)],
        compiler_params=_params("arbitrary", "arbitrary"),
        name="conv_group",
    )(h3, h3, conv_w, conv_b.reshape(1, cw), ln_g.reshape(1, cw), ln_b.reshape(1, cw))


def _pad_cols(w, n):
    return jnp.pad(w, ((0, 0), (0, n - w.shape[1])))


def _pad_rows(w, n):
    return jnp.pad(w, ((0, n - w.shape[0]), (0, 0)))


def _mixer_weights(lp, has_vres):
    names = [('w', 'mu_w', 'w1', 'w2'), ('a', 'mu_a', 'a1', 'a2'), ('g', 'mu_g', 'g1', 'g2')]
    if has_vres:
        names.append(('v', 'mu_v', 'v1', 'v2'))
    cur, prev, low, off = [], [], [], 0
    out = dict(lp)
    for tag, mu, w1, w2 in names:
        rank = lp[w1].shape[1]
        rpad = _round_up(rank, LANE)
        cur.append(_pad_cols((1.0 - lp[mu])[:, None] * lp[w1], rpad))
        prev.append(_pad_cols(lp[mu][:, None] * lp[w1], rpad))
        out[w2 + 'p'] = _pad_rows(lp[w2], rpad).astype(BF16)
        low.append((off, rpad))
        off += rpad
    if not has_vres:
        low.append((0, 0))
    in_cols = lp['w_in'].shape[1]
    low_start = _round_up(in_cols, 2 * off)
    out['w_big'] = jnp.concatenate([_pad_cols(lp['w_in'], low_start)] + cur + prev,
                                   axis=1).astype(BF16)
    return out, tuple(low), low_start // (2 * off)


def _mixer(xm, x, lp, mods, vfirst, nxt, router):
    bsz, s, d = x.shape
    rw = lp['w0'].shape[0]
    cw = lp['conv_b'].shape[0]
    has_vres = vfirst is not None
    lw, low, low_blk = _mixer_weights(lp, has_vres)
    n_cols = lw['w_big'].shape[1]
    h = matmul([xm.reshape(bsz * s, d)], [lw['w_big']], BF16, tn_pref=n_cols // (low_blk + 1),
               name="in_projection")
    h3 = h.reshape(bsz, s, n_cols)
    tm = _pick(s, 256)
    outs = rwkv_prep(h3, rw, low, low_blk, lw, vfirst, tm)
    rt, kt, at, bt, vb, g, bonus, wl = outs[:8]
    v_out = vfirst if has_vres else outs[8]
    y_rwkv = wkv_scan(rt, kt, at, bt, vb, g, bonus, wl, lp['lnx_g'], lp['lnx_b'], tm)
    u = conv_group(h3, rw, cw, lp['conv_w'], lp['conv_b'], lp['conv_ln_g'], lp['conv_ln_b'])
    w_out = lp['w_out'].astype(BF16)
    o = matmul([y_rwkv.reshape(bsz * s, rw), u.reshape(bsz * s, cw)], [w_out[:rw], w_out[rw:]],
               BF16, name="out_projection")
    res = sublayer_epilogue(o.reshape(bsz, s, d), x, lp['mix_post_g'], mods['gt_m'], nxt, router)
    return res, v_out


def _split_layer(p, prefix, has_vres):
    keys = ['ada_w', 'ada_b', 'mix_pre_g', 'mix_post_g', 'w_in', 'mu_rkv', 'mu_w', 'mu_a', 'mu_g',
            'w0', 'w1', 'w2', 'a0', 'a1', 'a2', 'g1', 'g2', 'k_k', 'k_a', 'r_k', 'lnx_g', 'lnx_b',
            'conv_w', 'conv_b', 'conv_ln_g', 'conv_ln_b', 'w_out', 'ffn_pre_g', 'ffn_post_g']
    if has_vres:
        keys += ['mu_v', 'v0', 'v1', 'v2']
    return {k: p[prefix + k] for k in keys}


def _forward(p):
    x = p['x']
    c = p['c']
    bsz, s, d = x.shape
    layers = [_split_layer(p, 'l0_', False), _split_layer(p, 'l1_', True)]
    mods = []
    for lp in layers:
        mod = ada_modulation(c, lp['ada_w'], lp['ada_b'])
        mods.append(dict(zip(['sh_m', 'sc_m', 'gt_m', 'sh_f', 'sc_f', 'gt_f'],
                             jnp.split(mod, 6, axis=-1))))

    lp, md = layers[0], mods[0]
    xm = norm_modulate(x, lp['mix_pre_g'], md['sh_m'], md['sc_m'])
    (x, xf), vfirst = _mixer(xm, x, lp, md, None, (lp['ffn_pre_g'], md['sh_f'], md['sc_f']), None)
    y = dense_swiglu(xf.reshape(bsz * s, d), p['l0_ffn_w_gate'].astype(BF16),
                     p['l0_ffn_w_up'].astype(BF16), p['l0_ffn_w_down'].astype(BF16))
    nlp, nmd = layers[1], mods[1]
    x, xm = sublayer_epilogue(y.reshape(bsz, s, d), x, lp['ffn_post_g'], md['gt_f'],
                              (nlp['mix_pre_g'], nmd['sh_m'], nmd['sc_m']))

    lp, md = layers[1], mods[1]
    (x, xf, route), _ = _mixer(xm, x, lp, md, vfirst, (lp['ffn_pre_g'], md['sh_f'], md['sc_f']),
                               (p['l1_router_w'], p['l1_router_b']))
    return moe_sublayer(xf, route, x, lp['ffn_post_g'], md['gt_f'],
                        p['l1_moe_w_gate'].astype(BF16), p['l1_moe_w_up'].astype(BF16),
                        p['l1_moe_w_down'].astype(BF16))


_ARG_NAMES = (
    'x', 'c',
    'l0_ada_w', 'l0_ada_b', 'l0_mix_pre_g', 'l0_mix_post_g', 'l0_w_in', 'l0_mu_rkv', 'l0_mu_w',
    'l0_mu_a', 'l0_mu_g', 'l0_w0', 'l0_w1', 'l0_w2', 'l0_a0', 'l0_a1', 'l0_a2', 'l0_g1', 'l0_g2',
    'l0_k_k', 'l0_k_a', 'l0_r_k', 'l0_lnx_g', 'l0_lnx_b', 'l0_conv_w', 'l0_conv_b', 'l0_conv_ln_g',
    'l0_conv_ln_b', 'l0_w_out',
    'l0_ffn_pre_g', 'l0_ffn_post_g', 'l0_ffn_w_gate', 'l0_ffn_w_up', 'l0_ffn_w_down',
    'l1_ada_w', 'l1_ada_b', 'l1_mix_pre_g', 'l1_mix_post_g', 'l1_w_in', 'l1_mu_rkv', 'l1_mu_w',
    'l1_mu_a', 'l1_mu_g', 'l1_w0', 'l1_w1', 'l1_w2', 'l1_a0', 'l1_a1', 'l1_a2', 'l1_g1', 'l1_g2',
    'l1_k_k', 'l1_k_a', 'l1_r_k', 'l1_lnx_g', 'l1_lnx_b', 'l1_conv_w', 'l1_conv_b', 'l1_conv_ln_g',
    'l1_conv_ln_b', 'l1_w_out', 'l1_mu_v', 'l1_v0', 'l1_v1', 'l1_v2',
    'l1_ffn_pre_g', 'l1_ffn_post_g', 'l1_router_w', 'l1_router_b', 'l1_moe_w_gate',
    'l1_moe_w_up', 'l1_moe_w_down')


def kernel(x, c, l0_ada_w, l0_ada_b, l0_mix_pre_g, l0_mix_post_g, l0_w_in, l0_mu_rkv, l0_mu_w, l0_mu_a, l0_mu_g, l0_w0, l0_w1, l0_w2, l0_a0, l0_a1, l0_a2, l0_g1, l0_g2, l0_k_k, l0_k_a, l0_r_k, l0_lnx_g, l0_lnx_b, l0_conv_w, l0_conv_b, l0_conv_ln_g, l0_conv_ln_b, l0_w_out, l0_ffn_pre_g, l0_ffn_post_g, l0_ffn_w_gate, l0_ffn_w_up, l0_ffn_w_down, l1_ada_w, l1_ada_b, l1_mix_pre_g, l1_mix_post_g, l1_w_in, l1_mu_rkv, l1_mu_w, l1_mu_a, l1_mu_g, l1_w0, l1_w1, l1_w2, l1_a0, l1_a1, l1_a2, l1_g1, l1_g2, l1_k_k, l1_k_a, l1_r_k, l1_lnx_g, l1_lnx_b, l1_conv_w, l1_conv_b, l1_conv_ln_g, l1_conv_ln_b, l1_w_out, l1_mu_v, l1_v0, l1_v1, l1_v2, l1_ffn_pre_g, l1_ffn_post_g, l1_router_w, l1_router_b, l1_moe_w_gate, l1_moe_w_up, l1_moe_w_down):
    args = (x, c, l0_ada_w, l0_ada_b, l0_mix_pre_g, l0_mix_post_g, l0_w_in, l0_mu_rkv, l0_mu_w, l0_mu_a, l0_mu_g, l0_w0, l0_w1, l0_w2, l0_a0, l0_a1, l0_a2, l0_g1, l0_g2, l0_k_k, l0_k_a, l0_r_k, l0_lnx_g, l0_lnx_b, l0_conv_w, l0_conv_b, l0_conv_ln_g, l0_conv_ln_b, l0_w_out, l0_ffn_pre_g, l0_ffn_post_g, l0_ffn_w_gate, l0_ffn_w_up, l0_ffn_w_down, l1_ada_w, l1_ada_b, l1_mix_pre_g, l1_mix_post_g, l1_w_in, l1_mu_rkv, l1_mu_w, l1_mu_a, l1_mu_g, l1_w0, l1_w1, l1_w2, l1_a0, l1_a1, l1_a2, l1_g1, l1_g2, l1_k_k, l1_k_a, l1_r_k, l1_lnx_g, l1_lnx_b, l1_conv_w, l1_conv_b, l1_conv_ln_g, l1_conv_ln_b, l1_w_out, l1_mu_v, l1_v0, l1_v1, l1_v2, l1_ffn_pre_g, l1_ffn_post_g, l1_router_w, l1_router_b, l1_moe_w_gate, l1_moe_w_up, l1_moe_w_down)
    return _forward(dict(zip(_ARG_NAMES, args)))
```

```python
import functools

import jax
import jax.numpy as jnp
from jax import lax
from jax.experimental import pallas as pl
from jax.experimental.pallas import tpu as pltpu

F32 = jnp.float32
BF16 = jnp.bfloat16

HEAD = 64
HEADS_PER_GROUP = 4
GROUP_LANES = HEAD * HEADS_PER_GROUP
CHUNK = 64
SCAN_GROUPS = 4
LANE = 128
SUBLANE = 8
TOP_K = 2
RMS_EPS = 1e-6
GN_EPS = 64e-5
LN_EPS = 1e-5
VMEM_LIMIT_BYTES = 60 * 1024 * 1024


def _params(*sem):
    return pltpu.CompilerParams(dimension_semantics=sem, vmem_limit_bytes=VMEM_LIMIT_BYTES)


def _round_up(n, m):
    return (n + m - 1) // m * m


def _pick(n, pref):
    if n <= pref:
        return n
    t = pref
    while t >= LANE:
        if n % t == 0:
            return t
        t -= LANE
    return n


def _dot(a, b):
    return jnp.dot(a, b, preferred_element_type=F32)


def _rms(x, g):
    return x * lax.rsqrt(jnp.mean(x * x, axis=-1, keepdims=True) + RMS_EPS) * g


def _ada_kernel(c_ref, w_ref, b_ref, o_ref):
    c = c_ref[...]
    s = c * jax.nn.sigmoid(c)
    o_ref[...] = jnp.dot(s, w_ref[...], preferred_element_type=F32,
                         precision=lax.Precision.HIGHEST) + b_ref[...]


def ada_modulation(c, ada_w, ada_b):
    bsz, d = c.shape
    n = ada_w.shape[1]
    rows = _round_up(bsz, SUBLANE)
    cp = jnp.zeros((rows, d), F32).at[:bsz].set(c)
    tn = _pick(n, 1024)
    out = pl.pallas_call(
        _ada_kernel,
        grid=(n // tn,),
        in_specs=[pl.BlockSpec((rows, d), lambda j: (0, 0)),
                  pl.BlockSpec((d, tn), lambda j: (0, j)),
                  pl.BlockSpec((1, tn), lambda j: (0, j))],
        out_specs=pl.BlockSpec((rows, tn), lambda j: (0, j)),
        out_shape=jax.ShapeDtypeStruct((rows, n), F32),
        compiler_params=_params("arbitrary"),
        name="ada_matvec",
    )(cp, ada_w, ada_b.reshape(1, n))
    return out[:bsz]


def _norm_mod_kernel(x_ref, g_ref, sh_ref, sc_ref, o_ref):
    xn = _rms(x_ref[0], g_ref[...])
    o_ref[0] = (xn * (1.0 + sc_ref[0]) + sh_ref[0]).astype(o_ref.dtype)


def norm_modulate(x, g, shift, scale):
    bsz, s, d = x.shape
    tm = _pick(s, 512)
    vec = pl.BlockSpec((1, 1, d), lambda b, i: (b, 0, 0))
    return pl.pallas_call(
        _norm_mod_kernel,
        grid=(bsz, s // tm),
        in_specs=[pl.BlockSpec((1, tm, d), lambda b, i: (b, i, 0)),
                  pl.BlockSpec((1, d), lambda b, i: (0, 0)), vec, vec],
        out_specs=pl.BlockSpec((1, tm, d), lambda b, i: (b, i, 0)),
        out_shape=jax.ShapeDtypeStruct((bsz, s, d), BF16),
        compiler_params=_params("arbitrary", "arbitrary"),
        name="norm_modulate",
    )(x, g.reshape(1, d), shift.reshape(bsz, 1, d), scale.reshape(bsz, 1, d))


def _side_cast_plan(w, n_steps, step_of):
    *lead, r, c = w.shape
    stack = lead[0] if lead else 1
    rb = next((cand for cand in range(2 * SUBLANE, r, 2 * SUBLANE)
               if r % cand == 0 and stack * (r // cand) <= n_steps), r)
    per = r // rb
    n_blocks = stack * per
    assert n_blocks <= n_steps, (w.shape, n_steps)

    def index(*grid):
        s = jnp.minimum(step_of(*grid), n_blocks - 1)
        return (s // per, s % per, 0) if lead else (s, 0)

    return pl.BlockSpec((None, rb, c) if lead else (rb, c), index), n_blocks


def _run_side_casts(step, side_in, side_out, side_blocks):
    for src, dst, n_blocks in zip(side_in, side_out, side_blocks):
        @pl.when(step < n_blocks)
        def _(src=src, dst=dst):
            dst[...] = src[...].astype(dst.dtype)


def _mm_kernel(*refs, n_seg, side_blocks):
    n_side = len(side_blocks)
    side_in = refs[2 * n_seg:2 * n_seg + n_side]
    o_ref = refs[2 * n_seg + n_side]
    side_out = refs[2 * n_seg + n_side + 1:]
    acc = _dot(refs[0][...], refs[n_seg][...])
    for s in range(1, n_seg):
        acc = acc + _dot(refs[s][...], refs[n_seg + s][...])
    o_ref[...] = acc.astype(o_ref.dtype)
    _run_side_casts(pl.program_id(0) * pl.num_programs(1) + pl.program_id(1),
                    side_in, side_out, side_blocks)


def matmul(a_list, w_list, out_dtype, tm_pref=1024, tn_pref=1024, name="matmul", side=()):
    m = a_list[0].shape[0]
    n = w_list[0].shape[1]
    tm = _pick(m, tm_pref)
    tn = _pick(n, tn_pref)
    n_seg = len(a_list)
    n_j = n // tn
    once = dict(pipeline_mode=pl.Buffered(1)) if side else {}
    in_specs = [pl.BlockSpec((tm, a.shape[1]), lambda i, j: (i, 0), **once) for a in a_list]
    in_specs += [pl.BlockSpec((w.shape[0], tn), lambda i, j: (0, j)) for w in w_list]
    plans = [_side_cast_plan(w, (m // tm) * n_j, lambda i, j: i * n_j + j) for w in side]
    outs = pl.pallas_call(
        functools.partial(_mm_kernel, n_seg=n_seg, side_blocks=tuple(nb for _, nb in plans)),
        grid=(m // tm, n_j),
        in_specs=in_specs + [spec for spec, _ in plans],
        out_specs=[pl.BlockSpec((tm, tn), lambda i, j: (i, j))] + [spec for spec, _ in plans],
        out_shape=[jax.ShapeDtypeStruct((m, n), out_dtype)]
        + [jax.ShapeDtypeStruct(w.shape, BF16) for w in side],
        compiler_params=_params("arbitrary", "arbitrary"),
        name=name,
    )(*a_list, *w_list, *side)
    return outs[0], tuple(outs[1:])


ROUTE_E1, ROUTE_E2, ROUTE_W1, ROUTE_W2 = 0, 1, 2, 3


def _top2_route(logits, n_experts):
    lane = lax.broadcasted_iota(jnp.int32, logits.shape, 1)
    neg = jnp.float32(-jnp.inf)
    lg = jnp.where(lane < n_experts, logits, neg)
    v1 = jnp.max(lg, axis=-1, keepdims=True)
    i1 = jnp.min(jnp.where(lg == v1, lane, LANE), axis=-1, keepdims=True)
    lg2 = jnp.where(lane == i1, neg, lg)
    v2 = jnp.max(lg2, axis=-1, keepdims=True)
    i2 = jnp.min(jnp.where(lg2 == v2, lane, LANE), axis=-1, keepdims=True)
    e2 = jnp.exp(v2 - v1)
    w1 = 1.0 / (1.0 + e2)
    w2 = e2 / (1.0 + e2)
    rec = jnp.where(lane == ROUTE_E1, i1.astype(F32), 0.0)
    rec = jnp.where(lane == ROUTE_E2, i2.astype(F32), rec)
    rec = jnp.where(lane == ROUTE_W1, w1, rec)
    return jnp.where(lane == ROUTE_W2, w2, rec)


def _pack_halves(x):
    half = x.shape[1] // 2
    lo = lax.bitcast_convert_type(x[:, :half].astype(BF16).astype(F32), jnp.uint32)
    hi = lax.bitcast_convert_type(x[:, half:].astype(BF16).astype(F32), jnp.uint32)
    return hi | (lo >> 16)


def _unpack_halves(w):
    lo = lax.bitcast_convert_type(w << 16, F32)
    hi = lax.bitcast_convert_type(w & jnp.uint32(0xFFFF0000), F32)
    return lo, hi


def _epilogue_kernel(*refs, has_next, n_experts):
    y_ref, x_ref, post_g_ref, gate_ref = refs[:4]
    pos = 4
    if has_next:
        pre_g_ref, sh_ref, sc_ref = refs[pos:pos + 3]
        pos += 3
    if n_experts:
        rw_ref, rb_ref = refs[pos:pos + 2]
        pos += 2
    outs = refs[pos:]
    x1 = x_ref[0] + gate_ref[0] * _rms(y_ref[0].astype(F32), post_g_ref[...])
    outs[0][0] = x1
    if has_next:
        xf = _rms(x1, pre_g_ref[...]) * (1.0 + sc_ref[0]) + sh_ref[0]
        outs[1][0] = _pack_halves(xf) if n_experts else xf.astype(outs[1].dtype)
        if n_experts:
            logits = jnp.dot(xf, rw_ref[...], preferred_element_type=F32,
                             precision=lax.Precision.HIGHEST) + rb_ref[...]
            outs[2][0] = _top2_route(logits, n_experts)


def sublayer_epilogue(y, x, post_g, gate, nxt=None, router=None):
    bsz, s, d = x.shape
    tm = _pick(s, 256)
    blk = pl.BlockSpec((1, tm, d), lambda b, i: (b, i, 0))
    row = pl.BlockSpec((1, d), lambda b, i: (0, 0))
    vec = pl.BlockSpec((1, 1, d), lambda b, i: (b, 0, 0))
    args = [y, x, post_g.reshape(1, d), gate.reshape(bsz, 1, d)]
    in_specs = [blk, blk, row, vec]
    out_shape = [jax.ShapeDtypeStruct((bsz, s, d), F32)]
    out_specs = [blk]
    n_experts = 0
    if nxt is not None:
        pre_g, shift, scale = nxt
        args += [pre_g.reshape(1, d), shift.reshape(bsz, 1, d), scale.reshape(bsz, 1, d)]
        in_specs += [row, vec, vec]
        if router is None:
            out_shape.append(jax.ShapeDtypeStruct((bsz, s, d), BF16))
            out_specs.append(blk)
        else:
            out_shape.append(jax.ShapeDtypeStruct((bsz, s, d // 2), jnp.uint32))
            out_specs.append(pl.BlockSpec((1, tm, d // 2), lambda b, i: (b, i, 0)))
        if router is not None:
            router_w, router_b = router
            n_experts = router_w.shape[1]
            rw = jnp.zeros((d, LANE), F32).at[:, :n_experts].set(router_w)
            rb = jnp.zeros((1, LANE), F32).at[0, :n_experts].set(router_b)
            args += [rw, rb]
            in_specs += [pl.BlockSpec((d, LANE), lambda b, i: (0, 0)),
                         pl.BlockSpec((1, LANE), lambda b, i: (0, 0))]
            out_shape.append(jax.ShapeDtypeStruct((bsz, s, LANE), F32))
            out_specs.append(pl.BlockSpec((1, tm, LANE), lambda b, i: (b, i, 0)))
    return pl.pallas_call(
        functools.partial(_epilogue_kernel, has_next=nxt is not None, n_experts=n_experts),
        grid=(bsz, s // tm),
        in_specs=in_specs,
        out_specs=out_specs,
        out_shape=out_shape,
        compiler_params=_params("arbitrary", "arbitrary"),
        name="sublayer_epilogue",
    )(*args)


def _ffn_kernel(*refs, side_blocks):
    n_side = len(side_blocks)
    x_ref, wg_ref, wu_ref, wd_ref = refs[:4]
    side_in = refs[4:4 + n_side]
    o_ref = refs[4 + n_side]
    side_out = refs[5 + n_side:5 + 2 * n_side]
    acc_ref = refs[5 + 2 * n_side]
    j = pl.program_id(1)

    @pl.when(j == 0)
    def _():
        acc_ref[...] = jnp.zeros_like(acc_ref)

    x = x_ref[...]
    hg = _dot(x, wg_ref[...])
    hu = _dot(x, wu_ref[...])
    h = (hg * jax.nn.sigmoid(hg) * hu).astype(BF16)
    acc_ref[...] += _dot(h, wd_ref[...])

    @pl.when(j == pl.num_programs(1) - 1)
    def _():
        o_ref[...] = acc_ref[...].astype(o_ref.dtype)

    _run_side_casts(pl.program_id(0) * pl.num_programs(1) + j, side_in, side_out, side_blocks)


def dense_swiglu(x, w_gate, w_up, w_down, side=()):
    m, d = x.shape
    f = w_gate.shape[1]
    tm = _pick(m, 512)
    tf = _pick(f, 512)
    n_j = f // tf
    plans = [_side_cast_plan(w, (m // tm) * n_j, lambda i, j: i * n_j + j) for w in side]
    once = dict(pipeline_mode=pl.Buffered(1))
    outs = pl.pallas_call(
        functools.partial(_ffn_kernel, side_blocks=tuple(nb for _, nb in plans)),
        grid=(m // tm, n_j),
        in_specs=[pl.BlockSpec((tm, d), lambda i, j: (i, 0), **once),
                  pl.BlockSpec((d, tf), lambda i, j: (0, j)),
                  pl.BlockSpec((d, tf), lambda i, j: (0, j)),
                  pl.BlockSpec((tf, d), lambda i, j: (j, 0))] + [spec for spec, _ in plans],
        out_specs=[pl.BlockSpec((tm, d), lambda i, j: (i, 0), **once)] + [spec for spec, _ in plans],
        out_shape=[jax.ShapeDtypeStruct((m, d), BF16)]
        + [jax.ShapeDtypeStruct(w.shape, BF16) for w in side],
        scratch_shapes=[pltpu.VMEM((tm, d), F32)],
        compiler_params=_params("arbitrary", "arbitrary"),
        name="dense_swiglu",
    )(x, w_gate, w_up, w_down, *side)
    return outs[0], tuple(outs[1:])


MOE_ROW_TILE = 512
DMA_ROWS = 256


def _rank_kernel(route_ref, tri_ref, rank_ref, count_ref, carry_ref):
    @pl.when(pl.program_id(0) == 0)
    def _():
        carry_ref[...] = jnp.zeros_like(carry_ref)

    route = route_ref[...]
    lane = lax.broadcasted_iota(jnp.int32, route.shape, 1)
    lane_f = lane.astype(F32)
    oh1 = lane_f == route[:, ROUTE_E1:ROUTE_E1 + 1]
    oh2 = lane_f == route[:, ROUTE_E2:ROUTE_E2 + 1]
    cnt = jnp.where(oh1 | oh2, 1.0, 0.0)
    before = _dot(tri_ref[...], cnt.astype(BF16)) + carry_ref[0:1, :]
    r1 = jnp.sum(jnp.where(oh1, before, 0.0), axis=-1, keepdims=True)
    r2 = jnp.sum(jnp.where(oh2, before, 0.0), axis=-1, keepdims=True)
    rank_ref[...] = jnp.where(lane == ROUTE_E1, r1, jnp.where(lane == ROUTE_E2, r2, 0.0))
    total = carry_ref[0:1, :] + jnp.sum(cnt, axis=0, keepdims=True)
    carry_ref[0:1, :] = total
    count_ref[...] = jnp.broadcast_to(total, count_ref.shape)


def expert_ranks(route):
    t = route.shape[0]
    tm = _pick(t, 256)
    ti = lax.broadcasted_iota(jnp.int32, (tm, tm), 0)
    tj = lax.broadcasted_iota(jnp.int32, (tm, tm), 1)
    tri = (ti > tj).astype(BF16)
    return pl.pallas_call(
        _rank_kernel,
        grid=(t // tm,),
        in_specs=[pl.BlockSpec((tm, LANE), lambda i: (i, 0)),
                  pl.BlockSpec((tm, tm), lambda i: (0, 0))],
        out_specs=[pl.BlockSpec((tm, LANE), lambda i: (i, 0)),
                   pl.BlockSpec((SUBLANE, LANE), lambda i: (0, 0))],
        out_shape=[jax.ShapeDtypeStruct((t, LANE), F32),
                   jax.ShapeDtypeStruct((SUBLANE, LANE), F32)],
        scratch_shapes=[pltpu.VMEM((SUBLANE, LANE), F32)],
        compiler_params=_params("arbitrary"),
        name="expert_ranks",
    )(route, tri)


def _row_copies(n_rows, make_copy):
    def start(r, carry):
        for cp in make_copy(r):
            cp.start()
        return carry

    def wait(r, carry):
        for cp in make_copy(r):
            cp.wait()
        return carry

    lax.fori_loop(0, n_rows, start, 0)
    lax.fori_loop(0, n_rows, wait, 0)


def _dispatch_kernel(pos1_ref, pos2_ref, x_ref, init_ref, o_ref, sem):
    del init_ref

    def make_copy(r):
        src = x_ref.at[pl.ds(r, 1)]
        return [pltpu.make_async_copy(src, o_ref.at[pl.ds(pos_ref[0, r], 1)], sem)
                for pos_ref in (pos1_ref, pos2_ref)]

    _row_copies(x_ref.shape[0], make_copy)


def moe_dispatch(x, pos1, pos2, n_rows):
    t, d = x.shape
    rows = pos1.shape[2]
    smem = pl.BlockSpec((None, 1, rows), lambda i: (i, 0, 0), memory_space=pltpu.SMEM)
    return pl.pallas_call(
        _dispatch_kernel,
        grid=(t // rows,),
        in_specs=[smem, smem, pl.BlockSpec((rows, d), lambda i: (i, 0)),
                  pl.BlockSpec(memory_space=pl.ANY)],
        out_specs=pl.BlockSpec(memory_space=pl.ANY),
        out_shape=jax.ShapeDtypeStruct((n_rows, d), x.dtype),
        scratch_shapes=[pltpu.SemaphoreType.DMA(())],
        input_output_aliases={3: 0},
        compiler_params=_params("arbitrary"),
        name="moe_dispatch",
    )(pos1, pos2, x, jnp.zeros((n_rows, d), x.dtype))


def _moe_kernel(tile_expert_ref, n_used_ref, x_ref, wg_ref, wu_ref, wd_ref, o_ref,
                xlo_ref, xhi_ref, acc_ref):
    del tile_expert_ref
    k = pl.program_id(1)
    half = xlo_ref.shape[1]

    @pl.when(k == 0)
    def _():
        acc_ref[...] = jnp.zeros_like(acc_ref)
        lo, hi = _unpack_halves(x_ref[...])
        xlo_ref[...] = lo.astype(BF16)
        xhi_ref[...] = hi.astype(BF16)

    @pl.when(pl.program_id(0) < n_used_ref[0])
    def _():
        xlo = xlo_ref[...]
        xhi = xhi_ref[...]
        hg = _dot(xlo, wg_ref[:half, :]) + _dot(xhi, wg_ref[half:, :])
        hu = _dot(xlo, wu_ref[:half, :]) + _dot(xhi, wu_ref[half:, :])
        h = (hg * jax.nn.sigmoid(hg) * hu).astype(BF16)
        acc_ref[...] += _dot(h, wd_ref[...])

    @pl.when(k == pl.num_programs(1) - 1)
    def _():
        o_ref[...] = _pack_halves(acc_ref[...])


def moe_grouped_swiglu(xs, tile_expert, n_used, w_gate, w_up, w_down):
    n_rows, half = xs.shape
    d = 2 * half
    f = w_gate.shape[2]
    tf = _pick(f, 512)
    live = lambda j, k, te, nu: jnp.where(j < nu[0], k, 0)
    grid_spec = pltpu.PrefetchScalarGridSpec(
        num_scalar_prefetch=2,
        grid=(n_rows // MOE_ROW_TILE, f // tf),
        in_specs=[pl.BlockSpec((MOE_ROW_TILE, half), lambda j, k, te, nu: (j, 0)),
                  pl.BlockSpec((None, d, tf), lambda j, k, te, nu: (te[j], 0, live(j, k, te, nu))),
                  pl.BlockSpec((None, d, tf), lambda j, k, te, nu: (te[j], 0, live(j, k, te, nu))),
                  pl.BlockSpec((None, tf, d), lambda j, k, te, nu: (te[j], live(j, k, te, nu), 0))],
        out_specs=pl.BlockSpec((MOE_ROW_TILE, half), lambda j, k, te, nu: (j, 0)),
        scratch_shapes=[pltpu.VMEM((MOE_ROW_TILE, half), BF16),
                        pltpu.VMEM((MOE_ROW_TILE, half), BF16),
                        pltpu.VMEM((MOE_ROW_TILE, d), F32)])
    return pl.pallas_call(
        _moe_kernel,
        grid_spec=grid_spec,
        out_shape=jax.ShapeDtypeStruct((n_rows, half), jnp.uint32),
        compiler_params=_params("arbitrary", "arbitrary"),
        name="moe_grouped_swiglu",
    )(tile_expert, n_used, xs, w_gate, w_up, w_down)


def _combine_kernel(pos1_ref, pos2_ref, route_ref, x_ref, post_g_ref, gate_ref, ys_ref, o_ref,
                    buf1, buf2, sem):
    def make_copy(r):
        return [pltpu.make_async_copy(ys_ref.at[pl.ds(pos_ref[0, r], 1)], buf.at[pl.ds(r, 1)], sem)
                for pos_ref, buf in ((pos1_ref, buf1), (pos2_ref, buf2))]

    _row_copies(buf1.shape[0], make_copy)
    route = route_ref[0]
    w1 = route[:, ROUTE_W1:ROUTE_W1 + 1]
    w2 = route[:, ROUTE_W2:ROUTE_W2 + 1]
    lo1, hi1 = _unpack_halves(buf1[...])
    lo2, hi2 = _unpack_halves(buf2[...])
    y = jnp.concatenate([w1 * lo1 + w2 * lo2, w1 * hi1 + w2 * hi2], axis=1)
    o_ref[0] = x_ref[0] + gate_ref[0] * _rms(y, post_g_ref[...])


def moe_combine_epilogue(ys, pos1, pos2, route, x, post_g, gate):
    bsz, s, d = x.shape
    rows = pos1.shape[2]
    n_i = s // rows
    smem = pl.BlockSpec((None, 1, rows), lambda b, i: (b * n_i + i, 0, 0),
                        memory_space=pltpu.SMEM)
    blk = pl.BlockSpec((1, rows, d), lambda b, i: (b, i, 0))
    return pl.pallas_call(
        _combine_kernel,
        grid=(bsz, n_i),
        in_specs=[smem, smem, pl.BlockSpec((1, rows, LANE), lambda b, i: (b, i, 0)), blk,
                  pl.BlockSpec((1, d), lambda b, i: (0, 0)),
                  pl.BlockSpec((1, 1, d), lambda b, i: (b, 0, 0)),
                  pl.BlockSpec(memory_space=pl.ANY)],
        out_specs=blk,
        out_shape=jax.ShapeDtypeStruct((bsz, s, d), F32),
        scratch_shapes=[pltpu.VMEM((rows, d // 2), jnp.uint32),
                        pltpu.VMEM((rows, d // 2), jnp.uint32),
                        pltpu.SemaphoreType.DMA(())],
        compiler_params=_params("arbitrary", "arbitrary"),
        name="moe_combine_epilogue",
    )(pos1, pos2, route, x, post_g.reshape(1, d), gate.reshape(bsz, 1, d), ys)


def moe_sublayer(xf, route, x, post_g, gate, w_gate, w_up, w_down):
    bsz, s, d = x.shape
    t = bsz * s
    n_e = w_gate.shape[0]
    rows = _pick(s, DMA_ROWS)
    assert (TOP_K * t) % MOE_ROW_TILE == 0
    n_tiles = TOP_K * t // MOE_ROW_TILE + n_e
    route2 = route.reshape(t, LANE)
    rank, count = expert_ranks(route2)
    counts = count[0, :n_e].astype(jnp.int32)
    tiles_e = (counts + MOE_ROW_TILE - 1) // MOE_ROW_TILE
    tile_end = jnp.cumsum(tiles_e)
    row_start = (tile_end - tiles_e) * MOE_ROW_TILE
    e1 = route2[:, ROUTE_E1].astype(jnp.int32)
    e2 = route2[:, ROUTE_E2].astype(jnp.int32)
    pos1 = (row_start[e1] + rank[:, ROUTE_E1].astype(jnp.int32)).reshape(t // rows, 1, rows)
    pos2 = (row_start[e2] + rank[:, ROUTE_E2].astype(jnp.int32)).reshape(t // rows, 1, rows)
    tile_expert = jnp.minimum(jnp.searchsorted(tile_end, jnp.arange(n_tiles), side='right'),
                              n_e - 1).astype(jnp.int32)
    n_used = tile_end[n_e - 1:].astype(jnp.int32)
    xs = moe_dispatch(xf.reshape(t, d // 2), pos1, pos2, n_tiles * MOE_ROW_TILE)
    ys = moe_grouped_swiglu(xs, tile_expert, n_used, w_gate, w_up, w_down)
    return moe_combine_epilogue(ys, pos1, pos2, route, x, post_g, gate)


def _shift_rows(x, carry_row):
    rolled = pltpu.roll(x, 1, axis=0)
    row = lax.broadcasted_iota(jnp.int32, x.shape, 0)
    return jnp.where(row == 0, carry_row, rolled)


def _head_sums(x, seg):
    parts = [_dot(x[:, c:c + GROUP_LANES].astype(BF16), seg)
             for c in range(0, x.shape[1], GROUP_LANES)]
    return parts[0] if len(parts) == 1 else jnp.concatenate(parts, axis=1)


def _prep_kernel(*refs, has_vres, tm, low):
    (r_ref, k_ref, v_ref, p_ref) = refs[:4]
    pos = 4
    if has_vres:
        vfirst_ref = refs[pos]
        pos += 1
    (mu_ref, w0_ref, w2_ref, a0_ref, a2_ref, g2_ref) = refs[pos:pos + 6]
    pos += 6
    if has_vres:
        v0_ref, v2_ref = refs[pos:pos + 2]
        pos += 2
    (kk_ref, ka_ref, rk_ref, seg_ref, tri_ref) = refs[pos:pos + 5]
    pos += 5
    (rt_ref, kt_ref, at_ref, bt_ref, vb_ref, g_ref, bonus_ref, wl_ref) = refs[pos:pos + 8]
    pos += 8
    if not has_vres:
        vf_ref = refs[pos]
        pos += 1
    carry_rkv, carry_p = refs[pos:pos + 2]
    rw = r_ref.shape[-1]
    n_low = p_ref.shape[-1] // 2

    @pl.when(pl.program_id(1) == 0)
    def _():
        carry_rkv[...] = jnp.zeros_like(carry_rkv)
        carry_p[...] = jnp.zeros_like(carry_p)

    def lerp_prev(ref, idx):
        cur = ref[0].astype(F32)
        prev = _shift_rows(cur, carry_rkv[0:1, idx * rw:(idx + 1) * rw])
        carry_rkv[0:1, idx * rw:(idx + 1) * rw] = cur[tm - 1:tm, :]
        return cur + (prev - cur) * mu_ref[idx:idx + 1, :]

    r = lerp_prev(r_ref, 0)
    k = lerp_prev(k_ref, 1)
    v = lerp_prev(v_ref, 2)

    p = p_ref[0].astype(F32)
    p_b = p[:, n_low:]
    lowr = p[:, :n_low] + _shift_rows(p_b, carry_p[0:1, :])
    carry_p[0:1, :] = p_b[tm - 1:tm, :]
    o_w, o_a, o_g, o_v = low

    def low_slice(o):
        return lowr[:, o[0]:o[0] + o[1]]

    zw = w0_ref[...] + _dot(jnp.tanh(low_slice(o_w)).astype(BF16), w2_ref[...])
    logw = -jnp.exp(jnp.float32(-0.5)) * jax.nn.sigmoid(zw)
    asig = jax.nn.sigmoid(a0_ref[...] + _dot(low_slice(o_a).astype(BF16), a2_ref[...]))
    g_ref[0] = _dot(jax.nn.sigmoid(low_slice(o_g)).astype(BF16), g2_ref[...]).astype(g_ref.dtype)
    if has_vres:
        vgate = jax.nn.sigmoid(v0_ref[...] + _dot(low_slice(o_v).astype(BF16), v2_ref[...]))
        v = v + (vfirst_ref[0] - v) * vgate
    else:
        vf_ref[0] = v

    seg = seg_ref[...]
    kk = k * kk_ref[...]
    kk = kk / jnp.maximum(jnp.sqrt(_head_sums(kk * kk, seg)), 1e-12)
    k = k * (1.0 + (asig - 1.0) * ka_ref[...])
    bonus_ref[0] = (_head_sums(r * k * rk_ref[...], seg) * v).astype(bonus_ref.dtype)

    cum = jnp.dot(tri_ref[...], logw, preferred_element_type=F32, precision=lax.Precision.HIGHEST)
    e_pos = jnp.exp(cum)
    e_neg = jnp.exp(-cum)
    rt_ref[0] = (r * e_pos).astype(BF16)
    kt_ref[0] = (k * e_neg).astype(BF16)
    bt_ref[0] = (kk * asig * e_neg).astype(BF16)
    at_ref[0] = (-kk * jnp.exp(cum - logw)).astype(BF16)
    vb_ref[0] = v.astype(BF16)
    ends = [cum[c * CHUNK + CHUNK - 1:c * CHUNK + CHUNK, :] for c in range(tm // CHUNK)]
    if tm // CHUNK < SUBLANE:
        ends.append(jnp.zeros((SUBLANE - tm // CHUNK, rw), F32))
    wl_ref[0, 0] = jnp.exp(jnp.concatenate(ends, axis=0))


def rwkv_prep(h3, rw, low, low_blk, lp, vfirst, tm):
    bsz, s, cols = h3.shape
    n_low2 = cols // (low_blk + 1)
    assert n_low2 * (low_blk + 1) == cols and tm % CHUNK == 0 and tm // CHUNK <= SUBLANE
    has_vres = vfirst is not None
    blk = lambda j: pl.BlockSpec((1, tm, rw), lambda b, i, j=j: (b, i, j))
    full = lambda a: pl.BlockSpec(a.shape, lambda b, i: (0,) * a.ndim)
    row = lambda a: a.reshape(1, -1)
    args = [h3, h3, h3, h3]
    in_specs = [blk(0), blk(1), blk(2),
                pl.BlockSpec((1, tm, n_low2), lambda b, i: (b, i, low_blk))]
    if has_vres:
        args.append(vfirst)
        in_specs.append(blk(0))
    seg = (lax.broadcasted_iota(jnp.int32, (GROUP_LANES, GROUP_LANES), 0) // HEAD ==
           lax.broadcasted_iota(jnp.int32, (GROUP_LANES, GROUP_LANES), 1) // HEAD).astype(BF16)
    ti = lax.broadcasted_iota(jnp.int32, (tm, tm), 0)
    tj = lax.broadcasted_iota(jnp.int32, (tm, tm), 1)
    tri = ((ti >= tj) & (ti // CHUNK == tj // CHUNK)).astype(F32)
    small = [lp['mu_rkv'].reshape(3, rw), row(lp['w0']), lp['w2p'], row(lp['a0']), lp['a2p'],
             lp['g2p']]
    if has_vres:
        small += [row(lp['v0']), lp['v2p']]
    small += [row(lp['k_k']), row(lp['k_a']), row(lp['r_k']), seg, tri]
    args += small
    in_specs += [full(a) for a in small]
    tok = lambda dt: jax.ShapeDtypeStruct((bsz, s, rw), dt)
    out_shape = [tok(BF16)] * 7 + [jax.ShapeDtypeStruct((bsz, s // tm, SUBLANE, rw), F32)]
    out_specs = [blk(0)] * 7 + [pl.BlockSpec((1, 1, SUBLANE, rw), lambda b, i: (b, i, 0, 0))]
    if not has_vres:
        out_shape.append(tok(F32))
        out_specs.append(blk(0))
    return pl.pallas_call(
        functools.partial(_prep_kernel, has_vres=has_vres, tm=tm, low=low),
        grid=(bsz, s // tm),
        in_specs=in_specs,
        out_specs=out_specs,
        out_shape=out_shape,
        scratch_shapes=[pltpu.VMEM((SUBLANE, 3 * rw), F32), pltpu.VMEM((SUBLANE, n_low2 // 2), F32)],
        compiler_params=_params("arbitrary", "arbitrary"),
        name="rwkv_prep",
    )(*args)


def _expand_heads(x, head_mask):
    return jnp.where(head_mask, jnp.concatenate([x] * HEADS_PER_GROUP, axis=0),
                     jnp.zeros((), x.dtype))


def _scan_kernel(rt_ref, kt_ref, at_ref, bt_ref, v_ref, g_ref, bonus_ref, wl_ref,
                 lng_ref, lnb_ref, o_ref, state_ref, *, n_chunks, n_groups):
    rows = HEADS_PER_GROUP * CHUNK
    t_i = lax.broadcasted_iota(jnp.int32, (CHUNK, rows), 0)
    s_i = lax.broadcasted_iota(jnp.int32, (CHUNK, rows), 1) % CHUNK
    strict = t_i > s_i
    incl = t_i >= s_i
    eye = (t_i == s_i).astype(F32)
    ri = lax.broadcasted_iota(jnp.int32, (rows, GROUP_LANES), 0)
    ci = lax.broadcasted_iota(jnp.int32, (rows, GROUP_LANES), 1)
    head_mask = ri // CHUNK == ci // HEAD
    gi = lax.broadcasted_iota(jnp.int32, (GROUP_LANES, GROUP_LANES), 0)
    gj = lax.broadcasted_iota(jnp.int32, (GROUP_LANES, GROUP_LANES), 1)
    same_head = gi // HEAD == gj // HEAD
    avg = jnp.where(same_head, 1.0 / HEAD, 0.0).astype(BF16)
    nt = (((1,), (1,)), ((), ()))
    tn = (((0,), (0,)), ((), ()))
    bd = lambda x: _expand_heads(x.astype(BF16), head_mask)

    @pl.when(pl.program_id(2) == 0)
    def _():
        state_ref[...] = jnp.zeros_like(state_ref)

    chains = [(c, g) for c in range(n_chunks) for g in range(n_groups)]

    def window(ref, c, g):
        return ref[0, c * CHUNK:(c + 1) * CHUNK, g * GROUP_LANES:(g + 1) * GROUP_LANES]

    ops, a_ab, a_ak, m_rbk, t_inv, pw = {}, {}, {}, {}, {}, {}
    for ch in chains:
        a_t, r_t, b_t, k_t, v = (window(ref, *ch) for ref in (at_ref, rt_ref, bt_ref, kt_ref, v_ref))
        ops[ch] = (a_t, r_t, jnp.concatenate([b_t, k_t], axis=0), v, bd(v))
        p = lax.dot_general(jnp.concatenate([a_t, r_t], axis=0),
                            jnp.concatenate([bd(b_t), bd(k_t)], axis=0), nt,
                            preferred_element_type=F32)
        a_ab[ch] = jnp.where(strict, p[:CHUNK, :rows], 0.0)
        a_ak[ch] = jnp.where(strict, p[:CHUNK, rows:], 0.0).astype(BF16)
        m_rbk[ch] = jnp.where(jnp.concatenate([incl, incl], axis=1), p[CHUNK:], 0.0).astype(BF16)
    for ch in chains:
        t_inv[ch] = eye + a_ab[ch]
        pw[ch] = _dot(a_ab[ch].astype(BF16), bd(a_ab[ch]))
    step = 2
    while 2 * step < CHUNK:
        for ch in chains:
            both = _dot(jnp.concatenate([t_inv[ch], pw[ch]], axis=0).astype(BF16), bd(pw[ch]))
            t_inv[ch] = t_inv[ch] + both[:CHUNK]
            pw[ch] = both[CHUNK:]
        step *= 2
    taw = {}
    for ch in chains:
        t_fin = (t_inv[ch] + _dot(t_inv[ch].astype(BF16), bd(pw[ch]))).astype(BF16)
        a_t, _, _, _, v_bd = ops[ch]
        av = _dot(a_ak[ch], v_bd)
        taw[ch] = _dot(t_fin, jnp.concatenate([bd(a_t), bd(av)], axis=1))

    ys = {}
    for c in range(n_chunks):
        for g in range(n_groups):
            _, r_t, bk, v, v_bd = ops[c, g]
            lanes = slice(g * GROUP_LANES, (g + 1) * GROUP_LANES)
            state = state_ref[g]
            ars = lax.dot_general(
                jnp.concatenate([taw[c, g][:, :GROUP_LANES].astype(BF16), r_t], axis=0),
                state.astype(BF16), nt, preferred_element_type=F32)
            u = (ars[:CHUNK] + taw[c, g][:, GROUP_LANES:]).astype(BF16)
            upd = lax.dot_general(jnp.concatenate([u, v], axis=0), bk, tn,
                                  preferred_element_type=F32)
            state_ref[g] = (state + jnp.where(same_head, upd, 0.0)) * wl_ref[0, 0, c:c + 1, lanes]
            ys[c, g] = ars[CHUNK:] + _dot(m_rbk[c, g], jnp.concatenate([bd(u), v_bd], axis=0))
    for g in range(n_groups):
        lanes = slice(g * GROUP_LANES, (g + 1) * GROUP_LANES)
        y = jnp.concatenate([ys[c, g] for c in range(n_chunks)], axis=0)
        dlt = y - _dot(y.astype(BF16), avg)
        var = _dot((dlt * dlt).astype(BF16), avg)
        yn = dlt * lax.rsqrt(var + GN_EPS) * lng_ref[:, lanes] + lnb_ref[:, lanes]
        o_ref[0, :, lanes] = ((yn + bonus_ref[0, :, lanes].astype(F32))
                              * g_ref[0, :, lanes].astype(F32)).astype(o_ref.dtype)


def wkv_scan(rt, kt, at, bt, vb, g, bonus, wl, lnx_g, lnx_b, tm):
    bsz, s, rw = rt.shape
    n_groups = SCAN_GROUPS if rw % (SCAN_GROUPS * GROUP_LANES) == 0 else 1
    width = n_groups * GROUP_LANES
    blk = pl.BlockSpec((1, tm, width), lambda b, hg, i: (b, i, hg))
    vec = pl.BlockSpec((1, width), lambda b, hg, i: (0, hg))
    return pl.pallas_call(
        functools.partial(_scan_kernel, n_chunks=tm // CHUNK, n_groups=n_groups),
        grid=(bsz, rw // width, s // tm),
        in_specs=[blk] * 7 + [pl.BlockSpec((1, 1, SUBLANE, width),
                                           lambda b, hg, i: (b, i, 0, hg)), vec, vec],
        out_specs=blk,
        out_shape=jax.ShapeDtypeStruct((bsz, s, rw), BF16),
        scratch_shapes=[pltpu.VMEM((n_groups, GROUP_LANES, GROUP_LANES), F32)],
        compiler_params=_params("arbitrary", "arbitrary", "arbitrary"),
        name="wkv_scan",
    )(rt, kt, at, bt, vb, g, bonus, wl, lnx_g.reshape(1, rw), lnx_b.reshape(1, rw))


CONV_ROWS = 64
CONV_LANES = 256


def _conv_kernel(val_ref, gate_ref, w_ref, b_ref, lg_ref, lb_ref, o_ref, win_ref, y_ref,
                 *, tm, taps, halo):
    keep = halo + SUBLANE

    @pl.when(pl.program_id(1) == 0)
    def _():
        win_ref[:, 0:keep, :] = jnp.zeros((SUBLANE, keep, win_ref.shape[2]), F32)

    @pl.when(pl.program_id(1) != 0)
    def _():
        win_ref[:, 0:keep, :] = win_ref[:, tm:tm + keep, :]

    u = val_ref[0].astype(F32) * jax.nn.sigmoid(gate_ref[0].astype(F32))
    for r in range(SUBLANE):
        win_ref[r, halo + r:halo + r + tm, :] = u
    for c0 in range(0, u.shape[1], CONV_LANES):
        lanes = slice(c0, min(c0 + CONV_LANES, u.shape[1]))
        for r0 in range(0, tm, CONV_ROWS):
            n = min(CONV_ROWS, tm - r0)
            acc = jnp.broadcast_to(b_ref[:, lanes], (n, lanes.stop - lanes.start))
            for j in range(taps):
                q, r = divmod(taps - 1 - j, SUBLANE)
                off = halo - SUBLANE * q + r0
                acc = acc + w_ref[j:j + 1, lanes] * win_ref[r, off:off + n, lanes]
            y_ref[r0:r0 + n, lanes] = acc
    acc = y_ref[...]
    mu = jnp.mean(acc, axis=-1, keepdims=True)
    d = acc - mu
    var = jnp.mean(d * d, axis=-1, keepdims=True)
    z = d * lax.rsqrt(var + LN_EPS) * lg_ref[...] + lb_ref[...]
    o_ref[0] = (z * jax.nn.sigmoid(z)).astype(o_ref.dtype)


def conv_group(h3, rw, cw, conv_w, conv_b, ln_g, ln_b):
    bsz, s, _ = h3.shape
    taps = conv_w.shape[0]
    halo = _round_up(taps - 1, SUBLANE)
    tm = _pick(s, 128)
    assert (3 * rw) % cw == 0 and tm >= halo + SUBLANE
    c0 = (3 * rw) // cw
    row = pl.BlockSpec((1, cw), lambda b, i: (0, 0))
    return pl.pallas_call(
        functools.partial(_conv_kernel, tm=tm, taps=taps, halo=halo),
        grid=(bsz, s // tm),
        in_specs=[pl.BlockSpec((1, tm, cw), lambda b, i: (b, i, c0)),
                  pl.BlockSpec((1, tm, cw), lambda b, i: (b, i, c0 + 1)),
                  pl.BlockSpec((taps, cw), lambda b, i: (0, 0)), row, row, row],
        out_specs=pl.BlockSpec((1, tm, cw), lambda b, i: (b, i, 0)),
        out_shape=jax.ShapeDtypeStruct((bsz, s, cw), BF16),
        scratch_shapes=[pltpu.VMEM((SUBLANE, halo + tm + SUBLANE, cw), F32),
                        pltpu.VMEM((tm, cw), F32)],
        compiler_params=_params("arbitrary", "arbitrary"),
        name="conv_group",
    )(h3, h3, conv_w, conv_b.reshape(1, cw), ln_g.reshape(1, cw), ln_b.reshape(1, cw))


def _pad_cols(w, n):
    return jnp.pad(w, ((0, 0), (0, n - w.shape[1])))


def _pad_rows(w, n):
    return jnp.pad(w, ((0, n - w.shape[0]), (0, 0)))


def _mixer_weights(lp, has_vres):
    names = [('w', 'mu_w', 'w1', 'w2'), ('a', 'mu_a', 'a1', 'a2'), ('g', 'mu_g', 'g1', 'g2')]
    if has_vres:
        names.append(('v', 'mu_v', 'v1', 'v2'))
    cur, prev, low, off = [], [], [], 0
    out = dict(lp)
    for tag, mu, w1, w2 in names:
        rank = lp[w1].shape[1]
        rpad = _round_up(rank, LANE)
        cur.append(_pad_cols((1.0 - lp[mu])[:, None] * lp[w1], rpad))
        prev.append(_pad_cols(lp[mu][:, None] * lp[w1], rpad))
        out[w2 + 'p'] = _pad_rows(lp[w2], rpad).astype(BF16)
        low.append((off, rpad))
        off += rpad
    if not has_vres:
        low.append((0, 0))
    in_cols = lp['w_in'].shape[1]
    low_start = _round_up(in_cols, 2 * off)
    out['w_big'] = jnp.concatenate([_pad_cols(lp['w_in'], low_start)] + cur + prev,
                                   axis=1).astype(BF16)
    return out, tuple(low), low_start // (2 * off)


def _mixer(xm, x, lp, mods, vfirst, nxt, router, side=()):
    bsz, s, d = x.shape
    rw = lp['w0'].shape[0]
    cw = lp['conv_b'].shape[0]
    has_vres = vfirst is not None
    lw, low, low_blk = _mixer_weights(lp, has_vres)
    n_cols = lw['w_big'].shape[1]
    h, side_bf16 = matmul([xm.reshape(bsz * s, d)], [lw['w_big']], BF16,
                          tn_pref=n_cols // (low_blk + 1), name="in_projection", side=side)
    h3 = h.reshape(bsz, s, n_cols)
    tm = _pick(s, 256)
    outs = rwkv_prep(h3, rw, low, low_blk, lw, vfirst, tm)
    rt, kt, at, bt, vb, g, bonus, wl = outs[:8]
    v_out = vfirst if has_vres else outs[8]
    y_rwkv = wkv_scan(rt, kt, at, bt, vb, g, bonus, wl, lp['lnx_g'], lp['lnx_b'], tm)
    u = conv_group(h3, rw, cw, lp['conv_w'], lp['conv_b'], lp['conv_ln_g'], lp['conv_ln_b'])
    w_out = lp['w_out'].astype(BF16)
    o, _ = matmul([y_rwkv.reshape(bsz * s, rw), u.reshape(bsz * s, cw)], [w_out[:rw], w_out[rw:]],
                  BF16, name="out_projection")
    res = sublayer_epilogue(o.reshape(bsz, s, d), x, lp['mix_post_g'], mods['gt_m'], nxt, router)
    return res, v_out, side_bf16


def _split_layer(p, prefix, has_vres):
    keys = ['ada_w', 'ada_b', 'mix_pre_g', 'mix_post_g', 'w_in', 'mu_rkv', 'mu_w', 'mu_a', 'mu_g',
            'w0', 'w1', 'w2', 'a0', 'a1', 'a2', 'g1', 'g2', 'k_k', 'k_a', 'r_k', 'lnx_g', 'lnx_b',
            'conv_w', 'conv_b', 'conv_ln_g', 'conv_ln_b', 'w_out', 'ffn_pre_g', 'ffn_post_g']
    if has_vres:
        keys += ['mu_v', 'v0', 'v1', 'v2']
    return {k: p[prefix + k] for k in keys}


def _forward(p):
    x = p['x']
    c = p['c']
    bsz, s, d = x.shape
    layers = [_split_layer(p, 'l0_', False), _split_layer(p, 'l1_', True)]
    mods = []
    for lp in layers:
        mod = ada_modulation(c, lp['ada_w'], lp['ada_b'])
        mods.append(dict(zip(['sh_m', 'sc_m', 'gt_m', 'sh_f', 'sc_f', 'gt_f'],
                             jnp.split(mod, 6, axis=-1))))

    lp, md = layers[0], mods[0]
    xm = norm_modulate(x, lp['mix_pre_g'], md['sh_m'], md['sc_m'])
    (x, xf), vfirst, ffn_w = _mixer(
        xm, x, lp, md, None, (lp['ffn_pre_g'], md['sh_f'], md['sc_f']), None,
        side=(p['l0_ffn_w_gate'], p['l0_ffn_w_up'], p['l0_ffn_w_down']))
    y, moe_w = dense_swiglu(xf.reshape(bsz * s, d), *ffn_w,
                            side=(p['l1_moe_w_gate'], p['l1_moe_w_up'], p['l1_moe_w_down']))
    nlp, nmd = layers[1], mods[1]
    x, xm = sublayer_epilogue(y.reshape(bsz, s, d), x, lp['ffn_post_g'], md['gt_f'],
                              (nlp['mix_pre_g'], nmd['sh_m'], nmd['sc_m']))

    lp, md = layers[1], mods[1]
    (x, xf, route), _, _ = _mixer(xm, x, lp, md, vfirst,
                                  (lp['ffn_pre_g'], md['sh_f'], md['sc_f']),
                                  (p['l1_router_w'], p['l1_router_b']))
    return moe_sublayer(xf, route, x, lp['ffn_post_g'], md['gt_f'], *moe_w)


_ARG_NAMES = (
    'x', 'c',
    'l0_ada_w', 'l0_ada_b', 'l0_mix_pre_g', 'l0_mix_post_g', 'l0_w_in', 'l0_mu_rkv', 'l0_mu_w',
    'l0_mu_a', 'l0_mu_g', 'l0_w0', 'l0_w1', 'l0_w2', 'l0_a0', 'l0_a1', 'l0_a2', 'l0_g1', 'l0_g2',
    'l0_k_k', 'l0_k_a', 'l0_r_k', 'l0_lnx_g', 'l0_lnx_b', 'l0_conv_w', 'l0_conv_b', 'l0_conv_ln_g',
    'l0_conv_ln_b', 'l0_w_out',
    'l0_ffn_pre_g', 'l0_ffn_post_g', 'l0_ffn_w_gate', 'l0_ffn_w_up', 'l0_ffn_w_down',
    'l1_ada_w', 'l1_ada_b', 'l1_mix_pre_g', 'l1_mix_post_g', 'l1_w_in', 'l1_mu_rkv', 'l1_mu_w',
    'l1_mu_a', 'l1_mu_g', 'l1_w0', 'l1_w1', 'l1_w2', 'l1_a0', 'l1_a1', 'l1_a2', 'l1_g1', 'l1_g2',
    'l1_k_k', 'l1_k_a', 'l1_r_k', 'l1_lnx_g', 'l1_lnx_b', 'l1_conv_w', 'l1_conv_b', 'l1_conv_ln_g',
    'l1_conv_ln_b', 'l1_w_out', 'l1_mu_v', 'l1_v0', 'l1_v1', 'l1_v2',
    'l1_ffn_pre_g', 'l1_ffn_post_g', 'l1_router_w', 'l1_router_b', 'l1_moe_w_gate',
    'l1_moe_w_up', 'l1_moe_w_down')


def kernel(x, c, l0_ada_w, l0_ada_b, l0_mix_pre_g, l0_mix_post_g, l0_w_in, l0_mu_rkv, l0_mu_w, l0_mu_a, l0_mu_g, l0_w0, l0_w1, l0_w2, l0_a0, l0_a1, l0_a2, l0_g1, l0_g2, l0_k_k, l0_k_a, l0_r_k, l0_lnx_g, l0_lnx_b, l0_conv_w, l0_conv_b, l0_conv_ln_g, l0_conv_ln_b, l0_w_out, l0_ffn_pre_g, l0_ffn_post_g, l0_ffn_w_gate, l0_ffn_w_up, l0_ffn_w_down, l1_ada_w, l1_ada_b, l1_mix_pre_g, l1_mix_post_g, l1_w_in, l1_mu_rkv, l1_mu_w, l1_mu_a, l1_mu_g, l1_w0, l1_w1, l1_w2, l1_a0, l1_a1, l1_a2, l1_g1, l1_g2, l1_k_k, l1_k_a, l1_r_k, l1_lnx_g, l1_lnx_b, l1_conv_w, l1_conv_b, l1_conv_ln_g, l1_conv_ln_b, l1_w_out, l1_mu_v, l1_v0, l1_v1, l1_v2, l1_ffn_pre_g, l1_ffn_post_g, l1_router_w, l1_router_b, l1_moe_w_gate, l1_moe_w_up, l1_moe_w_down):
    args = (x, c, l0_ada_w, l0_ada_b, l0_mix_pre_g, l0_mix_post_g, l0_w_in, l0_mu_rkv, l0_mu_w, l0_mu_a, l0_mu_g, l0_w0, l0_w1, l0_w2, l0_a0, l0_a1, l0_a2, l0_g1, l0_g2, l0_k_k, l0_k_a, l0_r_k, l0_lnx_g, l0_lnx_b, l0_conv_w, l0_conv_b, l0_conv_ln_g, l0_conv_ln_b, l0_w_out, l0_ffn_pre_g, l0_ffn_post_g, l0_ffn_w_gate, l0_ffn_w_up, l0_ffn_w_down, l1_ada_w, l1_ada_b, l1_mix_pre_g, l1_mix_post_g, l1_w_in, l1_mu_rkv, l1_mu_w, l1_mu_a, l1_mu_g, l1_w0, l1_w1, l1_w2, l1_a0, l1_a1, l1_a2, l1_g1, l1_g2, l1_k_k, l1_k_a, l1_r_k, l1_lnx_g, l1_lnx_b, l1_conv_w, l1_conv_b, l1_conv_ln_g, l1_conv_ln_b, l1_w_out, l1_mu_v, l1_v0, l1_v1, l1_v2, l1_ffn_pre_g, l1_ffn_post_g, l1_router_w, l1_router_b, l1_moe_w_gate, l1_moe_w_up, l1_moe_w_down)
    return _forward(dict(zip(_ARG_NAMES, args)))
```

```python
import functools

import jax
import jax.numpy as jnp
from jax import lax
from jax.experimental import pallas as pl
from jax.experimental.pallas import tpu as pltpu

F32 = jnp.float32
BF16 = jnp.bfloat16

HEAD = 64
HEADS_PER_GROUP = 4
GROUP_LANES = HEAD * HEADS_PER_GROUP
CHUNK = 64
SCAN_GROUPS = 4
LANE = 128
SUBLANE = 8
TOP_K = 2
RMS_EPS = 1e-6
GN_EPS = 64e-5
LN_EPS = 1e-5
VMEM_LIMIT_BYTES = 60 * 1024 * 1024


def _params(*sem):
    return pltpu.CompilerParams(dimension_semantics=sem, vmem_limit_bytes=VMEM_LIMIT_BYTES)


def _round_up(n, m):
    return (n + m - 1) // m * m


def _pick(n, pref):
    if n <= pref:
        return n
    t = pref
    while t >= LANE:
        if n % t == 0:
            return t
        t -= LANE
    return n


def _dot(a, b):
    return jnp.dot(a, b, preferred_element_type=F32)


def _rms(x, g):
    return x * lax.rsqrt(jnp.mean(x * x, axis=-1, keepdims=True) + RMS_EPS) * g


def _ada_kernel(c_ref, w_ref, b_ref, o_ref):
    w = w_ref[...]
    out_rows = []
    for b in range(c_ref.shape[0]):
        cb = c_ref[b]
        s = cb * jax.nn.sigmoid(cb)
        strips = [jnp.sum(w[:, c0:c0 + LANE] * s, axis=0, keepdims=True)
                  for c0 in range(0, w.shape[1], LANE)]
        out_rows.append(jnp.concatenate(strips, axis=1) + b_ref[...])
    pad = o_ref.shape[0] - len(out_rows)
    if pad:
        out_rows.append(jnp.zeros((pad, w.shape[1]), F32))
    o_ref[...] = jnp.concatenate(out_rows, axis=0)


def ada_modulation(c, ada_w, ada_b):
    bsz, d = c.shape
    n = ada_w.shape[1]
    rows = _round_up(bsz, SUBLANE)
    c_lanes = jnp.broadcast_to(c[:, :, None], (bsz, d, LANE))
    tn = _pick(n, 1024)
    out = pl.pallas_call(
        _ada_kernel,
        grid=(n // tn,),
        in_specs=[pl.BlockSpec((bsz, d, LANE), lambda j: (0, 0, 0)),
                  pl.BlockSpec((d, tn), lambda j: (0, j)),
                  pl.BlockSpec((1, tn), lambda j: (0, j))],
        out_specs=pl.BlockSpec((rows, tn), lambda j: (0, j)),
        out_shape=jax.ShapeDtypeStruct((rows, n), F32),
        compiler_params=_params("arbitrary"),
        name="ada_matvec",
    )(c_lanes, ada_w, ada_b.reshape(1, n))
    return out[:bsz]


def _norm_mod_kernel(x_ref, g_ref, sh_ref, sc_ref, o_ref):
    xn = _rms(x_ref[0], g_ref[...])
    o_ref[0] = (xn * (1.0 + sc_ref[0]) + sh_ref[0]).astype(o_ref.dtype)


def norm_modulate(x, g, shift, scale):
    bsz, s, d = x.shape
    tm = _pick(s, 512)
    vec = pl.BlockSpec((1, 1, d), lambda b, i: (b, 0, 0))
    return pl.pallas_call(
        _norm_mod_kernel,
        grid=(bsz, s // tm),
        in_specs=[pl.BlockSpec((1, tm, d), lambda b, i: (b, i, 0)),
                  pl.BlockSpec((1, d), lambda b, i: (0, 0)), vec, vec],
        out_specs=pl.BlockSpec((1, tm, d), lambda b, i: (b, i, 0)),
        out_shape=jax.ShapeDtypeStruct((bsz, s, d), BF16),
        compiler_params=_params("arbitrary", "arbitrary"),
        name="norm_modulate",
    )(x, g.reshape(1, d), shift.reshape(bsz, 1, d), scale.reshape(bsz, 1, d))


def _side_cast_plan(w, n_steps, step_of):
    *lead, r, c = w.shape
    stack = lead[0] if lead else 1
    rb = next((cand for cand in range(2 * SUBLANE, r, 2 * SUBLANE)
               if r % cand == 0 and stack * (r // cand) <= n_steps), r)
    per = r // rb
    n_blocks = stack * per
    assert n_blocks <= n_steps, (w.shape, n_steps)

    def index(*grid):
        s = jnp.minimum(step_of(*grid), n_blocks - 1)
        return (s // per, s % per, 0) if lead else (s, 0)

    return pl.BlockSpec((None, rb, c) if lead else (rb, c), index), n_blocks


def _run_side_casts(step, side_in, side_out, side_blocks):
    for src, dst, n_blocks in zip(side_in, side_out, side_blocks):
        @pl.when(step < n_blocks)
        def _(src=src, dst=dst):
            dst[...] = src[...].astype(dst.dtype)


def _mm_kernel(*refs, n_seg, side_blocks):
    n_side = len(side_blocks)
    side_in = refs[2 * n_seg:2 * n_seg + n_side]
    o_ref = refs[2 * n_seg + n_side]
    side_out = refs[2 * n_seg + n_side + 1:]
    acc = _dot(refs[0][...], refs[n_seg][...])
    for s in range(1, n_seg):
        acc = acc + _dot(refs[s][...], refs[n_seg + s][...])
    o_ref[...] = acc.astype(o_ref.dtype)
    _run_side_casts(pl.program_id(0) * pl.num_programs(1) + pl.program_id(1),
                    side_in, side_out, side_blocks)


def matmul(a_list, w_list, out_dtype, tm_pref=1024, tn_pref=1024, name="matmul", side=()):
    m = a_list[0].shape[0]
    n = w_list[0].shape[1]
    tm = _pick(m, tm_pref)
    tn = _pick(n, tn_pref)
    n_seg = len(a_list)
    n_j = n // tn
    once = dict(pipeline_mode=pl.Buffered(1)) if side else {}
    in_specs = [pl.BlockSpec((tm, a.shape[1]), lambda i, j: (i, 0), **once) for a in a_list]
    in_specs += [pl.BlockSpec((w.shape[0], tn), lambda i, j: (0, j)) for w in w_list]
    plans = [_side_cast_plan(w, (m // tm) * n_j, lambda i, j: i * n_j + j) for w in side]
    outs = pl.pallas_call(
        functools.partial(_mm_kernel, n_seg=n_seg, side_blocks=tuple(nb for _, nb in plans)),
        grid=(m // tm, n_j),
        in_specs=in_specs + [spec for spec, _ in plans],
        out_specs=[pl.BlockSpec((tm, tn), lambda i, j: (i, j))] + [spec for spec, _ in plans],
        out_shape=[jax.ShapeDtypeStruct((m, n), out_dtype)]
        + [jax.ShapeDtypeStruct(w.shape, BF16) for w in side],
        compiler_params=_params("arbitrary", "arbitrary"),
        name=name,
    )(*a_list, *w_list, *side)
    return outs[0], tuple(outs[1:])


ROUTE_E1, ROUTE_E2, ROUTE_W1, ROUTE_W2 = 0, 1, 2, 3


def _top2_route(logits, n_experts):
    lane = lax.broadcasted_iota(jnp.int32, logits.shape, 1)
    neg = jnp.float32(-jnp.inf)
    lg = jnp.where(lane < n_experts, logits, neg)
    v1 = jnp.max(lg, axis=-1, keepdims=True)
    i1 = jnp.min(jnp.where(lg == v1, lane, LANE), axis=-1, keepdims=True)
    lg2 = jnp.where(lane == i1, neg, lg)
    v2 = jnp.max(lg2, axis=-1, keepdims=True)
    i2 = jnp.min(jnp.where(lg2 == v2, lane, LANE), axis=-1, keepdims=True)
    e2 = jnp.exp(v2 - v1)
    w1 = 1.0 / (1.0 + e2)
    w2 = e2 / (1.0 + e2)
    rec = jnp.where(lane == ROUTE_E1, i1.astype(F32), 0.0)
    rec = jnp.where(lane == ROUTE_E2, i2.astype(F32), rec)
    rec = jnp.where(lane == ROUTE_W1, w1, rec)
    return jnp.where(lane == ROUTE_W2, w2, rec)


def _pack_halves(x):
    half = x.shape[1] // 2
    lo = lax.bitcast_convert_type(x[:, :half].astype(BF16).astype(F32), jnp.uint32)
    hi = lax.bitcast_convert_type(x[:, half:].astype(BF16).astype(F32), jnp.uint32)
    return hi | (lo >> 16)


def _unpack_halves(w):
    lo = lax.bitcast_convert_type(w << 16, F32)
    hi = lax.bitcast_convert_type(w & jnp.uint32(0xFFFF0000), F32)
    return lo, hi


def _epilogue_kernel(*refs, has_next, n_experts):
    y_ref, x_ref, post_g_ref, gate_ref = refs[:4]
    pos = 4
    if has_next:
        pre_g_ref, sh_ref, sc_ref = refs[pos:pos + 3]
        pos += 3
    if n_experts:
        rw_ref, rb_ref = refs[pos:pos + 2]
        pos += 2
    outs = refs[pos:]
    x1 = x_ref[0] + gate_ref[0] * _rms(y_ref[0].astype(F32), post_g_ref[...])
    outs[0][0] = x1
    if has_next:
        xf = _rms(x1, pre_g_ref[...]) * (1.0 + sc_ref[0]) + sh_ref[0]
        outs[1][0] = _pack_halves(xf) if n_experts else xf.astype(outs[1].dtype)
        if n_experts:
            logits = jnp.dot(xf, rw_ref[...], preferred_element_type=F32,
                             precision=lax.Precision.HIGHEST) + rb_ref[...]
            outs[2][0] = _top2_route(logits, n_experts)


def sublayer_epilogue(y, x, post_g, gate, nxt=None, router=None):
    bsz, s, d = x.shape
    tm = _pick(s, 256)
    blk = pl.BlockSpec((1, tm, d), lambda b, i: (b, i, 0))
    row = pl.BlockSpec((1, d), lambda b, i: (0, 0))
    vec = pl.BlockSpec((1, 1, d), lambda b, i: (b, 0, 0))
    args = [y, x, post_g.reshape(1, d), gate.reshape(bsz, 1, d)]
    in_specs = [blk, blk, row, vec]
    out_shape = [jax.ShapeDtypeStruct((bsz, s, d), F32)]
    out_specs = [blk]
    n_experts = 0
    if nxt is not None:
        pre_g, shift, scale = nxt
        args += [pre_g.reshape(1, d), shift.reshape(bsz, 1, d), scale.reshape(bsz, 1, d)]
        in_specs += [row, vec, vec]
        if router is None:
            out_shape.append(jax.ShapeDtypeStruct((bsz, s, d), BF16))
            out_specs.append(blk)
        else:
            out_shape.append(jax.ShapeDtypeStruct((bsz, s, d // 2), jnp.uint32))
            out_specs.append(pl.BlockSpec((1, tm, d // 2), lambda b, i: (b, i, 0)))
        if router is not None:
            router_w, router_b = router
            n_experts = router_w.shape[1]
            rw = jnp.zeros((d, LANE), F32).at[:, :n_experts].set(router_w)
            rb = jnp.zeros((1, LANE), F32).at[0, :n_experts].set(router_b)
            args += [rw, rb]
            in_specs += [pl.BlockSpec((d, LANE), lambda b, i: (0, 0)),
                         pl.BlockSpec((1, LANE), lambda b, i: (0, 0))]
            out_shape.append(jax.ShapeDtypeStruct((bsz, s, LANE), F32))
            out_specs.append(pl.BlockSpec((1, tm, LANE), lambda b, i: (b, i, 0)))
    return pl.pallas_call(
        functools.partial(_epilogue_kernel, has_next=nxt is not None, n_experts=n_experts),
        grid=(bsz, s // tm),
        in_specs=in_specs,
        out_specs=out_specs,
        out_shape=out_shape,
        compiler_params=_params("arbitrary", "arbitrary"),
        name="sublayer_epilogue",
    )(*args)


def _ffn_kernel(*refs, side_blocks):
    n_side = len(side_blocks)
    x_ref, wg_ref, wu_ref, wd_ref = refs[:4]
    side_in = refs[4:4 + n_side]
    o_ref = refs[4 + n_side]
    side_out = refs[5 + n_side:5 + 2 * n_side]
    acc_ref = refs[5 + 2 * n_side]
    j = pl.program_id(1)

    @pl.when(j == 0)
    def _():
        acc_ref[...] = jnp.zeros_like(acc_ref)

    x = x_ref[...]
    hg = _dot(x, wg_ref[...])
    hu = _dot(x, wu_ref[...])
    h = (hg * jax.nn.sigmoid(hg) * hu).astype(BF16)
    acc_ref[...] += _dot(h, wd_ref[...])

    @pl.when(j == pl.num_programs(1) - 1)
    def _():
        o_ref[...] = acc_ref[...].astype(o_ref.dtype)

    _run_side_casts(pl.program_id(0) * pl.num_programs(1) + j, side_in, side_out, side_blocks)


def dense_swiglu(x, w_gate, w_up, w_down, side=()):
    m, d = x.shape
    f = w_gate.shape[1]
    tm = _pick(m, 512)
    tf = _pick(f, 512)
    n_j = f // tf
    plans = [_side_cast_plan(w, (m // tm) * n_j, lambda i, j: i * n_j + j) for w in side]
    once = dict(pipeline_mode=pl.Buffered(1))
    outs = pl.pallas_call(
        functools.partial(_ffn_kernel, side_blocks=tuple(nb for _, nb in plans)),
        grid=(m // tm, n_j),
        in_specs=[pl.BlockSpec((tm, d), lambda i, j: (i, 0), **once),
                  pl.BlockSpec((d, tf), lambda i, j: (0, j)),
                  pl.BlockSpec((d, tf), lambda i, j: (0, j)),
                  pl.BlockSpec((tf, d), lambda i, j: (j, 0))] + [spec for spec, _ in plans],
        out_specs=[pl.BlockSpec((tm, d), lambda i, j: (i, 0), **once)] + [spec for spec, _ in plans],
        out_shape=[jax.ShapeDtypeStruct((m, d), BF16)]
        + [jax.ShapeDtypeStruct(w.shape, BF16) for w in side],
        scratch_shapes=[pltpu.VMEM((tm, d), F32)],
        compiler_params=_params("arbitrary", "arbitrary"),
        name="dense_swiglu",
    )(x, w_gate, w_up, w_down, *side)
    return outs[0], tuple(outs[1:])


MOE_ROW_TILE = 512
DMA_ROWS = 256


def _rank_kernel(route_ref, tri_ref, rank_ref, count_ref, carry_ref):
    @pl.when(pl.program_id(0) == 0)
    def _():
        carry_ref[...] = jnp.zeros_like(carry_ref)

    route = route_ref[...]
    lane = lax.broadcasted_iota(jnp.int32, route.shape, 1)
    lane_f = lane.astype(F32)
    oh1 = lane_f == route[:, ROUTE_E1:ROUTE_E1 + 1]
    oh2 = lane_f == route[:, ROUTE_E2:ROUTE_E2 + 1]
    cnt = jnp.where(oh1 | oh2, 1.0, 0.0)
    before = _dot(tri_ref[...], cnt.astype(BF16)) + carry_ref[0:1, :]
    r1 = jnp.sum(jnp.where(oh1, before, 0.0), axis=-1, keepdims=True)
    r2 = jnp.sum(jnp.where(oh2, before, 0.0), axis=-1, keepdims=True)
    rank_ref[...] = jnp.where(lane == ROUTE_E1, r1, jnp.where(lane == ROUTE_E2, r2, 0.0))
    total = carry_ref[0:1, :] + jnp.sum(cnt, axis=0, keepdims=True)
    carry_ref[0:1, :] = total
    count_ref[...] = jnp.broadcast_to(total, count_ref.shape)


def expert_ranks(route):
    t = route.shape[0]
    tm = _pick(t, 256)
    ti = lax.broadcasted_iota(jnp.int32, (tm, tm), 0)
    tj = lax.broadcasted_iota(jnp.int32, (tm, tm), 1)
    tri = (ti > tj).astype(BF16)
    return pl.pallas_call(
        _rank_kernel,
        grid=(t // tm,),
        in_specs=[pl.BlockSpec((tm, LANE), lambda i: (i, 0)),
                  pl.BlockSpec((tm, tm), lambda i: (0, 0))],
        out_specs=[pl.BlockSpec((tm, LANE), lambda i: (i, 0)),
                   pl.BlockSpec((SUBLANE, LANE), lambda i: (0, 0))],
        out_shape=[jax.ShapeDtypeStruct((t, LANE), F32),
                   jax.ShapeDtypeStruct((SUBLANE, LANE), F32)],
        scratch_shapes=[pltpu.VMEM((SUBLANE, LANE), F32)],
        compiler_params=_params("arbitrary"),
        name="expert_ranks",
    )(route, tri)


def _row_copies(n_rows, make_copy):
    def start(r, carry):
        for cp in make_copy(r):
            cp.start()
        return carry

    def wait(r, carry):
        for cp in make_copy(r):
            cp.wait()
        return carry

    lax.fori_loop(0, n_rows, start, 0)
    lax.fori_loop(0, n_rows, wait, 0)


def _dispatch_kernel(pos1_ref, pos2_ref, x_ref, init_ref, o_ref, sem):
    del init_ref

    def make_copy(r):
        src = x_ref.at[pl.ds(r, 1)]
        return [pltpu.make_async_copy(src, o_ref.at[pl.ds(pos_ref[0, r], 1)], sem)
                for pos_ref in (pos1_ref, pos2_ref)]

    _row_copies(x_ref.shape[0], make_copy)


def moe_dispatch(x, pos1, pos2, n_rows):
    t, d = x.shape
    rows = pos1.shape[2]
    smem = pl.BlockSpec((None, 1, rows), lambda i: (i, 0, 0), memory_space=pltpu.SMEM)
    return pl.pallas_call(
        _dispatch_kernel,
        grid=(t // rows,),
        in_specs=[smem, smem, pl.BlockSpec((rows, d), lambda i: (i, 0)),
                  pl.BlockSpec(memory_space=pl.ANY)],
        out_specs=pl.BlockSpec(memory_space=pl.ANY),
        out_shape=jax.ShapeDtypeStruct((n_rows, d), x.dtype),
        scratch_shapes=[pltpu.SemaphoreType.DMA(())],
        input_output_aliases={3: 0},
        compiler_params=_params("arbitrary"),
        name="moe_dispatch",
    )(pos1, pos2, x, jnp.zeros((n_rows, d), x.dtype))


def _moe_kernel(tile_expert_ref, n_used_ref, x_ref, wg_ref, wu_ref, wd_ref, o_ref,
                xlo_ref, xhi_ref, acc_ref):
    del tile_expert_ref
    k = pl.program_id(1)
    half = xlo_ref.shape[1]

    @pl.when(k == 0)
    def _():
        acc_ref[...] = jnp.zeros_like(acc_ref)
        lo, hi = _unpack_halves(x_ref[...])
        xlo_ref[...] = lo.astype(BF16)
        xhi_ref[...] = hi.astype(BF16)

    @pl.when(pl.program_id(0) < n_used_ref[0])
    def _():
        xlo = xlo_ref[...]
        xhi = xhi_ref[...]
        hg = _dot(xlo, wg_ref[:half, :]) + _dot(xhi, wg_ref[half:, :])
        hu = _dot(xlo, wu_ref[:half, :]) + _dot(xhi, wu_ref[half:, :])
        h = (hg * jax.nn.sigmoid(hg) * hu).astype(BF16)
        acc_ref[...] += _dot(h, wd_ref[...])

    @pl.when(k == pl.num_programs(1) - 1)
    def _():
        o_ref[...] = _pack_halves(acc_ref[...])


def moe_grouped_swiglu(xs, tile_expert, n_used, w_gate, w_up, w_down):
    n_rows, half = xs.shape
    d = 2 * half
    f = w_gate.shape[2]
    tf = _pick(f, 512)
    live = lambda j, k, te, nu: jnp.where(j < nu[0], k, 0)
    grid_spec = pltpu.PrefetchScalarGridSpec(
        num_scalar_prefetch=2,
        grid=(n_rows // MOE_ROW_TILE, f // tf),
        in_specs=[pl.BlockSpec((MOE_ROW_TILE, half), lambda j, k, te, nu: (j, 0)),
                  pl.BlockSpec((None, d, tf), lambda j, k, te, nu: (te[j], 0, live(j, k, te, nu))),
                  pl.BlockSpec((None, d, tf), lambda j, k, te, nu: (te[j], 0, live(j, k, te, nu))),
                  pl.BlockSpec((None, tf, d), lambda j, k, te, nu: (te[j], live(j, k, te, nu), 0))],
        out_specs=pl.BlockSpec((MOE_ROW_TILE, half), lambda j, k, te, nu: (j, 0)),
        scratch_shapes=[pltpu.VMEM((MOE_ROW_TILE, half), BF16),
                        pltpu.VMEM((MOE_ROW_TILE, half), BF16),
                        pltpu.VMEM((MOE_ROW_TILE, d), F32)])
    return pl.pallas_call(
        _moe_kernel,
        grid_spec=grid_spec,
        out_shape=jax.ShapeDtypeStruct((n_rows, half), jnp.uint32),
        compiler_params=_params("arbitrary", "arbitrary"),
        name="moe_grouped_swiglu",
    )(tile_expert, n_used, xs, w_gate, w_up, w_down)


def _combine_kernel(pos1_ref, pos2_ref, route_ref, x_ref, post_g_ref, gate_ref, ys_ref, o_ref,
                    buf1, buf2, sem):
    def make_copy(r):
        return [pltpu.make_async_copy(ys_ref.at[pl.ds(pos_ref[0, r], 1)], buf.at[pl.ds(r, 1)], sem)
                for pos_ref, buf in ((pos1_ref, buf1), (pos2_ref, buf2))]

    _row_copies(buf1.shape[0], make_copy)
    route = route_ref[0]
    w1 = route[:, ROUTE_W1:ROUTE_W1 + 1]
    w2 = route[:, ROUTE_W2:ROUTE_W2 + 1]
    lo1, hi1 = _unpack_halves(buf1[...])
    lo2, hi2 = _unpack_halves(buf2[...])
    y = jnp.concatenate([w1 * lo1 + w2 * lo2, w1 * hi1 + w2 * hi2], axis=1)
    o_ref[0] = x_ref[0] + gate_ref[0] * _rms(y, post_g_ref[...])


def moe_combine_epilogue(ys, pos1, pos2, route, x, post_g, gate):
    bsz, s, d = x.shape
    rows = pos1.shape[2]
    n_i = s // rows
    smem = pl.BlockSpec((None, 1, rows), lambda b, i: (b * n_i + i, 0, 0),
                        memory_space=pltpu.SMEM)
    blk = pl.BlockSpec((1, rows, d), lambda b, i: (b, i, 0))
    return pl.pallas_call(
        _combine_kernel,
        grid=(bsz, n_i),
        in_specs=[smem, smem, pl.BlockSpec((1, rows, LANE), lambda b, i: (b, i, 0)), blk,
                  pl.BlockSpec((1, d), lambda b, i: (0, 0)),
                  pl.BlockSpec((1, 1, d), lambda b, i: (b, 0, 0)),
                  pl.BlockSpec(memory_space=pl.ANY)],
        out_specs=blk,
        out_shape=jax.ShapeDtypeStruct((bsz, s, d), F32),
        scratch_shapes=[pltpu.VMEM((rows, d // 2), jnp.uint32),
                        pltpu.VMEM((rows, d // 2), jnp.uint32),
                        pltpu.SemaphoreType.DMA(())],
        compiler_params=_params("arbitrary", "arbitrary"),
        name="moe_combine_epilogue",
    )(pos1, pos2, route, x, post_g.reshape(1, d), gate.reshape(bsz, 1, d), ys)


def moe_sublayer(xf, route, x, post_g, gate, w_gate, w_up, w_down):
    bsz, s, d = x.shape
    t = bsz * s
    n_e = w_gate.shape[0]
    rows = _pick(s, DMA_ROWS)
    assert (TOP_K * t) % MOE_ROW_TILE == 0
    n_tiles = TOP_K * t // MOE_ROW_TILE + n_e
    route2 = route.reshape(t, LANE)
    rank, count = expert_ranks(route2)
    counts = count[0, :n_e].astype(jnp.int32)
    tiles_e = (counts + MOE_ROW_TILE - 1) // MOE_ROW_TILE
    tile_end = jnp.cumsum(tiles_e)
    row_start = (tile_end - tiles_e) * MOE_ROW_TILE
    e1 = route2[:, ROUTE_E1].astype(jnp.int32)
    e2 = route2[:, ROUTE_E2].astype(jnp.int32)
    pos1 = (row_start[e1] + rank[:, ROUTE_E1].astype(jnp.int32)).reshape(t // rows, 1, rows)
    pos2 = (row_start[e2] + rank[:, ROUTE_E2].astype(jnp.int32)).reshape(t // rows, 1, rows)
    tile_expert = jnp.minimum(jnp.searchsorted(tile_end, jnp.arange(n_tiles), side='right'),
                              n_e - 1).astype(jnp.int32)
    n_used = tile_end[n_e - 1:].astype(jnp.int32)
    xs = moe_dispatch(xf.reshape(t, d // 2), pos1, pos2, n_tiles * MOE_ROW_TILE)
    ys = moe_grouped_swiglu(xs, tile_expert, n_used, w_gate, w_up, w_down)
    return moe_combine_epilogue(ys, pos1, pos2, route, x, post_g, gate)


def _shift_rows(x, carry_row):
    rolled = pltpu.roll(x, 1, axis=0)
    row = lax.broadcasted_iota(jnp.int32, x.shape, 0)
    return jnp.where(row == 0, carry_row, rolled)


def _head_sums(x, seg):
    parts = [_dot(x[:, c:c + GROUP_LANES].astype(BF16), seg)
             for c in range(0, x.shape[1], GROUP_LANES)]
    return parts[0] if len(parts) == 1 else jnp.concatenate(parts, axis=1)


def _prep_kernel(*refs, has_vres, tm, low):
    (r_ref, k_ref, v_ref, p_ref) = refs[:4]
    pos = 4
    if has_vres:
        vfirst_ref = refs[pos]
        pos += 1
    (mu_ref, w0_ref, w2_ref, a0_ref, a2_ref, g2_ref) = refs[pos:pos + 6]
    pos += 6
    if has_vres:
        v0_ref, v2_ref = refs[pos:pos + 2]
        pos += 2
    (kk_ref, ka_ref, rk_ref, seg_ref, tri_ref) = refs[pos:pos + 5]
    pos += 5
    (rt_ref, kt_ref, at_ref, bt_ref, vb_ref, g_ref, bonus_ref, wl_ref) = refs[pos:pos + 8]
    pos += 8
    if not has_vres:
        vf_ref = refs[pos]
        pos += 1
    carry_rkv, carry_p = refs[pos:pos + 2]
    rw = r_ref.shape[-1]
    n_low = p_ref.shape[-1] // 2

    @pl.when(pl.program_id(1) == 0)
    def _():
        carry_rkv[...] = jnp.zeros_like(carry_rkv)
        carry_p[...] = jnp.zeros_like(carry_p)

    def lerp_prev(ref, idx):
        cur = ref[0].astype(F32)
        prev = _shift_rows(cur, carry_rkv[0:1, idx * rw:(idx + 1) * rw])
        carry_rkv[0:1, idx * rw:(idx + 1) * rw] = cur[tm - 1:tm, :]
        return cur + (prev - cur) * mu_ref[idx:idx + 1, :]

    r = lerp_prev(r_ref, 0)
    k = lerp_prev(k_ref, 1)
    v = lerp_prev(v_ref, 2)

    p = p_ref[0].astype(F32)
    p_b = p[:, n_low:]
    lowr = p[:, :n_low] + _shift_rows(p_b, carry_p[0:1, :])
    carry_p[0:1, :] = p_b[tm - 1:tm, :]
    o_w, o_a, o_g, o_v = low

    def low_slice(o):
        return lowr[:, o[0]:o[0] + o[1]]

    zw = w0_ref[...] + _dot(jnp.tanh(low_slice(o_w)).astype(BF16), w2_ref[...])
    logw = -jnp.exp(jnp.float32(-0.5)) * jax.nn.sigmoid(zw)
    asig = jax.nn.sigmoid(a0_ref[...] + _dot(low_slice(o_a).astype(BF16), a2_ref[...]))
    g_ref[0] = _dot(jax.nn.sigmoid(low_slice(o_g)).astype(BF16), g2_ref[...]).astype(g_ref.dtype)
    if has_vres:
        vgate = jax.nn.sigmoid(v0_ref[...] + _dot(low_slice(o_v).astype(BF16), v2_ref[...]))
        v = v + (vfirst_ref[0] - v) * vgate
    else:
        vf_ref[0] = v

    seg = seg_ref[...]
    kk = k * kk_ref[...]
    kk = kk / jnp.maximum(jnp.sqrt(_head_sums(kk * kk, seg)), 1e-12)
    k = k * (1.0 + (asig - 1.0) * ka_ref[...])
    bonus_ref[0] = (_head_sums(r * k * rk_ref[...], seg) * v).astype(bonus_ref.dtype)

    cum = jnp.dot(tri_ref[...], logw, preferred_element_type=F32, precision=lax.Precision.HIGHEST)
    e_pos = jnp.exp(cum)
    e_neg = jnp.exp(-cum)
    rt_ref[0] = (r * e_pos).astype(BF16)
    kt_ref[0] = (k * e_neg).astype(BF16)
    bt_ref[0] = (kk * asig * e_neg).astype(BF16)
    at_ref[0] = (-kk * jnp.exp(cum - logw)).astype(BF16)
    vb_ref[0] = v.astype(BF16)
    ends = [cum[c * CHUNK + CHUNK - 1:c * CHUNK + CHUNK, :] for c in range(tm // CHUNK)]
    if tm // CHUNK < SUBLANE:
        ends.append(jnp.zeros((SUBLANE - tm // CHUNK, rw), F32))
    wl_ref[0, 0] = jnp.exp(jnp.concatenate(ends, axis=0))


def rwkv_prep(h3, rw, low, low_blk, lp, vfirst, tm):
    bsz, s, cols = h3.shape
    n_low2 = cols // (low_blk + 1)
    assert n_low2 * (low_blk + 1) == cols and tm % CHUNK == 0 and tm // CHUNK <= SUBLANE
    has_vres = vfirst is not None
    blk = lambda j: pl.BlockSpec((1, tm, rw), lambda b, i, j=j: (b, i, j))
    full = lambda a: pl.BlockSpec(a.shape, lambda b, i: (0,) * a.ndim)
    row = lambda a: a.reshape(1, -1)
    args = [h3, h3, h3, h3]
    in_specs = [blk(0), blk(1), blk(2),
                pl.BlockSpec((1, tm, n_low2), lambda b, i: (b, i, low_blk))]
    if has_vres:
        args.append(vfirst)
        in_specs.append(blk(0))
    seg = (lax.broadcasted_iota(jnp.int32, (GROUP_LANES, GROUP_LANES), 0) // HEAD ==
           lax.broadcasted_iota(jnp.int32, (GROUP_LANES, GROUP_LANES), 1) // HEAD).astype(BF16)
    ti = lax.broadcasted_iota(jnp.int32, (tm, tm), 0)
    tj = lax.broadcasted_iota(jnp.int32, (tm, tm), 1)
    tri = ((ti >= tj) & (ti // CHUNK == tj // CHUNK)).astype(F32)
    small = [lp['mu_rkv'].reshape(3, rw), row(lp['w0']), lp['w2p'], row(lp['a0']), lp['a2p'],
             lp['g2p']]
    if has_vres:
        small += [row(lp['v0']), lp['v2p']]
    small += [row(lp['k_k']), row(lp['k_a']), row(lp['r_k']), seg, tri]
    args += small
    in_specs += [full(a) for a in small]
    tok = lambda dt: jax.ShapeDtypeStruct((bsz, s, rw), dt)
    out_shape = [tok(BF16)] * 7 + [jax.ShapeDtypeStruct((bsz, s // tm, SUBLANE, rw), F32)]
    out_specs = [blk(0)] * 7 + [pl.BlockSpec((1, 1, SUBLANE, rw), lambda b, i: (b, i, 0, 0))]
    if not has_vres:
        out_shape.append(tok(F32))
        out_specs.append(blk(0))
    return pl.pallas_call(
        functools.partial(_prep_kernel, has_vres=has_vres, tm=tm, low=low),
        grid=(bsz, s // tm),
        in_specs=in_specs,
        out_specs=out_specs,
        out_shape=out_shape,
        scratch_shapes=[pltpu.VMEM((SUBLANE, 3 * rw), F32), pltpu.VMEM((SUBLANE, n_low2 // 2), F32)],
        compiler_params=_params("arbitrary", "arbitrary"),
        name="rwkv_prep",
    )(*args)


def _expand_heads(x, head_mask):
    return jnp.where(head_mask, jnp.concatenate([x] * HEADS_PER_GROUP, axis=0),
                     jnp.zeros((), x.dtype))


def _scan_kernel(*refs, n_chunks, n_groups, side_blocks):
    n_side = len(side_blocks)
    (rt_ref, kt_ref, at_ref, bt_ref, v_ref, g_ref, bonus_ref, wl_ref, lng_ref, lnb_ref) = refs[:10]
    side_in = refs[10:10 + n_side]
    o_ref = refs[10 + n_side]
    side_out = refs[11 + n_side:11 + 2 * n_side]
    state_ref = refs[11 + 2 * n_side]
    _scan_body(rt_ref, kt_ref, at_ref, bt_ref, v_ref, g_ref, bonus_ref, wl_ref, lng_ref, lnb_ref,
               o_ref, state_ref, n_chunks=n_chunks, n_groups=n_groups)
    step = ((pl.program_id(0) * pl.num_programs(1) + pl.program_id(1)) * pl.num_programs(2)
            + pl.program_id(2))
    _run_side_casts(step, side_in, side_out, side_blocks)


def _scan_body(rt_ref, kt_ref, at_ref, bt_ref, v_ref, g_ref, bonus_ref, wl_ref,
               lng_ref, lnb_ref, o_ref, state_ref, *, n_chunks, n_groups):
    rows = HEADS_PER_GROUP * CHUNK
    t_i = lax.broadcasted_iota(jnp.int32, (CHUNK, rows), 0)
    s_i = lax.broadcasted_iota(jnp.int32, (CHUNK, rows), 1) % CHUNK
    strict = t_i > s_i
    incl = t_i >= s_i
    eye = (t_i == s_i).astype(F32)
    ri = lax.broadcasted_iota(jnp.int32, (rows, GROUP_LANES), 0)
    ci = lax.broadcasted_iota(jnp.int32, (rows, GROUP_LANES), 1)
    head_mask = ri // CHUNK == ci // HEAD
    gi = lax.broadcasted_iota(jnp.int32, (GROUP_LANES, GROUP_LANES), 0)
    gj = lax.broadcasted_iota(jnp.int32, (GROUP_LANES, GROUP_LANES), 1)
    same_head = gi // HEAD == gj // HEAD
    avg = jnp.where(same_head, 1.0 / HEAD, 0.0).astype(BF16)
    nt = (((1,), (1,)), ((), ()))
    tn = (((0,), (0,)), ((), ()))
    bd = lambda x: _expand_heads(x.astype(BF16), head_mask)

    @pl.when(pl.program_id(2) == 0)
    def _():
        state_ref[...] = jnp.zeros_like(state_ref)

    chains = [(c, g) for c in range(n_chunks) for g in range(n_groups)]

    def window(ref, c, g):
        return ref[0, c * CHUNK:(c + 1) * CHUNK, g * GROUP_LANES:(g + 1) * GROUP_LANES]

    ops, a_ab, a_ak, m_rbk, t_inv, pw = {}, {}, {}, {}, {}, {}
    for ch in chains:
        a_t, r_t, b_t, k_t, v = (window(ref, *ch) for ref in (at_ref, rt_ref, bt_ref, kt_ref, v_ref))
        ops[ch] = (a_t, r_t, jnp.concatenate([b_t, k_t], axis=0), v, bd(v))
        p = lax.dot_general(jnp.concatenate([a_t, r_t], axis=0),
                            jnp.concatenate([bd(b_t), bd(k_t)], axis=0), nt,
                            preferred_element_type=F32)
        a_ab[ch] = jnp.where(strict, p[:CHUNK, :rows], 0.0)
        a_ak[ch] = jnp.where(strict, p[:CHUNK, rows:], 0.0).astype(BF16)
        m_rbk[ch] = jnp.where(jnp.concatenate([incl, incl], axis=1), p[CHUNK:], 0.0).astype(BF16)
    for ch in chains:
        t_inv[ch] = eye + a_ab[ch]
        pw[ch] = _dot(a_ab[ch].astype(BF16), bd(a_ab[ch]))
    step = 2
    while 2 * step < CHUNK:
        for ch in chains:
            both = _dot(jnp.concatenate([t_inv[ch], pw[ch]], axis=0).astype(BF16), bd(pw[ch]))
            t_inv[ch] = t_inv[ch] + both[:CHUNK]
            pw[ch] = both[CHUNK:]
        step *= 2
    taw = {}
    for ch in chains:
        t_fin = (t_inv[ch] + _dot(t_inv[ch].astype(BF16), bd(pw[ch]))).astype(BF16)
        a_t, _, _, _, v_bd = ops[ch]
        av = _dot(a_ak[ch], v_bd)
        taw[ch] = _dot(t_fin, jnp.concatenate([bd(a_t), bd(av)], axis=1))

    ys = {}
    for c in range(n_chunks):
        for g in range(n_groups):
            _, r_t, bk, v, v_bd = ops[c, g]
            lanes = slice(g * GROUP_LANES, (g + 1) * GROUP_LANES)
            state = state_ref[g]
            ars = lax.dot_general(
                jnp.concatenate([taw[c, g][:, :GROUP_LANES].astype(BF16), r_t], axis=0),
                state.astype(BF16), nt, preferred_element_type=F32)
            u = (ars[:CHUNK] + taw[c, g][:, GROUP_LANES:]).astype(BF16)
            upd = lax.dot_general(jnp.concatenate([u, v], axis=0), bk, tn,
                                  preferred_element_type=F32)
            state_ref[g] = (state + jnp.where(same_head, upd, 0.0)) * wl_ref[0, 0, c:c + 1, lanes]
            ys[c, g] = ars[CHUNK:] + _dot(m_rbk[c, g], jnp.concatenate([bd(u), v_bd], axis=0))
    for g in range(n_groups):
        lanes = slice(g * GROUP_LANES, (g + 1) * GROUP_LANES)
        y = jnp.concatenate([ys[c, g] for c in range(n_chunks)], axis=0)
        dlt = y - _dot(y.astype(BF16), avg)
        var = _dot((dlt * dlt).astype(BF16), avg)
        yn = dlt * lax.rsqrt(var + GN_EPS) * lng_ref[:, lanes] + lnb_ref[:, lanes]
        o_ref[0, :, lanes] = ((yn + bonus_ref[0, :, lanes].astype(F32))
                              * g_ref[0, :, lanes].astype(F32)).astype(o_ref.dtype)


def wkv_scan(rt, kt, at, bt, vb, g, bonus, wl, lnx_g, lnx_b, tm, side=()):
    bsz, s, rw = rt.shape
    n_groups = SCAN_GROUPS if rw % (SCAN_GROUPS * GROUP_LANES) == 0 else 1
    width = n_groups * GROUP_LANES
    n_hg, n_i = rw // width, s // tm
    blk = pl.BlockSpec((1, tm, width), lambda b, hg, i: (b, i, hg))
    vec = pl.BlockSpec((1, width), lambda b, hg, i: (0, hg))
    plans = [_side_cast_plan(w, bsz * n_hg * n_i, lambda b, hg, i: (b * n_hg + hg) * n_i + i)
             for w in side]
    outs = pl.pallas_call(
        functools.partial(_scan_kernel, n_chunks=tm // CHUNK, n_groups=n_groups,
                          side_blocks=tuple(nb for _, nb in plans)),
        grid=(bsz, n_hg, n_i),
        in_specs=[blk] * 7 + [pl.BlockSpec((1, 1, SUBLANE, width),
                                           lambda b, hg, i: (b, i, 0, hg)), vec, vec]
        + [spec for spec, _ in plans],
        out_specs=[blk] + [spec for spec, _ in plans],
        out_shape=[jax.ShapeDtypeStruct((bsz, s, rw), BF16)]
        + [jax.ShapeDtypeStruct(w.shape, BF16) for w in side],
        scratch_shapes=[pltpu.VMEM((n_groups, GROUP_LANES, GROUP_LANES), F32)],
        compiler_params=_params("arbitrary", "arbitrary", "arbitrary"),
        name="wkv_scan",
    )(rt, kt, at, bt, vb, g, bonus, wl, lnx_g.reshape(1, rw), lnx_b.reshape(1, rw), *side)
    return outs[0], tuple(outs[1:])


CONV_ROWS = 64
CONV_LANES = 256


def _conv_kernel(val_ref, gate_ref, w_ref, b_ref, lg_ref, lb_ref, o_ref, win_ref, y_ref,
                 *, tm, taps, halo):
    keep = halo + SUBLANE

    @pl.when(pl.program_id(1) == 0)
    def _():
        win_ref[:, 0:keep, :] = jnp.zeros((SUBLANE, keep, win_ref.shape[2]), F32)

    @pl.when(pl.program_id(1) != 0)
    def _():
        win_ref[:, 0:keep, :] = win_ref[:, tm:tm + keep, :]

    u = val_ref[0].astype(F32) * jax.nn.sigmoid(gate_ref[0].astype(F32))
    for r in range(SUBLANE):
        win_ref[r, halo + r:halo + r + tm, :] = u
    for c0 in range(0, u.shape[1], CONV_LANES):
        lanes = slice(c0, min(c0 + CONV_LANES, u.shape[1]))
        for r0 in range(0, tm, CONV_ROWS):
            n = min(CONV_ROWS, tm - r0)
            acc = jnp.broadcast_to(b_ref[:, lanes], (n, lanes.stop - lanes.start))
            for j in range(taps):
                q, r = divmod(taps - 1 - j, SUBLANE)
                off = halo - SUBLANE * q + r0
                acc = acc + w_ref[j:j + 1, lanes] * win_ref[r, off:off + n, lanes]
            y_ref[r0:r0 + n, lanes] = acc
    acc = y_ref[...]
    mu = jnp.mean(acc, axis=-1, keepdims=True)
    d = acc - mu
    var = jnp.mean(d * d, axis=-1, keepdims=True)
    z = d * lax.rsqrt(var + LN_EPS) * lg_ref[...] + lb_ref[...]
    o_ref[0] = (z * jax.nn.sigmoid(z)).astype(o_ref.dtype)


def conv_group(h3, rw, cw, conv_w, conv_b, ln_g, ln_b):
    bsz, s, _ = h3.shape
    taps = conv_w.shape[0]
    halo = _round_up(taps - 1, SUBLANE)
    tm = _pick(s, 128)
    assert (3 * rw) % cw == 0 and tm >= halo + SUBLANE
    c0 = (3 * rw) // cw
    row = pl.BlockSpec((1, cw), lambda b, i: (0, 0))
    return pl.pallas_call(
        functools.partial(_conv_kernel, tm=tm, taps=taps, halo=halo),
        grid=(bsz, s // tm),
        in_specs=[pl.BlockSpec((1, tm, cw), lambda b, i: (b, i, c0)),
                  pl.BlockSpec((1, tm, cw), lambda b, i: (b, i, c0 + 1)),
                  pl.BlockSpec((taps, cw), lambda b, i: (0, 0)), row, row, row],
        out_specs=pl.BlockSpec((1, tm, cw), lambda b, i: (b, i, 0)),
        out_shape=jax.ShapeDtypeStruct((bsz, s, cw), BF16),
        scratch_shapes=[pltpu.VMEM((SUBLANE, halo + tm + SUBLANE, cw), F32),
                        pltpu.VMEM((tm, cw), F32)],
        compiler_params=_params("arbitrary", "arbitrary"),
        name="conv_group",
    )(h3, h3, conv_w, conv_b.reshape(1, cw), ln_g.reshape(1, cw), ln_b.reshape(1, cw))


def _pad_cols(w, n):
    return jnp.pad(w, ((0, 0), (0, n - w.shape[1])))


def _pad_rows(w, n):
    return jnp.pad(w, ((0, n - w.shape[0]), (0, 0)))


def _mixer_weights(lp, has_vres):
    names = [('w', 'mu_w', 'w1', 'w2'), ('a', 'mu_a', 'a1', 'a2'), ('g', 'mu_g', 'g1', 'g2')]
    if has_vres:
        names.append(('v', 'mu_v', 'v1', 'v2'))
    cur, prev, low, off = [], [], [], 0
    out = dict(lp)
    for tag, mu, w1, w2 in names:
        rank = lp[w1].shape[1]
        rpad = _round_up(rank, LANE)
        cur.append(_pad_cols((1.0 - lp[mu])[:, None] * lp[w1], rpad))
        prev.append(_pad_cols(lp[mu][:, None] * lp[w1], rpad))
        out[w2 + 'p'] = _pad_rows(lp[w2], rpad).astype(BF16)
        low.append((off, rpad))
        off += rpad
    if not has_vres:
        low.append((0, 0))
    in_cols = lp['w_in'].shape[1]
    low_start = _round_up(in_cols, 2 * off)
    out['w_big'] = jnp.concatenate([_pad_cols(lp['w_in'], low_start)] + cur + prev,
                                   axis=1).astype(BF16)
    return out, tuple(low), low_start // (2 * off)


def _mixer(xm, x, lp, mods, vfirst, nxt, router, side=(), scan_side=()):
    bsz, s, d = x.shape
    rw = lp['w0'].shape[0]
    cw = lp['conv_b'].shape[0]
    has_vres = vfirst is not None
    lw, low, low_blk = _mixer_weights(lp, has_vres)
    n_cols = lw['w_big'].shape[1]
    h, side_bf16 = matmul([xm.reshape(bsz * s, d)], [lw['w_big']], BF16,
                          tn_pref=n_cols // (low_blk + 1), name="in_projection", side=side)
    h3 = h.reshape(bsz, s, n_cols)
    tm = _pick(s, 256)
    outs = rwkv_prep(h3, rw, low, low_blk, lw, vfirst, tm)
    rt, kt, at, bt, vb, g, bonus, wl = outs[:8]
    v_out = vfirst if has_vres else outs[8]
    y_rwkv, scan_bf16 = wkv_scan(rt, kt, at, bt, vb, g, bonus, wl, lp['lnx_g'], lp['lnx_b'], tm,
                                 side=scan_side)
    u = conv_group(h3, rw, cw, lp['conv_w'], lp['conv_b'], lp['conv_ln_g'], lp['conv_ln_b'])
    w_out = lp['w_out'].astype(BF16)
    o, _ = matmul([y_rwkv.reshape(bsz * s, rw), u.reshape(bsz * s, cw)], [w_out[:rw], w_out[rw:]],
                  BF16, name="out_projection")
    res = sublayer_epilogue(o.reshape(bsz, s, d), x, lp['mix_post_g'], mods['gt_m'], nxt, router)
    return res, v_out, side_bf16 + scan_bf16


def _split_layer(p, prefix, has_vres):
    keys = ['ada_w', 'ada_b', 'mix_pre_g', 'mix_post_g', 'w_in', 'mu_rkv', 'mu_w', 'mu_a', 'mu_g',
            'w0', 'w1', 'w2', 'a0', 'a1', 'a2', 'g1', 'g2', 'k_k', 'k_a', 'r_k', 'lnx_g', 'lnx_b',
            'conv_w', 'conv_b', 'conv_ln_g', 'conv_ln_b', 'w_out', 'ffn_pre_g', 'ffn_post_g']
    if has_vres:
        keys += ['mu_v', 'v0', 'v1', 'v2']
    return {k: p[prefix + k] for k in keys}


def _forward(p):
    x = p['x']
    c = p['c']
    bsz, s, d = x.shape
    layers = [_split_layer(p, 'l0_', False), _split_layer(p, 'l1_', True)]
    mods = []
    for lp in layers:
        mod = ada_modulation(c, lp['ada_w'], lp['ada_b'])
        mods.append(dict(zip(['sh_m', 'sc_m', 'gt_m', 'sh_f', 'sc_f', 'gt_f'],
                             jnp.split(mod, 6, axis=-1))))

    lp, md = layers[0], mods[0]
    xm = norm_modulate(x, lp['mix_pre_g'], md['sh_m'], md['sc_m'])
    (x, xf), vfirst, cast0 = _mixer(
        xm, x, lp, md, None, (lp['ffn_pre_g'], md['sh_f'], md['sc_f']), None,
        side=(p['l0_ffn_w_gate'], p['l0_ffn_w_up'], p['l0_ffn_w_down']),
        scan_side=(p['l1_moe_w_gate'],))
    ffn_w, moe_w_gate = cast0[:3], cast0[3]
    y, (moe_w_down,) = dense_swiglu(xf.reshape(bsz * s, d), *ffn_w, side=(p['l1_moe_w_down'],))
    nlp, nmd = layers[1], mods[1]
    x, xm = sublayer_epilogue(y.reshape(bsz, s, d), x, lp['ffn_post_g'], md['gt_f'],
                              (nlp['mix_pre_g'], nmd['sh_m'], nmd['sc_m']))

    lp, md = layers[1], mods[1]
    (x, xf, route), _, (moe_w_up,) = _mixer(xm, x, lp, md, vfirst,
                                            (lp['ffn_pre_g'], md['sh_f'], md['sc_f']),
                                            (p['l1_router_w'], p['l1_router_b']),
                                            scan_side=(p['l1_moe_w_up'],))
    return moe_sublayer(xf, route, x, lp['ffn_post_g'], md['gt_f'],
                        moe_w_gate, moe_w_up, moe_w_down)


_ARG_NAMES = (
    'x', 'c',
    'l0_ada_w', 'l0_ada_b', 'l0_mix_pre_g', 'l0_mix_post_g', 'l0_w_in', 'l0_mu_rkv', 'l0_mu_w',
    'l0_mu_a', 'l0_mu_g', 'l0_w0', 'l0_w1', 'l0_w2', 'l0_a0', 'l0_a1', 'l0_a2', 'l0_g1', 'l0_g2',
    'l0_k_k', 'l0_k_a', 'l0_r_k', 'l0_lnx_g', 'l0_lnx_b', 'l0_conv_w', 'l0_conv_b', 'l0_conv_ln_g',
    'l0_conv_ln_b', 'l0_w_out',
    'l0_ffn_pre_g', 'l0_ffn_post_g', 'l0_ffn_w_gate', 'l0_ffn_w_up', 'l0_ffn_w_down',
    'l1_ada_w', 'l1_ada_b', 'l1_mix_pre_g', 'l1_mix_post_g', 'l1_w_in', 'l1_mu_rkv', 'l1_mu_w',
    'l1_mu_a', 'l1_mu_g', 'l1_w0', 'l1_w1', 'l1_w2', 'l1_a0', 'l1_a1', 'l1_a2', 'l1_g1', 'l1_g2',
    'l1_k_k', 'l1_k_a', 'l1_r_k', 'l1_lnx_g', 'l1_lnx_b', 'l1_conv_w', 'l1_conv_b', 'l1_conv_ln_g',
    'l1_conv_ln_b', 'l1_w_out', 'l1_mu_v', 'l1_v0', 'l1_v1', 'l1_v2',
    'l1_ffn_pre_g', 'l1_ffn_post_g', 'l1_router_w', 'l1_router_b', 'l1_moe_w_gate',
    'l1_moe_w_up', 'l1_moe_w_down')


def kernel(x, c, l0_ada_w, l0_ada_b, l0_mix_pre_g, l0_mix_post_g, l0_w_in, l0_mu_rkv, l0_mu_w, l0_mu_a, l0_mu_g, l0_w0, l0_w1, l0_w2, l0_a0, l0_a1, l0_a2, l0_g1, l0_g2, l0_k_k, l0_k_a, l0_r_k, l0_lnx_g, l0_lnx_b, l0_conv_w, l0_conv_b, l0_conv_ln_g, l0_conv_ln_b, l0_w_out, l0_ffn_pre_g, l0_ffn_post_g, l0_ffn_w_gate, l0_ffn_w_up, l0_ffn_w_down, l1_ada_w, l1_ada_b, l1_mix_pre_g, l1_mix_post_g, l1_w_in, l1_mu_rkv, l1_mu_w, l1_mu_a, l1_mu_g, l1_w0, l1_w1, l1_w2, l1_a0, l1_a1, l1_a2, l1_g1, l1_g2, l1_k_k, l1_k_a, l1_r_k, l1_lnx_g, l1_lnx_b, l1_conv_w, l1_conv_b, l1_conv_ln_g, l1_conv_ln_b, l1_w_out, l1_mu_v, l1_v0, l1_v1, l1_v2, l1_ffn_pre_g, l1_ffn_post_g, l1_router_w, l1_router_b, l1_moe_w_gate, l1_moe_w_up, l1_moe_w_down):
    args = (x, c, l0_ada_w, l0_ada_b, l0_mix_pre_g, l0_mix_post_g, l0_w_in, l0_mu_rkv, l0_mu_w, l0_mu_a, l0_mu_g, l0_w0, l0_w1, l0_w2, l0_a0, l0_a1, l0_a2, l0_g1, l0_g2, l0_k_k, l0_k_a, l0_r_k, l0_lnx_g, l0_lnx_b, l0_conv_w, l0_conv_b, l0_conv_ln_g, l0_conv_ln_b, l0_w_out, l0_ffn_pre_g, l0_ffn_post_g, l0_ffn_w_gate, l0_ffn_w_up, l0_ffn_w_down, l1_ada_w, l1_ada_b, l1_mix_pre_g, l1_mix_post_g, l1_w_in, l1_mu_rkv, l1_mu_w, l1_mu_a, l1_mu_g, l1_w0, l1_w1, l1_w2, l1_a0, l1_a1, l1_a2, l1_g1, l1_g2, l1_k_k, l1_k_a, l1_r_k, l1_lnx_g, l1_lnx_b, l1_conv_w, l1_conv_b, l1_conv_ln_g, l1_conv_ln_b, l1_w_out, l1_mu_v, l1_v0, l1_v1, l1_v2, l1_ffn_pre_g, l1_ffn_post_g, l1_router_w, l1_router_b, l1_moe_w_gate, l1_moe_w_up, l1_moe_w_down)
    return _forward(dict(zip(_ARG_NAMES, args)))
```

```python
import functools

import jax
import jax.numpy as jnp
from jax import lax
from jax.experimental import pallas as pl
from jax.experimental.pallas import tpu as pltpu

F32 = jnp.float32
BF16 = jnp.bfloat16

HEAD = 64
HEADS_PER_GROUP = 4
GROUP_LANES = HEAD * HEADS_PER_GROUP
CHUNK = 64
SCAN_GROUPS = 4
LANE = 128
SUBLANE = 8
TOP_K = 2
RMS_EPS = 1e-6
GN_EPS = 64e-5
LN_EPS = 1e-5
VMEM_LIMIT_BYTES = 60 * 1024 * 1024


def _params(*sem):
    return pltpu.CompilerParams(dimension_semantics=sem, vmem_limit_bytes=VMEM_LIMIT_BYTES)


def _round_up(n, m):
    return (n + m - 1) // m * m


def _pick(n, pref):
    if n <= pref:
        return n
    t = pref
    while t >= LANE:
        if n % t == 0:
            return t
        t -= LANE
    return n


def _dot(a, b):
    return jnp.dot(a, b, preferred_element_type=F32)


def _rms(x, g):
    return x * lax.rsqrt(jnp.mean(x * x, axis=-1, keepdims=True) + RMS_EPS) * g


def _ada_kernel(c_ref, w_ref, b_ref, o_ref):
    w = w_ref[...]
    out_rows = []
    for b in range(c_ref.shape[0]):
        cb = c_ref[b]
        s = cb * jax.nn.sigmoid(cb)
        strips = [jnp.sum(w[:, c0:c0 + LANE] * s, axis=0, keepdims=True)
                  for c0 in range(0, w.shape[1], LANE)]
        out_rows.append(jnp.concatenate(strips, axis=1) + b_ref[...])
    pad = o_ref.shape[0] - len(out_rows)
    if pad:
        out_rows.append(jnp.zeros((pad, w.shape[1]), F32))
    o_ref[...] = jnp.concatenate(out_rows, axis=0)


def ada_modulation(c, ada_w, ada_b):
    bsz, d = c.shape
    n = ada_w.shape[1]
    rows = _round_up(bsz, SUBLANE)
    c_lanes = jnp.broadcast_to(c[:, :, None], (bsz, d, LANE))
    tn = _pick(n, 1024)
    out = pl.pallas_call(
        _ada_kernel,
        grid=(n // tn,),
        in_specs=[pl.BlockSpec((bsz, d, LANE), lambda j: (0, 0, 0)),
                  pl.BlockSpec((d, tn), lambda j: (0, j)),
                  pl.BlockSpec((1, tn), lambda j: (0, j))],
        out_specs=pl.BlockSpec((rows, tn), lambda j: (0, j)),
        out_shape=jax.ShapeDtypeStruct((rows, n), F32),
        compiler_params=_params("arbitrary"),
        name="ada_matvec",
    )(c_lanes, ada_w, ada_b.reshape(1, n))
    return out[:bsz]


def _norm_mod_kernel(x_ref, g_ref, sh_ref, sc_ref, o_ref):
    xn = _rms(x_ref[0], g_ref[...])
    o_ref[0] = (xn * (1.0 + sc_ref[0]) + sh_ref[0]).astype(o_ref.dtype)


def norm_modulate(x, g, shift, scale):
    bsz, s, d = x.shape
    tm = _pick(s, 512)
    vec = pl.BlockSpec((1, 1, d), lambda b, i: (b, 0, 0))
    return pl.pallas_call(
        _norm_mod_kernel,
        grid=(bsz, s // tm),
        in_specs=[pl.BlockSpec((1, tm, d), lambda b, i: (b, i, 0)),
                  pl.BlockSpec((1, d), lambda b, i: (0, 0)), vec, vec],
        out_specs=pl.BlockSpec((1, tm, d), lambda b, i: (b, i, 0)),
        out_shape=jax.ShapeDtypeStruct((bsz, s, d), BF16),
        compiler_params=_params("arbitrary", "arbitrary"),
        name="norm_modulate",
    )(x, g.reshape(1, d), shift.reshape(bsz, 1, d), scale.reshape(bsz, 1, d))


def _side_cast_plan(w, n_steps, step_of):
    *lead, r, c = w.shape
    stack = lead[0] if lead else 1
    rb = next((cand for cand in range(2 * SUBLANE, r, 2 * SUBLANE)
               if r % cand == 0 and stack * (r // cand) <= n_steps), r)
    per = r // rb
    n_blocks = stack * per
    assert n_blocks <= n_steps, (w.shape, n_steps)

    def index(*grid):
        s = jnp.minimum(step_of(*grid), n_blocks - 1)
        return (s // per, s % per, 0) if lead else (s, 0)

    return pl.BlockSpec((None, rb, c) if lead else (rb, c), index), n_blocks


def _run_side_casts(step, side_in, side_out, side_blocks):
    for src, dst, n_blocks in zip(side_in, side_out, side_blocks):
        @pl.when(step < n_blocks)
        def _(src=src, dst=dst):
            dst[...] = src[...].astype(dst.dtype)


def _mm_kernel(*refs, n_seg, side_blocks):
    n_side = len(side_blocks)
    side_in = refs[2 * n_seg:2 * n_seg + n_side]
    o_ref = refs[2 * n_seg + n_side]
    side_out = refs[2 * n_seg + n_side + 1:]
    acc = _dot(refs[0][...], refs[n_seg][...])
    for s in range(1, n_seg):
        acc = acc + _dot(refs[s][...], refs[n_seg + s][...])
    o_ref[...] = acc.astype(o_ref.dtype)
    _run_side_casts(pl.program_id(0) * pl.num_programs(1) + pl.program_id(1),
                    side_in, side_out, side_blocks)


def matmul(a_list, w_list, out_dtype, tm_pref=1024, tn_pref=1024, name="matmul", side=()):
    m = a_list[0].shape[0]
    n = w_list[0].shape[1]
    tm = _pick(m, tm_pref)
    tn = _pick(n, tn_pref)
    n_seg = len(a_list)
    n_j = n // tn
    in_specs = [pl.BlockSpec((tm, a.shape[1]), lambda i, j: (i, 0)) for a in a_list]
    in_specs += [pl.BlockSpec((w.shape[0], tn), lambda i, j: (0, j)) for w in w_list]
    plans = [_side_cast_plan(w, (m // tm) * n_j, lambda i, j: i * n_j + j) for w in side]
    outs = pl.pallas_call(
        functools.partial(_mm_kernel, n_seg=n_seg, side_blocks=tuple(nb for _, nb in plans)),
        grid=(m // tm, n_j),
        in_specs=in_specs + [spec for spec, _ in plans],
        out_specs=[pl.BlockSpec((tm, tn), lambda i, j: (i, j))] + [spec for spec, _ in plans],
        out_shape=[jax.ShapeDtypeStruct((m, n), out_dtype)]
        + [jax.ShapeDtypeStruct(w.shape, BF16) for w in side],
        compiler_params=_params("arbitrary", "arbitrary"),
        name=name,
    )(*a_list, *w_list, *side)
    return outs[0], tuple(outs[1:])


ROUTE_E1, ROUTE_E2, ROUTE_W1, ROUTE_W2 = 0, 1, 2, 3


def _top2_route(logits, n_experts):
    lane = lax.broadcasted_iota(jnp.int32, logits.shape, 1)
    neg = jnp.float32(-jnp.inf)
    lg = jnp.where(lane < n_experts, logits, neg)
    v1 = jnp.max(lg, axis=-1, keepdims=True)
    i1 = jnp.min(jnp.where(lg == v1, lane, LANE), axis=-1, keepdims=True)
    lg2 = jnp.where(lane == i1, neg, lg)
    v2 = jnp.max(lg2, axis=-1, keepdims=True)
    i2 = jnp.min(jnp.where(lg2 == v2, lane, LANE), axis=-1, keepdims=True)
    e2 = jnp.exp(v2 - v1)
    w1 = 1.0 / (1.0 + e2)
    w2 = e2 / (1.0 + e2)
    rec = jnp.where(lane == ROUTE_E1, i1.astype(F32), 0.0)
    rec = jnp.where(lane == ROUTE_E2, i2.astype(F32), rec)
    rec = jnp.where(lane == ROUTE_W1, w1, rec)
    return jnp.where(lane == ROUTE_W2, w2, rec)


def _pack_halves(x):
    half = x.shape[1] // 2
    lo = lax.bitcast_convert_type(x[:, :half].astype(BF16).astype(F32), jnp.uint32)
    hi = lax.bitcast_convert_type(x[:, half:].astype(BF16).astype(F32), jnp.uint32)
    return hi | (lo >> 16)


def _unpack_halves(w):
    lo = lax.bitcast_convert_type(w << 16, F32)
    hi = lax.bitcast_convert_type(w & jnp.uint32(0xFFFF0000), F32)
    return lo, hi


def _epilogue_kernel(*refs, has_next, n_experts):
    y_ref, x_ref, post_g_ref, gate_ref = refs[:4]
    pos = 4
    if has_next:
        pre_g_ref, sh_ref, sc_ref = refs[pos:pos + 3]
        pos += 3
    if n_experts:
        rw_ref, rb_ref = refs[pos:pos + 2]
        pos += 2
    outs = refs[pos:]
    x1 = x_ref[0] + gate_ref[0] * _rms(y_ref[0].astype(F32), post_g_ref[...])
    outs[0][0] = x1
    if has_next:
        xf = _rms(x1, pre_g_ref[...]) * (1.0 + sc_ref[0]) + sh_ref[0]
        outs[1][0] = _pack_halves(xf) if n_experts else xf.astype(outs[1].dtype)
        if n_experts:
            logits = jnp.dot(xf, rw_ref[...], preferred_element_type=F32,
                             precision=lax.Precision.HIGHEST) + rb_ref[...]
            outs[2][0] = _top2_route(logits, n_experts)


def sublayer_epilogue(y, x, post_g, gate, nxt=None, router=None):
    bsz, s, d = x.shape
    tm = _pick(s, 256)
    blk = pl.BlockSpec((1, tm, d), lambda b, i: (b, i, 0))
    row = pl.BlockSpec((1, d), lambda b, i: (0, 0))
    vec = pl.BlockSpec((1, 1, d), lambda b, i: (b, 0, 0))
    args = [y, x, post_g.reshape(1, d), gate.reshape(bsz, 1, d)]
    in_specs = [blk, blk, row, vec]
    out_shape = [jax.ShapeDtypeStruct((bsz, s, d), F32)]
    out_specs = [blk]
    n_experts = 0
    if nxt is not None:
        pre_g, shift, scale = nxt
        args += [pre_g.reshape(1, d), shift.reshape(bsz, 1, d), scale.reshape(bsz, 1, d)]
        in_specs += [row, vec, vec]
        if router is None:
            out_shape.append(jax.ShapeDtypeStruct((bsz, s, d), BF16))
            out_specs.append(blk)
        else:
            out_shape.append(jax.ShapeDtypeStruct((bsz, s, d // 2), jnp.uint32))
            out_specs.append(pl.BlockSpec((1, tm, d // 2), lambda b, i: (b, i, 0)))
        if router is not None:
            router_w, router_b = router
            n_experts = router_w.shape[1]
            rw = jnp.zeros((d, LANE), F32).at[:, :n_experts].set(router_w)
            rb = jnp.zeros((1, LANE), F32).at[0, :n_experts].set(router_b)
            args += [rw, rb]
            in_specs += [pl.BlockSpec((d, LANE), lambda b, i: (0, 0)),
                         pl.BlockSpec((1, LANE), lambda b, i: (0, 0))]
            out_shape.append(jax.ShapeDtypeStruct((bsz, s, LANE), F32))
            out_specs.append(pl.BlockSpec((1, tm, LANE), lambda b, i: (b, i, 0)))
    return pl.pallas_call(
        functools.partial(_epilogue_kernel, has_next=nxt is not None, n_experts=n_experts),
        grid=(bsz, s // tm),
        in_specs=in_specs,
        out_specs=out_specs,
        out_shape=out_shape,
        compiler_params=_params("arbitrary", "arbitrary"),
        name="sublayer_epilogue",
    )(*args)


def _ffn_kernel(*refs, side_blocks):
    n_side = len(side_blocks)
    x_ref, wg_ref, wu_ref, wd_ref = refs[:4]
    side_in = refs[4:4 + n_side]
    o_ref = refs[4 + n_side]
    side_out = refs[5 + n_side:5 + 2 * n_side]
    acc_ref = refs[5 + 2 * n_side]
    j = pl.program_id(1)

    @pl.when(j == 0)
    def _():
        acc_ref[...] = jnp.zeros_like(acc_ref)

    x = x_ref[...]
    hg = _dot(x, wg_ref[...])
    hu = _dot(x, wu_ref[...])
    h = (hg * jax.nn.sigmoid(hg) * hu).astype(BF16)
    acc_ref[...] += _dot(h, wd_ref[...])

    @pl.when(j == pl.num_programs(1) - 1)
    def _():
        o_ref[...] = acc_ref[...].astype(o_ref.dtype)

    _run_side_casts(pl.program_id(0) * pl.num_programs(1) + j, side_in, side_out, side_blocks)


def dense_swiglu(x, w_gate, w_up, w_down, side=()):
    m, d = x.shape
    f = w_gate.shape[1]
    tm = _pick(m, 512)
    tf = _pick(f, 512)
    n_j = f // tf
    plans = [_side_cast_plan(w, (m // tm) * n_j, lambda i, j: i * n_j + j) for w in side]
    outs = pl.pallas_call(
        functools.partial(_ffn_kernel, side_blocks=tuple(nb for _, nb in plans)),
        grid=(m // tm, n_j),
        in_specs=[pl.BlockSpec((tm, d), lambda i, j: (i, 0)),
                  pl.BlockSpec((d, tf), lambda i, j: (0, j)),
                  pl.BlockSpec((d, tf), lambda i, j: (0, j)),
                  pl.BlockSpec((tf, d), lambda i, j: (j, 0))] + [spec for spec, _ in plans],
        out_specs=[pl.BlockSpec((tm, d), lambda i, j: (i, 0))] + [spec for spec, _ in plans],
        out_shape=[jax.ShapeDtypeStruct((m, d), BF16)]
        + [jax.ShapeDtypeStruct(w.shape, BF16) for w in side],
        scratch_shapes=[pltpu.VMEM((tm, d), F32)],
        compiler_params=_params("arbitrary", "arbitrary"),
        name="dense_swiglu",
    )(x, w_gate, w_up, w_down, *side)
    return outs[0], tuple(outs[1:])


MOE_ROW_TILE = 512
DMA_ROWS = 256


def _rank_kernel(route_ref, tri_ref, rank_ref, count_ref, carry_ref):
    @pl.when(pl.program_id(0) == 0)
    def _():
        carry_ref[...] = jnp.zeros_like(carry_ref)

    route = route_ref[...]
    lane = lax.broadcasted_iota(jnp.int32, route.shape, 1)
    lane_f = lane.astype(F32)
    oh1 = lane_f == route[:, ROUTE_E1:ROUTE_E1 + 1]
    oh2 = lane_f == route[:, ROUTE_E2:ROUTE_E2 + 1]
    cnt = jnp.where(oh1 | oh2, 1.0, 0.0)
    before = _dot(tri_ref[...], cnt.astype(BF16)) + carry_ref[0:1, :]
    r1 = jnp.sum(jnp.where(oh1, before, 0.0), axis=-1, keepdims=True)
    r2 = jnp.sum(jnp.where(oh2, before, 0.0), axis=-1, keepdims=True)
    rank_ref[...] = jnp.where(lane == ROUTE_E1, r1, jnp.where(lane == ROUTE_E2, r2, 0.0))
    total = carry_ref[0:1, :] + jnp.sum(cnt, axis=0, keepdims=True)
    carry_ref[0:1, :] = total
    count_ref[...] = jnp.broadcast_to(total, count_ref.shape)


def expert_ranks(route):
    t = route.shape[0]
    tm = _pick(t, 256)
    ti = lax.broadcasted_iota(jnp.int32, (tm, tm), 0)
    tj = lax.broadcasted_iota(jnp.int32, (tm, tm), 1)
    tri = (ti > tj).astype(BF16)
    return pl.pallas_call(
        _rank_kernel,
        grid=(t // tm,),
        in_specs=[pl.BlockSpec((tm, LANE), lambda i: (i, 0)),
                  pl.BlockSpec((tm, tm), lambda i: (0, 0))],
        out_specs=[pl.BlockSpec((tm, LANE), lambda i: (i, 0)),
                   pl.BlockSpec((SUBLANE, LANE), lambda i: (0, 0))],
        out_shape=[jax.ShapeDtypeStruct((t, LANE), F32),
                   jax.ShapeDtypeStruct((SUBLANE, LANE), F32)],
        scratch_shapes=[pltpu.VMEM((SUBLANE, LANE), F32)],
        compiler_params=_params("arbitrary"),
        name="expert_ranks",
    )(route, tri)


def _row_copies(n_rows, make_copy):
    def start(r, carry):
        for cp in make_copy(r):
            cp.start()
        return carry

    def wait(r, carry):
        for cp in make_copy(r):
            cp.wait()
        return carry

    lax.fori_loop(0, n_rows, start, 0)
    lax.fori_loop(0, n_rows, wait, 0)


def _dispatch_kernel(pos1_ref, pos2_ref, x_ref, init_ref, o_ref, sem):
    del init_ref

    def make_copy(r):
        src = x_ref.at[pl.ds(r, 1)]
        return [pltpu.make_async_copy(src, o_ref.at[pl.ds(pos_ref[0, r], 1)], sem)
                for pos_ref in (pos1_ref, pos2_ref)]

    _row_copies(x_ref.shape[0], make_copy)


def moe_dispatch(x, pos1, pos2, n_rows):
    t, d = x.shape
    rows = pos1.shape[2]
    smem = pl.BlockSpec((None, 1, rows), lambda i: (i, 0, 0), memory_space=pltpu.SMEM)
    return pl.pallas_call(
        _dispatch_kernel,
        grid=(t // rows,),
        in_specs=[smem, smem, pl.BlockSpec((rows, d), lambda i: (i, 0)),
                  pl.BlockSpec(memory_space=pl.ANY)],
        out_specs=pl.BlockSpec(memory_space=pl.ANY),
        out_shape=jax.ShapeDtypeStruct((n_rows, d), x.dtype),
        scratch_shapes=[pltpu.SemaphoreType.DMA(())],
        input_output_aliases={3: 0},
        compiler_params=_params("arbitrary"),
        name="moe_dispatch",
    )(pos1, pos2, x, jnp.zeros((n_rows, d), x.dtype))


def _moe_kernel(tile_expert_ref, n_used_ref, x_ref, wg_ref, wu_ref, wd_ref, o_ref,
                xlo_ref, xhi_ref, acc_ref):
    del tile_expert_ref
    k = pl.program_id(1)
    half = xlo_ref.shape[1]

    @pl.when(k == 0)
    def _():
        acc_ref[...] = jnp.zeros_like(acc_ref)
        lo, hi = _unpack_halves(x_ref[...])
        xlo_ref[...] = lo.astype(BF16)
        xhi_ref[...] = hi.astype(BF16)

    @pl.when(pl.program_id(0) < n_used_ref[0])
    def _():
        xlo = xlo_ref[...]
        xhi = xhi_ref[...]
        hg = _dot(xlo, wg_ref[:half, :]) + _dot(xhi, wg_ref[half:, :])
        hu = _dot(xlo, wu_ref[:half, :]) + _dot(xhi, wu_ref[half:, :])
        h = (hg * jax.nn.sigmoid(hg) * hu).astype(BF16)
        acc_ref[...] += _dot(h, wd_ref[...])

    @pl.when(k == pl.num_programs(1) - 1)
    def _():
        o_ref[...] = _pack_halves(acc_ref[...])


def moe_grouped_swiglu(xs, tile_expert, n_used, w_gate, w_up, w_down):
    n_rows, half = xs.shape
    d = 2 * half
    f = w_gate.shape[2]
    tf = _pick(f, 512)
    live = lambda j, k, te, nu: jnp.where(j < nu[0], k, 0)
    grid_spec = pltpu.PrefetchScalarGridSpec(
        num_scalar_prefetch=2,
        grid=(n_rows // MOE_ROW_TILE, f // tf),
        in_specs=[pl.BlockSpec((MOE_ROW_TILE, half), lambda j, k, te, nu: (j, 0)),
                  pl.BlockSpec((None, d, tf), lambda j, k, te, nu: (te[j], 0, live(j, k, te, nu))),
                  pl.BlockSpec((None, d, tf), lambda j, k, te, nu: (te[j], 0, live(j, k, te, nu))),
                  pl.BlockSpec((None, tf, d), lambda j, k, te, nu: (te[j], live(j, k, te, nu), 0))],
        out_specs=pl.BlockSpec((MOE_ROW_TILE, half), lambda j, k, te, nu: (j, 0)),
        scratch_shapes=[pltpu.VMEM((MOE_ROW_TILE, half), BF16),
                        pltpu.VMEM((MOE_ROW_TILE, half), BF16),
                        pltpu.VMEM((MOE_ROW_TILE, d), F32)])
    return pl.pallas_call(
        _moe_kernel,
        grid_spec=grid_spec,
        out_shape=jax.ShapeDtypeStruct((n_rows, half), jnp.uint32),
        compiler_params=_params("arbitrary", "arbitrary"),
        name="moe_grouped_swiglu",
    )(tile_expert, n_used, xs, w_gate, w_up, w_down)


def _combine_kernel(pos1_ref, pos2_ref, route_ref, x_ref, post_g_ref, gate_ref, ys_ref, o_ref,
                    buf1, buf2, sem):
    def make_copy(r):
        return [pltpu.make_async_copy(ys_ref.at[pl.ds(pos_ref[0, r], 1)], buf.at[pl.ds(r, 1)], sem)
                for pos_ref, buf in ((pos1_ref, buf1), (pos2_ref, buf2))]

    _row_copies(buf1.shape[0], make_copy)
    route = route_ref[0]
    w1 = route[:, ROUTE_W1:ROUTE_W1 + 1]
    w2 = route[:, ROUTE_W2:ROUTE_W2 + 1]
    lo1, hi1 = _unpack_halves(buf1[...])
    lo2, hi2 = _unpack_halves(buf2[...])
    y = jnp.concatenate([w1 * lo1 + w2 * lo2, w1 * hi1 + w2 * hi2], axis=1)
    o_ref[0] = x_ref[0] + gate_ref[0] * _rms(y, post_g_ref[...])


def moe_combine_epilogue(ys, pos1, pos2, route, x, post_g, gate):
    bsz, s, d = x.shape
    rows = pos1.shape[2]
    n_i = s // rows
    smem = pl.BlockSpec((None, 1, rows), lambda b, i: (b * n_i + i, 0, 0),
                        memory_space=pltpu.SMEM)
    blk = pl.BlockSpec((1, rows, d), lambda b, i: (b, i, 0))
    return pl.pallas_call(
        _combine_kernel,
        grid=(bsz, n_i),
        in_specs=[smem, smem, pl.BlockSpec((1, rows, LANE), lambda b, i: (b, i, 0)), blk,
                  pl.BlockSpec((1, d), lambda b, i: (0, 0)),
                  pl.BlockSpec((1, 1, d), lambda b, i: (b, 0, 0)),
                  pl.BlockSpec(memory_space=pl.ANY)],
        out_specs=blk,
        out_shape=jax.ShapeDtypeStruct((bsz, s, d), F32),
        scratch_shapes=[pltpu.VMEM((rows, d // 2), jnp.uint32),
                        pltpu.VMEM((rows, d // 2), jnp.uint32),
                        pltpu.SemaphoreType.DMA(())],
        compiler_params=_params("arbitrary", "arbitrary"),
        name="moe_combine_epilogue",
    )(pos1, pos2, route, x, post_g.reshape(1, d), gate.reshape(bsz, 1, d), ys)


def moe_sublayer(xf, route, x, post_g, gate, w_gate, w_up, w_down):
    bsz, s, d = x.shape
    t = bsz * s
    n_e = w_gate.shape[0]
    rows = _pick(s, DMA_ROWS)
    assert (TOP_K * t) % MOE_ROW_TILE == 0
    n_tiles = TOP_K * t // MOE_ROW_TILE + n_e
    route2 = route.reshape(t, LANE)
    rank, count = expert_ranks(route2)
    counts = count[0, :n_e].astype(jnp.int32)
    tiles_e = (counts + MOE_ROW_TILE - 1) // MOE_ROW_TILE
    tile_end = jnp.cumsum(tiles_e)
    row_start = (tile_end - tiles_e) * MOE_ROW_TILE
    e1 = route2[:, ROUTE_E1].astype(jnp.int32)
    e2 = route2[:, ROUTE_E2].astype(jnp.int32)
    pos1 = (row_start[e1] + rank[:, ROUTE_E1].astype(jnp.int32)).reshape(t // rows, 1, rows)
    pos2 = (row_start[e2] + rank[:, ROUTE_E2].astype(jnp.int32)).reshape(t // rows, 1, rows)
    tile_expert = jnp.minimum(jnp.searchsorted(tile_end, jnp.arange(n_tiles), side='right'),
                              n_e - 1).astype(jnp.int32)
    n_used = tile_end[n_e - 1:].astype(jnp.int32)
    xs = moe_dispatch(xf.reshape(t, d // 2), pos1, pos2, n_tiles * MOE_ROW_TILE)
    ys = moe_grouped_swiglu(xs, tile_expert, n_used, w_gate, w_up, w_down)
    return moe_combine_epilogue(ys, pos1, pos2, route, x, post_g, gate)


def _shift_rows(x, carry_row):
    rolled = pltpu.roll(x, 1, axis=0)
    row = lax.broadcasted_iota(jnp.int32, x.shape, 0)
    return jnp.where(row == 0, carry_row, rolled)


def _head_sums(x, seg):
    parts = [_dot(x[:, c:c + GROUP_LANES].astype(BF16), seg)
             for c in range(0, x.shape[1], GROUP_LANES)]
    return parts[0] if len(parts) == 1 else jnp.concatenate(parts, axis=1)


def _prep_kernel(*refs, has_vres, tm, low):
    (r_ref, k_ref, v_ref, p_ref) = refs[:4]
    pos = 4
    if has_vres:
        vfirst_ref = refs[pos]
        pos += 1
    (mu_ref, w0_ref, w2_ref, a0_ref, a2_ref, g2_ref) = refs[pos:pos + 6]
    pos += 6
    if has_vres:
        v0_ref, v2_ref = refs[pos:pos + 2]
        pos += 2
    (kk_ref, ka_ref, rk_ref, seg_ref, tri_ref) = refs[pos:pos + 5]
    pos += 5
    (rt_ref, kt_ref, at_ref, bt_ref, vb_ref, g_ref, bonus_ref, wl_ref) = refs[pos:pos + 8]
    pos += 8
    if not has_vres:
        vf_ref = refs[pos]
        pos += 1
    carry_rkv, carry_p = refs[pos:pos + 2]
    rw = r_ref.shape[-1]
    n_low = p_ref.shape[-1] // 2

    @pl.when(pl.program_id(1) == 0)
    def _():
        carry_rkv[...] = jnp.zeros_like(carry_rkv)
        carry_p[...] = jnp.zeros_like(carry_p)

    def lerp_prev(ref, idx):
        cur = ref[0].astype(F32)
        prev = _shift_rows(cur, carry_rkv[0:1, idx * rw:(idx + 1) * rw])
        carry_rkv[0:1, idx * rw:(idx + 1) * rw] = cur[tm - 1:tm, :]
        return cur + (prev - cur) * mu_ref[idx:idx + 1, :]

    r = lerp_prev(r_ref, 0)
    k = lerp_prev(k_ref, 1)
    v = lerp_prev(v_ref, 2)

    p = p_ref[0].astype(F32)
    p_b = p[:, n_low:]
    lowr = p[:, :n_low] + _shift_rows(p_b, carry_p[0:1, :])
    carry_p[0:1, :] = p_b[tm - 1:tm, :]
    o_w, o_a, o_g, o_v = low

    def low_slice(o):
        return lowr[:, o[0]:o[0] + o[1]]

    zw = w0_ref[...] + _dot(jnp.tanh(low_slice(o_w)).astype(BF16), w2_ref[...])
    logw = -jnp.exp(jnp.float32(-0.5)) * jax.nn.sigmoid(zw)
    asig = jax.nn.sigmoid(a0_ref[...] + _dot(low_slice(o_a).astype(BF16), a2_ref[...]))
    g_ref[0] = _dot(jax.nn.sigmoid(low_slice(o_g)).astype(BF16), g2_ref[...]).astype(g_ref.dtype)
    if has_vres:
        vgate = jax.nn.sigmoid(v0_ref[...] + _dot(low_slice(o_v).astype(BF16), v2_ref[...]))
        v = v + (vfirst_ref[0] - v) * vgate
    else:
        vf_ref[0] = v

    seg = seg_ref[...]
    kk = k * kk_ref[...]
    kk = kk / jnp.maximum(jnp.sqrt(_head_sums(kk * kk, seg)), 1e-12)
    k = k * (1.0 + (asig - 1.0) * ka_ref[...])
    bonus_ref[0] = (_head_sums(r * k * rk_ref[...], seg) * v).astype(bonus_ref.dtype)

    cum = jnp.dot(tri_ref[...], logw, preferred_element_type=F32, precision=lax.Precision.HIGHEST)
    e_pos = jnp.exp(cum)
    e_neg = jnp.exp(-cum)
    rt_ref[0] = (r * e_pos).astype(BF16)
    kt_ref[0] = (k * e_neg).astype(BF16)
    bt_ref[0] = (kk * asig * e_neg).astype(BF16)
    at_ref[0] = (-kk * jnp.exp(cum - logw)).astype(BF16)
    vb_ref[0] = v.astype(BF16)
    ends = [cum[c * CHUNK + CHUNK - 1:c * CHUNK + CHUNK, :] for c in range(tm // CHUNK)]
    if tm // CHUNK < SUBLANE:
        ends.append(jnp.zeros((SUBLANE - tm // CHUNK, rw), F32))
    wl_ref[0, 0] = jnp.exp(jnp.concatenate(ends, axis=0))


def rwkv_prep(h3, rw, low, low_blk, lp, vfirst, tm):
    bsz, s, cols = h3.shape
    n_low2 = cols // (low_blk + 1)
    assert n_low2 * (low_blk + 1) == cols and tm % CHUNK == 0 and tm // CHUNK <= SUBLANE
    has_vres = vfirst is not None
    blk = lambda j: pl.BlockSpec((1, tm, rw), lambda b, i, j=j: (b, i, j))
    full = lambda a: pl.BlockSpec(a.shape, lambda b, i: (0,) * a.ndim)
    row = lambda a: a.reshape(1, -1)
    args = [h3, h3, h3, h3]
    in_specs = [blk(0), blk(1), blk(2),
                pl.BlockSpec((1, tm, n_low2), lambda b, i: (b, i, low_blk))]
    if has_vres:
        args.append(vfirst)
        in_specs.append(blk(0))
    seg = (lax.broadcasted_iota(jnp.int32, (GROUP_LANES, GROUP_LANES), 0) // HEAD ==
           lax.broadcasted_iota(jnp.int32, (GROUP_LANES, GROUP_LANES), 1) // HEAD).astype(BF16)
    ti = lax.broadcasted_iota(jnp.int32, (tm, tm), 0)
    tj = lax.broadcasted_iota(jnp.int32, (tm, tm), 1)
    tri = ((ti >= tj) & (ti // CHUNK == tj // CHUNK)).astype(F32)
    small = [lp['mu_rkv'].reshape(3, rw), row(lp['w0']), lp['w2p'], row(lp['a0']), lp['a2p'],
             lp['g2p']]
    if has_vres:
        small += [row(lp['v0']), lp['v2p']]
    small += [row(lp['k_k']), row(lp['k_a']), row(lp['r_k']), seg, tri]
    args += small
    in_specs += [full(a) for a in small]
    tok = lambda dt: jax.ShapeDtypeStruct((bsz, s, rw), dt)
    out_shape = [tok(BF16)] * 7 + [jax.ShapeDtypeStruct((bsz, s // tm, SUBLANE, rw), F32)]
    out_specs = [blk(0)] * 7 + [pl.BlockSpec((1, 1, SUBLANE, rw), lambda b, i: (b, i, 0, 0))]
    if not has_vres:
        out_shape.append(tok(F32))
        out_specs.append(blk(0))
    return pl.pallas_call(
        functools.partial(_prep_kernel, has_vres=has_vres, tm=tm, low=low),
        grid=(bsz, s // tm),
        in_specs=in_specs,
        out_specs=out_specs,
        out_shape=out_shape,
        scratch_shapes=[pltpu.VMEM((SUBLANE, 3 * rw), F32), pltpu.VMEM((SUBLANE, n_low2 // 2), F32)],
        compiler_params=_params("arbitrary", "arbitrary"),
        name="rwkv_prep",
    )(*args)


def _expand_heads(x, head_mask):
    return jnp.where(head_mask, jnp.concatenate([x] * HEADS_PER_GROUP, axis=0),
                     jnp.zeros((), x.dtype))


def _scan_kernel(*refs, n_chunks, n_groups, side_blocks):
    n_side = len(side_blocks)
    (rt_ref, kt_ref, at_ref, bt_ref, v_ref, g_ref, bonus_ref, wl_ref, lng_ref, lnb_ref) = refs[:10]
    side_in = refs[10:10 + n_side]
    o_ref = refs[10 + n_side]
    side_out = refs[11 + n_side:11 + 2 * n_side]
    state_ref = refs[11 + 2 * n_side]
    _scan_body(rt_ref, kt_ref, at_ref, bt_ref, v_ref, g_ref, bonus_ref, wl_ref, lng_ref, lnb_ref,
               o_ref, state_ref, n_chunks=n_chunks, n_groups=n_groups)
    step = ((pl.program_id(0) * pl.num_programs(1) + pl.program_id(1)) * pl.num_programs(2)
            + pl.program_id(2))
    _run_side_casts(step, side_in, side_out, side_blocks)


def _scan_body(rt_ref, kt_ref, at_ref, bt_ref, v_ref, g_ref, bonus_ref, wl_ref,
               lng_ref, lnb_ref, o_ref, state_ref, *, n_chunks, n_groups):
    rows = HEADS_PER_GROUP * CHUNK
    t_i = lax.broadcasted_iota(jnp.int32, (CHUNK, rows), 0)
    s_i = lax.broadcasted_iota(jnp.int32, (CHUNK, rows), 1) % CHUNK
    strict = t_i > s_i
    incl = t_i >= s_i
    eye = (t_i == s_i).astype(F32)
    ri = lax.broadcasted_iota(jnp.int32, (rows, GROUP_LANES), 0)
    ci = lax.broadcasted_iota(jnp.int32, (rows, GROUP_LANES), 1)
    head_mask = ri // CHUNK == ci // HEAD
    gi = lax.broadcasted_iota(jnp.int32, (GROUP_LANES, GROUP_LANES), 0)
    gj = lax.broadcasted_iota(jnp.int32, (GROUP_LANES, GROUP_LANES), 1)
    same_head = gi // HEAD == gj // HEAD
    avg = jnp.where(same_head, 1.0 / HEAD, 0.0).astype(BF16)
    nt = (((1,), (1,)), ((), ()))
    tn = (((0,), (0,)), ((), ()))
    bd = lambda x: _expand_heads(x.astype(BF16), head_mask)

    @pl.when(pl.program_id(2) == 0)
    def _():
        state_ref[...] = jnp.zeros_like(state_ref)

    chains = [(c, g) for c in range(n_chunks) for g in range(n_groups)]

    def window(ref, c, g):
        return ref[0, c * CHUNK:(c + 1) * CHUNK, g * GROUP_LANES:(g + 1) * GROUP_LANES]

    ops, a_ab, a_ak, m_rbk, t_inv, pw = {}, {}, {}, {}, {}, {}
    for ch in chains:
        a_t, r_t, b_t, k_t, v = (window(ref, *ch) for ref in (at_ref, rt_ref, bt_ref, kt_ref, v_ref))
        ops[ch] = (a_t, r_t, jnp.concatenate([b_t, k_t], axis=0), v, bd(v))
        p = lax.dot_general(jnp.concatenate([a_t, r_t], axis=0),
                            jnp.concatenate([bd(b_t), bd(k_t)], axis=0), nt,
                            preferred_element_type=F32)
        a_ab[ch] = jnp.where(strict, p[:CHUNK, :rows], 0.0)
        a_ak[ch] = jnp.where(strict, p[:CHUNK, rows:], 0.0).astype(BF16)
        m_rbk[ch] = jnp.where(jnp.concatenate([incl, incl], axis=1), p[CHUNK:], 0.0).astype(BF16)
    for ch in chains:
        t_inv[ch] = eye + a_ab[ch]
        pw[ch] = _dot(a_ab[ch].astype(BF16), bd(a_ab[ch]))
    step = 2
    while 2 * step < CHUNK:
        for ch in chains:
            both = _dot(jnp.concatenate([t_inv[ch], pw[ch]], axis=0).astype(BF16), bd(pw[ch]))
            t_inv[ch] = t_inv[ch] + both[:CHUNK]
            pw[ch] = both[CHUNK:]
        step *= 2
    taw = {}
    for ch in chains:
        t_fin = (t_inv[ch] + _dot(t_inv[ch].astype(BF16), bd(pw[ch]))).astype(BF16)
        a_t, _, _, _, v_bd = ops[ch]
        av = _dot(a_ak[ch], v_bd)
        taw[ch] = _dot(t_fin, jnp.concatenate([bd(a_t), bd(av)], axis=1))

    ys = {}
    for c in range(n_chunks):
        for g in range(n_groups):
            _, r_t, bk, v, v_bd = ops[c, g]
            lanes = slice(g * GROUP_LANES, (g + 1) * GROUP_LANES)
            state = state_ref[g]
            ars = lax.dot_general(
                jnp.concatenate([taw[c, g][:, :GROUP_LANES].astype(BF16), r_t], axis=0),
                state.astype(BF16), nt, preferred_element_type=F32)
            u = (ars[:CHUNK] + taw[c, g][:, GROUP_LANES:]).astype(BF16)
            upd = lax.dot_general(jnp.concatenate([u, v], axis=0), bk, tn,
                                  preferred_element_type=F32)
            state_ref[g] = (state + jnp.where(same_head, upd, 0.0)) * wl_ref[0, 0, c:c + 1, lanes]
            ys[c, g] = ars[CHUNK:] + _dot(m_rbk[c, g], jnp.concatenate([bd(u), v_bd], axis=0))
    for g in range(n_groups):
        lanes = slice(g * GROUP_LANES, (g + 1) * GROUP_LANES)
        y = jnp.concatenate([ys[c, g] for c in range(n_chunks)], axis=0)
        dlt = y - _dot(y.astype(BF16), avg)
        var = _dot((dlt * dlt).astype(BF16), avg)
        yn = dlt * lax.rsqrt(var + GN_EPS) * lng_ref[:, lanes] + lnb_ref[:, lanes]
        o_ref[0, :, lanes] = ((yn + bonus_ref[0, :, lanes].astype(F32))
                              * g_ref[0, :, lanes].astype(F32)).astype(o_ref.dtype)


def wkv_scan(rt, kt, at, bt, vb, g, bonus, wl, lnx_g, lnx_b, tm, side=()):
    bsz, s, rw = rt.shape
    n_groups = SCAN_GROUPS if rw % (SCAN_GROUPS * GROUP_LANES) == 0 else 1
    width = n_groups * GROUP_LANES
    n_hg, n_i = rw // width, s // tm
    blk = pl.BlockSpec((1, tm, width), lambda b, hg, i: (b, i, hg))
    vec = pl.BlockSpec((1, width), lambda b, hg, i: (0, hg))
    plans = [_side_cast_plan(w, bsz * n_hg * n_i, lambda b, hg, i: (b * n_hg + hg) * n_i + i)
             for w in side]
    outs = pl.pallas_call(
        functools.partial(_scan_kernel, n_chunks=tm // CHUNK, n_groups=n_groups,
                          side_blocks=tuple(nb for _, nb in plans)),
        grid=(bsz, n_hg, n_i),
        in_specs=[blk] * 7 + [pl.BlockSpec((1, 1, SUBLANE, width),
                                           lambda b, hg, i: (b, i, 0, hg)), vec, vec]
        + [spec for spec, _ in plans],
        out_specs=[blk] + [spec for spec, _ in plans],
        out_shape=[jax.ShapeDtypeStruct((bsz, s, rw), BF16)]
        + [jax.ShapeDtypeStruct(w.shape, BF16) for w in side],
        scratch_shapes=[pltpu.VMEM((n_groups, GROUP_LANES, GROUP_LANES), F32)],
        compiler_params=_params("arbitrary", "arbitrary", "arbitrary"),
        name="wkv_scan",
    )(rt, kt, at, bt, vb, g, bonus, wl, lnx_g.reshape(1, rw), lnx_b.reshape(1, rw), *side)
    return outs[0], tuple(outs[1:])


CONV_ROWS = 64
CONV_LANES = 256


def _conv_kernel(val_ref, gate_ref, w_ref, b_ref, lg_ref, lb_ref, o_ref, win_ref, y_ref,
                 *, tm, taps, halo):
    keep = halo + SUBLANE

    @pl.when(pl.program_id(1) == 0)
    def _():
        win_ref[:, 0:keep, :] = jnp.zeros((SUBLANE, keep, win_ref.shape[2]), F32)

    @pl.when(pl.program_id(1) != 0)
    def _():
        win_ref[:, 0:keep, :] = win_ref[:, tm:tm + keep, :]

    u = val_ref[0].astype(F32) * jax.nn.sigmoid(gate_ref[0].astype(F32))
    for r in range(SUBLANE):
        win_ref[r, halo + r:halo + r + tm, :] = u
    for c0 in range(0, u.shape[1], CONV_LANES):
        lanes = slice(c0, min(c0 + CONV_LANES, u.shape[1]))
        for r0 in range(0, tm, CONV_ROWS):
            n = min(CONV_ROWS, tm - r0)
            acc = jnp.broadcast_to(b_ref[:, lanes], (n, lanes.stop - lanes.start))
            for j in range(taps):
                q, r = divmod(taps - 1 - j, SUBLANE)
                off = halo - SUBLANE * q + r0
                acc = acc + w_ref[j:j + 1, lanes] * win_ref[r, off:off + n, lanes]
            y_ref[r0:r0 + n, lanes] = acc
    acc = y_ref[...]
    mu = jnp.mean(acc, axis=-1, keepdims=True)
    d = acc - mu
    var = jnp.mean(d * d, axis=-1, keepdims=True)
    z = d * lax.rsqrt(var + LN_EPS) * lg_ref[...] + lb_ref[...]
    o_ref[0] = (z * jax.nn.sigmoid(z)).astype(o_ref.dtype)


def conv_group(h3, rw, cw, conv_w, conv_b, ln_g, ln_b):
    bsz, s, _ = h3.shape
    taps = conv_w.shape[0]
    halo = _round_up(taps - 1, SUBLANE)
    tm = _pick(s, 128)
    assert (3 * rw) % cw == 0 and tm >= halo + SUBLANE
    c0 = (3 * rw) // cw
    row = pl.BlockSpec((1, cw), lambda b, i: (0, 0))
    return pl.pallas_call(
        functools.partial(_conv_kernel, tm=tm, taps=taps, halo=halo),
        grid=(bsz, s // tm),
        in_specs=[pl.BlockSpec((1, tm, cw), lambda b, i: (b, i, c0)),
                  pl.BlockSpec((1, tm, cw), lambda b, i: (b, i, c0 + 1)),
                  pl.BlockSpec((taps, cw), lambda b, i: (0, 0)), row, row, row],
        out_specs=pl.BlockSpec((1, tm, cw), lambda b, i: (b, i, 0)),
        out_shape=jax.ShapeDtypeStruct((bsz, s, cw), BF16),
        scratch_shapes=[pltpu.VMEM((SUBLANE, halo + tm + SUBLANE, cw), F32),
                        pltpu.VMEM((tm, cw), F32)],
        compiler_params=_params("arbitrary", "arbitrary"),
        name="conv_group",
    )(h3, h3, conv_w, conv_b.reshape(1, cw), ln_g.reshape(1, cw), ln_b.reshape(1, cw))


def _pad_cols(w, n):
    return jnp.pad(w, ((0, 0), (0, n - w.shape[1])))


def _pad_rows(w, n):
    return jnp.pad(w, ((0, n - w.shape[0]), (0, 0)))


def _mixer_weights(lp, has_vres):
    names = [('w', 'mu_w', 'w1', 'w2'), ('a', 'mu_a', 'a1', 'a2'), ('g', 'mu_g', 'g1', 'g2')]
    if has_vres:
        names.append(('v', 'mu_v', 'v1', 'v2'))
    cur, prev, low, off = [], [], [], 0
    out = dict(lp)
    for tag, mu, w1, w2 in names:
        rank = lp[w1].shape[1]
        rpad = _round_up(rank, LANE)
        cur.append(_pad_cols((1.0 - lp[mu])[:, None] * lp[w1], rpad))
        prev.append(_pad_cols(lp[mu][:, None] * lp[w1], rpad))
        out[w2 + 'p'] = _pad_rows(lp[w2], rpad).astype(BF16)
        low.append((off, rpad))
        off += rpad
    if not has_vres:
        low.append((0, 0))
    in_cols = lp['w_in'].shape[1]
    low_start = _round_up(in_cols, 2 * off)
    out['w_big'] = jnp.concatenate([_pad_cols(lp['w_in'], low_start)] + cur + prev,
                                   axis=1).astype(BF16)
    return out, tuple(low), low_start // (2 * off)


def _mixer(xm, x, lp, mods, vfirst, nxt, router, side=(), scan_side=(), out_side=()):
    bsz, s, d = x.shape
    rw = lp['w0'].shape[0]
    cw = lp['conv_b'].shape[0]
    has_vres = vfirst is not None
    lw, low, low_blk = _mixer_weights(lp, has_vres)
    n_cols = lw['w_big'].shape[1]
    h, side_bf16 = matmul([xm.reshape(bsz * s, d)], [lw['w_big']], BF16,
                          tn_pref=n_cols // (low_blk + 1), name="in_projection", side=side)
    h3 = h.reshape(bsz, s, n_cols)
    tm = _pick(s, 256)
    outs = rwkv_prep(h3, rw, low, low_blk, lw, vfirst, tm)
    rt, kt, at, bt, vb, g, bonus, wl = outs[:8]
    v_out = vfirst if has_vres else outs[8]
    y_rwkv, scan_bf16 = wkv_scan(rt, kt, at, bt, vb, g, bonus, wl, lp['lnx_g'], lp['lnx_b'], tm,
                                 side=scan_side)
    u = conv_group(h3, rw, cw, lp['conv_w'], lp['conv_b'], lp['conv_ln_g'], lp['conv_ln_b'])
    w_out = lp['w_out'].astype(BF16)
    o, out_bf16 = matmul([y_rwkv.reshape(bsz * s, rw), u.reshape(bsz * s, cw)],
                         [w_out[:rw], w_out[rw:]], BF16, name="out_projection", side=out_side)
    res = sublayer_epilogue(o.reshape(bsz, s, d), x, lp['mix_post_g'], mods['gt_m'], nxt, router)
    return res, v_out, side_bf16 + scan_bf16 + out_bf16


def _split_layer(p, prefix, has_vres):
    keys = ['ada_w', 'ada_b', 'mix_pre_g', 'mix_post_g', 'w_in', 'mu_rkv', 'mu_w', 'mu_a', 'mu_g',
            'w0', 'w1', 'w2', 'a0', 'a1', 'a2', 'g1', 'g2', 'k_k', 'k_a', 'r_k', 'lnx_g', 'lnx_b',
            'conv_w', 'conv_b', 'conv_ln_g', 'conv_ln_b', 'w_out', 'ffn_pre_g', 'ffn_post_g']
    if has_vres:
        keys += ['mu_v', 'v0', 'v1', 'v2']
    return {k: p[prefix + k] for k in keys}


def _forward(p):
    x = p['x']
    c = p['c']
    bsz, s, d = x.shape
    layers = [_split_layer(p, 'l0_', False), _split_layer(p, 'l1_', True)]
    mods = []
    for lp in layers:
        mod = ada_modulation(c, lp['ada_w'], lp['ada_b'])
        mods.append(dict(zip(['sh_m', 'sc_m', 'gt_m', 'sh_f', 'sc_f', 'gt_f'],
                             jnp.split(mod, 6, axis=-1))))

    lp, md = layers[0], mods[0]
    xm = norm_modulate(x, lp['mix_pre_g'], md['sh_m'], md['sc_m'])
    (x, xf), vfirst, cast0 = _mixer(
        xm, x, lp, md, None, (lp['ffn_pre_g'], md['sh_f'], md['sc_f']), None,
        side=(p['l0_ffn_w_gate'], p['l0_ffn_w_up']), scan_side=(p['l1_moe_w_gate'],),
        out_side=(p['l0_ffn_w_down'],))
    ffn_w, moe_w_gate = (cast0[0], cast0[1], cast0[3]), cast0[2]
    y, (moe_w_down,) = dense_swiglu(xf.reshape(bsz * s, d), *ffn_w, side=(p['l1_moe_w_down'],))
    nlp, nmd = layers[1], mods[1]
    x, xm = sublayer_epilogue(y.reshape(bsz, s, d), x, lp['ffn_post_g'], md['gt_f'],
                              (nlp['mix_pre_g'], nmd['sh_m'], nmd['sc_m']))

    lp, md = layers[1], mods[1]
    (x, xf, route), _, (moe_w_up,) = _mixer(xm, x, lp, md, vfirst,
                                            (lp['ffn_pre_g'], md['sh_f'], md['sc_f']),
                                            (p['l1_router_w'], p['l1_router_b']),
                                            scan_side=(p['l1_moe_w_up'],))
    return moe_sublayer(xf, route, x, lp['ffn_post_g'], md['gt_f'],
                        moe_w_gate, moe_w_up, moe_w_down)


_ARG_NAMES = (
    'x', 'c',
    'l0_ada_w', 'l0_ada_b', 'l0_mix_pre_g', 'l0_mix_post_g', 'l0_w_in', 'l0_mu_rkv', 'l0_mu_w',
    'l0_mu_a', 'l0_mu_g', 'l0_w0', 'l0_w1', 'l0_w2', 'l0_a0', 'l0_a1', 'l0_a2', 'l0_g1', 'l0_g2',
    'l0_k_k', 'l0_k_a', 'l0_r_k', 'l0_lnx_g', 'l0_lnx_b', 'l0_conv_w', 'l0_conv_b', 'l0_conv_ln_g',
    'l0_conv_ln_b', 'l0_w_out',
    'l0_ffn_pre_g', 'l0_ffn_post_g', 'l0_ffn_w_gate', 'l0_ffn_w_up', 'l0_ffn_w_down',
    'l1_ada_w', 'l1_ada_b', 'l1_mix_pre_g', 'l1_mix_post_g', 'l1_w_in', 'l1_mu_rkv', 'l1_mu_w',
    'l1_mu_a', 'l1_mu_g', 'l1_w0', 'l1_w1', 'l1_w2', 'l1_a0', 'l1_a1', 'l1_a2', 'l1_g1', 'l1_g2',
    'l1_k_k', 'l1_k_a', 'l1_r_k', 'l1_lnx_g', 'l1_lnx_b', 'l1_conv_w', 'l1_conv_b', 'l1_conv_ln_g',
    'l1_conv_ln_b', 'l1_w_out', 'l1_mu_v', 'l1_v0', 'l1_v1', 'l1_v2',
    'l1_ffn_pre_g', 'l1_ffn_post_g', 'l1_router_w', 'l1_router_b', 'l1_moe_w_gate',
    'l1_moe_w_up', 'l1_moe_w_down')


def kernel(x, c, l0_ada_w, l0_ada_b, l0_mix_pre_g, l0_mix_post_g, l0_w_in, l0_mu_rkv, l0_mu_w, l0_mu_a, l0_mu_g, l0_w0, l0_w1, l0_w2, l0_a0, l0_a1, l0_a2, l0_g1, l0_g2, l0_k_k, l0_k_a, l0_r_k, l0_lnx_g, l0_lnx_b, l0_conv_w, l0_conv_b, l0_conv_ln_g, l0_conv_ln_b, l0_w_out, l0_ffn_pre_g, l0_ffn_post_g, l0_ffn_w_gate, l0_ffn_w_up, l0_ffn_w_down, l1_ada_w, l1_ada_b, l1_mix_pre_g, l1_mix_post_g, l1_w_in, l1_mu_rkv, l1_mu_w, l1_mu_a, l1_mu_g, l1_w0, l1_w1, l1_w2, l1_a0, l1_a1, l1_a2, l1_g1, l1_g2, l1_k_k, l1_k_a, l1_r_k, l1_lnx_g, l1_lnx_b, l1_conv_w, l1_conv_b, l1_conv_ln_g, l1_conv_ln_b, l1_w_out, l1_mu_v, l1_v0, l1_v1, l1_v2, l1_ffn_pre_g, l1_ffn_post_g, l1_router_w, l1_router_b, l1_moe_w_gate, l1_moe_w_up, l1_moe_w_down):
    args = (x, c, l0_ada_w, l0_ada_b, l0_mix_pre_g, l0_mix_post_g, l0_w_in, l0_mu_rkv, l0_mu_w, l0_mu_a, l0_mu_g, l0_w0, l0_w1, l0_w2, l0_a0, l0_a1, l0_a2, l0_g1, l0_g2, l0_k_k, l0_k_a, l0_r_k, l0_lnx_g, l0_lnx_b, l0_conv_w, l0_conv_b, l0_conv_ln_g, l0_conv_ln_b, l0_w_out, l0_ffn_pre_g, l0_ffn_post_g, l0_ffn_w_gate, l0_ffn_w_up, l0_ffn_w_down, l1_ada_w, l1_ada_b, l1_mix_pre_g, l1_mix_post_g, l1_w_in, l1_mu_rkv, l1_mu_w, l1_mu_a, l1_mu_g, l1_w0, l1_w1, l1_w2, l1_a0, l1_a1, l1_a2, l1_g1, l1_g2, l1_k_k, l1_k_a, l1_r_k, l1_lnx_g, l1_lnx_b, l1_conv_w, l1_conv_b, l1_conv_ln_g, l1_conv_ln_b, l1_w_out, l1_mu_v, l1_v0, l1_v1, l1_v2, l1_ffn_pre_g, l1_ffn_post_g, l1_router_w, l1_router_b, l1_moe_w_gate, l1_moe_w_up, l1_moe_w_down)
    return _forward(dict(zip(_ARG_NAMES, args)))
```

```python
import functools

import jax
import jax.numpy as jnp
from jax import lax
from jax.experimental import pallas as pl
from jax.experimental.pallas import tpu as pltpu

F32 = jnp.float32
BF16 = jnp.bfloat16

HEAD = 64
HEADS_PER_GROUP = 4
GROUP_LANES = HEAD * HEADS_PER_GROUP
CHUNK = 64
SCAN_GROUPS = 4
LANE = 128
SUBLANE = 8
TOP_K = 2
RMS_EPS = 1e-6
GN_EPS = 64e-5
LN_EPS = 1e-5
VMEM_LIMIT_BYTES = 60 * 1024 * 1024


def _params(*sem):
    return pltpu.CompilerParams(dimension_semantics=sem, vmem_limit_bytes=VMEM_LIMIT_BYTES)


def _round_up(n, m):
    return (n + m - 1) // m * m


def _pick(n, pref):
    if n <= pref:
        return n
    t = pref
    while t >= LANE:
        if n % t == 0:
            return t
        t -= LANE
    return n


def _dot(a, b):
    return jnp.dot(a, b, preferred_element_type=F32)


def _rms(x, g):
    return x * lax.rsqrt(jnp.mean(x * x, axis=-1, keepdims=True) + RMS_EPS) * g


def _ada_kernel(c_ref, w_ref, b_ref, o_ref):
    w = w_ref[...]
    out_rows = []
    for b in range(c_ref.shape[0]):
        cb = c_ref[b]
        s = cb * jax.nn.sigmoid(cb)
        strips = [jnp.sum(w[:, c0:c0 + LANE] * s, axis=0, keepdims=True)
                  for c0 in range(0, w.shape[1], LANE)]
        out_rows.append(jnp.concatenate(strips, axis=1) + b_ref[...])
    pad = o_ref.shape[0] - len(out_rows)
    if pad:
        out_rows.append(jnp.zeros((pad, w.shape[1]), F32))
    o_ref[...] = jnp.concatenate(out_rows, axis=0)


def ada_modulation(c, ada_w, ada_b):
    bsz, d = c.shape
    n = ada_w.shape[1]
    rows = _round_up(bsz, SUBLANE)
    c_lanes = jnp.broadcast_to(c[:, :, None], (bsz, d, LANE))
    tn = _pick(n, 1024)
    out = pl.pallas_call(
        _ada_kernel,
        grid=(n // tn,),
        in_specs=[pl.BlockSpec((bsz, d, LANE), lambda j: (0, 0, 0)),
                  pl.BlockSpec((d, tn), lambda j: (0, j)),
                  pl.BlockSpec((1, tn), lambda j: (0, j))],
        out_specs=pl.BlockSpec((rows, tn), lambda j: (0, j)),
        out_shape=jax.ShapeDtypeStruct((rows, n), F32),
        compiler_params=_params("arbitrary"),
        name="ada_matvec",
    )(c_lanes, ada_w, ada_b.reshape(1, n))
    return out[:bsz]


def _norm_mod_kernel(x_ref, g_ref, sh_ref, sc_ref, o_ref):
    xn = _rms(x_ref[0], g_ref[...])
    o_ref[0] = (xn * (1.0 + sc_ref[0]) + sh_ref[0]).astype(o_ref.dtype)


def norm_modulate(x, g, shift, scale):
    bsz, s, d = x.shape
    tm = _pick(s, 512)
    vec = pl.BlockSpec((1, 1, d), lambda b, i: (b, 0, 0))
    return pl.pallas_call(
        _norm_mod_kernel,
        grid=(bsz, s // tm),
        in_specs=[pl.BlockSpec((1, tm, d), lambda b, i: (b, i, 0)),
                  pl.BlockSpec((1, d), lambda b, i: (0, 0)), vec, vec],
        out_specs=pl.BlockSpec((1, tm, d), lambda b, i: (b, i, 0)),
        out_shape=jax.ShapeDtypeStruct((bsz, s, d), BF16),
        compiler_params=_params("arbitrary", "arbitrary"),
        name="norm_modulate",
    )(x, g.reshape(1, d), shift.reshape(bsz, 1, d), scale.reshape(bsz, 1, d))


def _side_cast_plan(w, n_steps, step_of):
    *lead, r, c = w.shape
    stack = lead[0] if lead else 1
    rb = next((cand for cand in range(2 * SUBLANE, r, 2 * SUBLANE)
               if r % cand == 0 and stack * (r // cand) <= n_steps), r)
    per = r // rb
    n_blocks = stack * per
    assert n_blocks <= n_steps, (w.shape, n_steps)

    def index(*grid):
        s = jnp.minimum(step_of(*grid), n_blocks - 1)
        return (s // per, s % per, 0) if lead else (s, 0)

    return pl.BlockSpec((None, rb, c) if lead else (rb, c), index), n_blocks


def _run_side_casts(step, side_in, side_out, side_blocks):
    for src, dst, n_blocks in zip(side_in, side_out, side_blocks):
        @pl.when(step < n_blocks)
        def _(src=src, dst=dst):
            dst[...] = src[...].astype(dst.dtype)


def _mm_kernel(*refs, n_seg, side_blocks):
    n_side = len(side_blocks)
    side_in = refs[2 * n_seg:2 * n_seg + n_side]
    o_ref = refs[2 * n_seg + n_side]
    side_out = refs[2 * n_seg + n_side + 1:]
    acc = _dot(refs[0][...], refs[n_seg][...])
    for s in range(1, n_seg):
        acc = acc + _dot(refs[s][...], refs[n_seg + s][...])
    o_ref[...] = acc.astype(o_ref.dtype)
    _run_side_casts(pl.program_id(0) * pl.num_programs(1) + pl.program_id(1),
                    side_in, side_out, side_blocks)


def matmul(a_list, w_list, out_dtype, tm_pref=1024, tn_pref=1024, name="matmul", side=()):
    m = a_list[0].shape[0]
    n = w_list[0].shape[1]
    tm = _pick(m, tm_pref)
    tn = _pick(n, tn_pref)
    n_seg = len(a_list)
    n_j = n // tn
    in_specs = [pl.BlockSpec((tm, a.shape[1]), lambda i, j: (i, 0)) for a in a_list]
    in_specs += [pl.BlockSpec((w.shape[0], tn), lambda i, j: (0, j)) for w in w_list]
    plans = [_side_cast_plan(w, (m // tm) * n_j, lambda i, j: i * n_j + j) for w in side]
    outs = pl.pallas_call(
        functools.partial(_mm_kernel, n_seg=n_seg, side_blocks=tuple(nb for _, nb in plans)),
        grid=(m // tm, n_j),
        in_specs=in_specs + [spec for spec, _ in plans],
        out_specs=[pl.BlockSpec((tm, tn), lambda i, j: (i, j))] + [spec for spec, _ in plans],
        out_shape=[jax.ShapeDtypeStruct((m, n), out_dtype)]
        + [jax.ShapeDtypeStruct(w.shape, BF16) for w in side],
        compiler_params=_params("arbitrary", "arbitrary"),
        name=name,
    )(*a_list, *w_list, *side)
    return outs[0], tuple(outs[1:])


ROUTE_E1, ROUTE_E2, ROUTE_W1, ROUTE_W2 = 0, 1, 2, 3


def _top2_route(logits, n_experts):
    lane = lax.broadcasted_iota(jnp.int32, logits.shape, 1)
    neg = jnp.float32(-jnp.inf)
    lg = jnp.where(lane < n_experts, logits, neg)
    v1 = jnp.max(lg, axis=-1, keepdims=True)
    i1 = jnp.min(jnp.where(lg == v1, lane, LANE), axis=-1, keepdims=True)
    lg2 = jnp.where(lane == i1, neg, lg)
    v2 = jnp.max(lg2, axis=-1, keepdims=True)
    i2 = jnp.min(jnp.where(lg2 == v2, lane, LANE), axis=-1, keepdims=True)
    e2 = jnp.exp(v2 - v1)
    w1 = 1.0 / (1.0 + e2)
    w2 = e2 / (1.0 + e2)
    rec = jnp.where(lane == ROUTE_E1, i1.astype(F32), 0.0)
    rec = jnp.where(lane == ROUTE_E2, i2.astype(F32), rec)
    rec = jnp.where(lane == ROUTE_W1, w1, rec)
    return jnp.where(lane == ROUTE_W2, w2, rec)


def _pack_halves(x):
    half = x.shape[1] // 2
    lo = lax.bitcast_convert_type(x[:, :half].astype(BF16).astype(F32), jnp.uint32)
    hi = lax.bitcast_convert_type(x[:, half:].astype(BF16).astype(F32), jnp.uint32)
    return hi | (lo >> 16)


def _unpack_halves(w):
    lo = lax.bitcast_convert_type(w << 16, F32)
    hi = lax.bitcast_convert_type(w & jnp.uint32(0xFFFF0000), F32)
    return lo, hi


def _epilogue_kernel(*refs, has_next, n_experts):
    y_ref, x_ref, post_g_ref, gate_ref = refs[:4]
    pos = 4
    if has_next:
        pre_g_ref, sh_ref, sc_ref = refs[pos:pos + 3]
        pos += 3
    if n_experts:
        rw_ref, rb_ref = refs[pos:pos + 2]
        pos += 2
    outs = refs[pos:]
    x1 = x_ref[0] + gate_ref[0] * _rms(y_ref[0].astype(F32), post_g_ref[...])
    outs[0][0] = x1
    if has_next:
        xf = _rms(x1, pre_g_ref[...]) * (1.0 + sc_ref[0]) + sh_ref[0]
        outs[1][0] = _pack_halves(xf) if n_experts else xf.astype(outs[1].dtype)
        if n_experts:
            logits = jnp.dot(xf, rw_ref[...], preferred_element_type=F32,
                             precision=lax.Precision.HIGHEST) + rb_ref[...]
            outs[2][0] = _top2_route(logits, n_experts)


def sublayer_epilogue(y, x, post_g, gate, nxt=None, router=None):
    bsz, s, d = x.shape
    tm = _pick(s, 256)
    blk = pl.BlockSpec((1, tm, d), lambda b, i: (b, i, 0))
    row = pl.BlockSpec((1, d), lambda b, i: (0, 0))
    vec = pl.BlockSpec((1, 1, d), lambda b, i: (b, 0, 0))
    args = [y, x, post_g.reshape(1, d), gate.reshape(bsz, 1, d)]
    in_specs = [blk, blk, row, vec]
    out_shape = [jax.ShapeDtypeStruct((bsz, s, d), F32)]
    out_specs = [blk]
    n_experts = 0
    if nxt is not None:
        pre_g, shift, scale = nxt
        args += [pre_g.reshape(1, d), shift.reshape(bsz, 1, d), scale.reshape(bsz, 1, d)]
        in_specs += [row, vec, vec]
        if router is None:
            out_shape.append(jax.ShapeDtypeStruct((bsz, s, d), BF16))
            out_specs.append(blk)
        else:
            out_shape.append(jax.ShapeDtypeStruct((bsz, s, d // 2), jnp.uint32))
            out_specs.append(pl.BlockSpec((1, tm, d // 2), lambda b, i: (b, i, 0)))
        if router is not None:
            router_w, router_b = router
            n_experts = router_w.shape[1]
            rw = jnp.zeros((d, LANE), F32).at[:, :n_experts].set(router_w)
            rb = jnp.zeros((1, LANE), F32).at[0, :n_experts].set(router_b)
            args += [rw, rb]
            in_specs += [pl.BlockSpec((d, LANE), lambda b, i: (0, 0)),
                         pl.BlockSpec((1, LANE), lambda b, i: (0, 0))]
            out_shape.append(jax.ShapeDtypeStruct((bsz, s, LANE), F32))
            out_specs.append(pl.BlockSpec((1, tm, LANE), lambda b, i: (b, i, 0)))
    return pl.pallas_call(
        functools.partial(_epilogue_kernel, has_next=nxt is not None, n_experts=n_experts),
        grid=(bsz, s // tm),
        in_specs=in_specs,
        out_specs=out_specs,
        out_shape=out_shape,
        compiler_params=_params("arbitrary", "arbitrary"),
        name="sublayer_epilogue",
    )(*args)


def _ffn_kernel(*refs, side_blocks):
    n_side = len(side_blocks)
    x_ref, wg_ref, wu_ref, wd_ref = refs[:4]
    side_in = refs[4:4 + n_side]
    o_ref = refs[4 + n_side]
    side_out = refs[5 + n_side:5 + 2 * n_side]
    acc_ref = refs[5 + 2 * n_side]
    j = pl.program_id(1)

    @pl.when(j == 0)
    def _():
        acc_ref[...] = jnp.zeros_like(acc_ref)

    x = x_ref[...]
    hg = _dot(x, wg_ref[...])
    hu = _dot(x, wu_ref[...])
    h = (hg * jax.nn.sigmoid(hg) * hu).astype(BF16)
    acc_ref[...] += _dot(h, wd_ref[...])

    @pl.when(j == pl.num_programs(1) - 1)
    def _():
        o_ref[...] = acc_ref[...].astype(o_ref.dtype)

    _run_side_casts(pl.program_id(0) * pl.num_programs(1) + j, side_in, side_out, side_blocks)


def dense_swiglu(x, w_gate, w_up, w_down, side=()):
    m, d = x.shape
    f = w_gate.shape[1]
    tm = _pick(m, 512)
    tf = _pick(f, 512)
    n_j = f // tf
    plans = [_side_cast_plan(w, (m // tm) * n_j, lambda i, j: i * n_j + j) for w in side]
    outs = pl.pallas_call(
        functools.partial(_ffn_kernel, side_blocks=tuple(nb for _, nb in plans)),
        grid=(m // tm, n_j),
        in_specs=[pl.BlockSpec((tm, d), lambda i, j: (i, 0)),
                  pl.BlockSpec((d, tf), lambda i, j: (0, j)),
                  pl.BlockSpec((d, tf), lambda i, j: (0, j)),
                  pl.BlockSpec((tf, d), lambda i, j: (j, 0))] + [spec for spec, _ in plans],
        out_specs=[pl.BlockSpec((tm, d), lambda i, j: (i, 0))] + [spec for spec, _ in plans],
        out_shape=[jax.ShapeDtypeStruct((m, d), BF16)]
        + [jax.ShapeDtypeStruct(w.shape, BF16) for w in side],
        scratch_shapes=[pltpu.VMEM((tm, d), F32)],
        compiler_params=_params("arbitrary", "arbitrary"),
        name="dense_swiglu",
    )(x, w_gate, w_up, w_down, *side)
    return outs[0], tuple(outs[1:])


MOE_ROW_TILE = 512
DMA_ROWS = 256


def _rank_kernel(route_ref, tri_ref, rank_ref, count_ref, carry_ref):
    @pl.when(pl.program_id(0) == 0)
    def _():
        carry_ref[...] = jnp.zeros_like(carry_ref)

    route = route_ref[...]
    lane = lax.broadcasted_iota(jnp.int32, route.shape, 1)
    lane_f = lane.astype(F32)
    oh1 = lane_f == route[:, ROUTE_E1:ROUTE_E1 + 1]
    oh2 = lane_f == route[:, ROUTE_E2:ROUTE_E2 + 1]
    cnt = jnp.where(oh1 | oh2, 1.0, 0.0)
    before = _dot(tri_ref[...], cnt.astype(BF16)) + carry_ref[0:1, :]
    r1 = jnp.sum(jnp.where(oh1, before, 0.0), axis=-1, keepdims=True)
    r2 = jnp.sum(jnp.where(oh2, before, 0.0), axis=-1, keepdims=True)
    rank_ref[...] = jnp.where(lane == ROUTE_E1, r1, jnp.where(lane == ROUTE_E2, r2, 0.0))
    total = carry_ref[0:1, :] + jnp.sum(cnt, axis=0, keepdims=True)
    carry_ref[0:1, :] = total
    count_ref[...] = jnp.broadcast_to(total, count_ref.shape)


def expert_ranks(route):
    t = route.shape[0]
    tm = _pick(t, 256)
    ti = lax.broadcasted_iota(jnp.int32, (tm, tm), 0)
    tj = lax.broadcasted_iota(jnp.int32, (tm, tm), 1)
    tri = (ti > tj).astype(BF16)
    return pl.pallas_call(
        _rank_kernel,
        grid=(t // tm,),
        in_specs=[pl.BlockSpec((tm, LANE), lambda i: (i, 0)),
                  pl.BlockSpec((tm, tm), lambda i: (0, 0))],
        out_specs=[pl.BlockSpec((tm, LANE), lambda i: (i, 0)),
                   pl.BlockSpec((SUBLANE, LANE), lambda i: (0, 0))],
        out_shape=[jax.ShapeDtypeStruct((t, LANE), F32),
                   jax.ShapeDtypeStruct((SUBLANE, LANE), F32)],
        scratch_shapes=[pltpu.VMEM((SUBLANE, LANE), F32)],
        compiler_params=_params("arbitrary"),
        name="expert_ranks",
    )(route, tri)


def _row_copies(n_rows, make_copy):
    def start(r, carry):
        for cp in make_copy(r):
            cp.start()
        return carry

    def wait(r, carry):
        for cp in make_copy(r):
            cp.wait()
        return carry

    lax.fori_loop(0, n_rows, start, 0)
    lax.fori_loop(0, n_rows, wait, 0)


def _dispatch_kernel(pos1_ref, pos2_ref, x_ref, init_ref, o_ref, sem):
    del init_ref

    def make_copy(r):
        src = x_ref.at[pl.ds(r, 1)]
        return [pltpu.make_async_copy(src, o_ref.at[pl.ds(pos_ref[0, r], 1)], sem)
                for pos_ref in (pos1_ref, pos2_ref)]

    _row_copies(x_ref.shape[0], make_copy)


def moe_dispatch(x, pos1, pos2, n_rows):
    t, d = x.shape
    rows = pos1.shape[2]
    smem = pl.BlockSpec((None, 1, rows), lambda i: (i, 0, 0), memory_space=pltpu.SMEM)
    return pl.pallas_call(
        _dispatch_kernel,
        grid=(t // rows,),
        in_specs=[smem, smem, pl.BlockSpec((rows, d), lambda i: (i, 0)),
                  pl.BlockSpec(memory_space=pl.ANY)],
        out_specs=pl.BlockSpec(memory_space=pl.ANY),
        out_shape=jax.ShapeDtypeStruct((n_rows, d), x.dtype),
        scratch_shapes=[pltpu.SemaphoreType.DMA(())],
        input_output_aliases={3: 0},
        compiler_params=_params("arbitrary"),
        name="moe_dispatch",
    )(pos1, pos2, x, jnp.zeros((n_rows, d), x.dtype))


def _moe_kernel(tile_expert_ref, n_used_ref, x_ref, wg_ref, wu_ref, wd_ref, o_ref,
                xlo_ref, xhi_ref, acc_ref):
    del tile_expert_ref
    k = pl.program_id(1)
    half = xlo_ref.shape[1]

    @pl.when(k == 0)
    def _():
        acc_ref[...] = jnp.zeros_like(acc_ref)
        lo, hi = _unpack_halves(x_ref[...])
        xlo_ref[...] = lo.astype(BF16)
        xhi_ref[...] = hi.astype(BF16)

    @pl.when(pl.program_id(0) < n_used_ref[0])
    def _():
        xlo = xlo_ref[...]
        xhi = xhi_ref[...]
        hg = _dot(xlo, wg_ref[:half, :]) + _dot(xhi, wg_ref[half:, :])
        hu = _dot(xlo, wu_ref[:half, :]) + _dot(xhi, wu_ref[half:, :])
        h = (hg * jax.nn.sigmoid(hg) * hu).astype(BF16)
        acc_ref[...] += _dot(h, wd_ref[...])

    @pl.when(k == pl.num_programs(1) - 1)
    def _():
        o_ref[...] = _pack_halves(acc_ref[...])


def moe_grouped_swiglu(xs, tile_expert, n_used, w_gate, w_up, w_down):
    n_rows, half = xs.shape
    d = 2 * half
    f = w_gate.shape[2]
    tf = _pick(f, 512)
    live = lambda j, k, te, nu: jnp.where(j < nu[0], k, 0)
    grid_spec = pltpu.PrefetchScalarGridSpec(
        num_scalar_prefetch=2,
        grid=(n_rows // MOE_ROW_TILE, f // tf),
        in_specs=[pl.BlockSpec((MOE_ROW_TILE, half), lambda j, k, te, nu: (j, 0)),
                  pl.BlockSpec((None, d, tf), lambda j, k, te, nu: (te[j], 0, live(j, k, te, nu))),
                  pl.BlockSpec((None, d, tf), lambda j, k, te, nu: (te[j], 0, live(j, k, te, nu))),
                  pl.BlockSpec((None, tf, d), lambda j, k, te, nu: (te[j], live(j, k, te, nu), 0))],
        out_specs=pl.BlockSpec((MOE_ROW_TILE, half), lambda j, k, te, nu: (j, 0)),
        scratch_shapes=[pltpu.VMEM((MOE_ROW_TILE, half), BF16),
                        pltpu.VMEM((MOE_ROW_TILE, half), BF16),
                        pltpu.VMEM((MOE_ROW_TILE, d), F32)])
    return pl.pallas_call(
        _moe_kernel,
        grid_spec=grid_spec,
        out_shape=jax.ShapeDtypeStruct((n_rows, half), jnp.uint32),
        compiler_params=_params("arbitrary", "arbitrary"),
        name="moe_grouped_swiglu",
    )(tile_expert, n_used, xs, w_gate, w_up, w_down)


def _combine_kernel(pos1_ref, pos2_ref, route_ref, x_ref, post_g_ref, gate_ref, ys_ref, o_ref,
                    buf1, buf2, sem):
    def make_copy(r):
        return [pltpu.make_async_copy(ys_ref.at[pl.ds(pos_ref[0, r], 1)], buf.at[pl.ds(r, 1)], sem)
                for pos_ref, buf in ((pos1_ref, buf1), (pos2_ref, buf2))]

    _row_copies(buf1.shape[0], make_copy)
    route = route_ref[0]
    w1 = route[:, ROUTE_W1:ROUTE_W1 + 1]
    w2 = route[:, ROUTE_W2:ROUTE_W2 + 1]
    lo1, hi1 = _unpack_halves(buf1[...])
    lo2, hi2 = _unpack_halves(buf2[...])
    y = jnp.concatenate([w1 * lo1 + w2 * lo2, w1 * hi1 + w2 * hi2], axis=1)
    o_ref[0] = x_ref[0] + gate_ref[0] * _rms(y, post_g_ref[...])


def moe_combine_epilogue(ys, pos1, pos2, route, x, post_g, gate):
    bsz, s, d = x.shape
    rows = pos1.shape[2]
    n_i = s // rows
    smem = pl.BlockSpec((None, 1, rows), lambda b, i: (b * n_i + i, 0, 0),
                        memory_space=pltpu.SMEM)
    blk = pl.BlockSpec((1, rows, d), lambda b, i: (b, i, 0))
    return pl.pallas_call(
        _combine_kernel,
        grid=(bsz, n_i),
        in_specs=[smem, smem, pl.BlockSpec((1, rows, LANE), lambda b, i: (b, i, 0)), blk,
                  pl.BlockSpec((1, d), lambda b, i: (0, 0)),
                  pl.BlockSpec((1, 1, d), lambda b, i: (b, 0, 0)),
                  pl.BlockSpec(memory_space=pl.ANY)],
        out_specs=blk,
        out_shape=jax.ShapeDtypeStruct((bsz, s, d), F32),
        scratch_shapes=[pltpu.VMEM((rows, d // 2), jnp.uint32),
                        pltpu.VMEM((rows, d // 2), jnp.uint32),
                        pltpu.SemaphoreType.DMA(())],
        compiler_params=_params("arbitrary", "arbitrary"),
        name="moe_combine_epilogue",
    )(pos1, pos2, route, x, post_g.reshape(1, d), gate.reshape(bsz, 1, d), ys)


def moe_sublayer(xf, route, x, post_g, gate, w_gate, w_up, w_down):
    bsz, s, d = x.shape
    t = bsz * s
    n_e = w_gate.shape[0]
    rows = _pick(s, DMA_ROWS)
    assert (TOP_K * t) % MOE_ROW_TILE == 0
    n_tiles = TOP_K * t // MOE_ROW_TILE + n_e
    route2 = route.reshape(t, LANE)
    rank, count = expert_ranks(route2)
    counts = count[0, :n_e].astype(jnp.int32)
    tiles_e = (counts + MOE_ROW_TILE - 1) // MOE_ROW_TILE
    tile_end = jnp.cumsum(tiles_e)
    row_start = (tile_end - tiles_e) * MOE_ROW_TILE
    e1 = route2[:, ROUTE_E1].astype(jnp.int32)
    e2 = route2[:, ROUTE_E2].astype(jnp.int32)
    pos1 = (row_start[e1] + rank[:, ROUTE_E1].astype(jnp.int32)).reshape(t // rows, 1, rows)
    pos2 = (row_start[e2] + rank[:, ROUTE_E2].astype(jnp.int32)).reshape(t // rows, 1, rows)
    tile_expert = jnp.minimum(jnp.searchsorted(tile_end, jnp.arange(n_tiles), side='right'),
                              n_e - 1).astype(jnp.int32)
    n_used = tile_end[n_e - 1:].astype(jnp.int32)
    xs = moe_dispatch(xf.reshape(t, d // 2), pos1, pos2, n_tiles * MOE_ROW_TILE)
    ys = moe_grouped_swiglu(xs, tile_expert, n_used, w_gate, w_up, w_down)
    return moe_combine_epilogue(ys, pos1, pos2, route, x, post_g, gate)


def _shift_rows(x, carry_row):
    rolled = pltpu.roll(x, 1, axis=0)
    row = lax.broadcasted_iota(jnp.int32, x.shape, 0)
    return jnp.where(row == 0, carry_row, rolled)


def _head_sums(x, seg):
    parts = [_dot(x[:, c:c + GROUP_LANES].astype(BF16), seg)
             for c in range(0, x.shape[1], GROUP_LANES)]
    return parts[0] if len(parts) == 1 else jnp.concatenate(parts, axis=1)


def _prep_kernel(*refs, has_vres, tm, low):
    (r_ref, k_ref, v_ref, p_ref) = refs[:4]
    pos = 4
    if has_vres:
        vfirst_ref = refs[pos]
        pos += 1
    (mu_ref, w0_ref, w2_ref, a0_ref, a2_ref, g2_ref) = refs[pos:pos + 6]
    pos += 6
    if has_vres:
        v0_ref, v2_ref = refs[pos:pos + 2]
        pos += 2
    (kk_ref, ka_ref, rk_ref, seg_ref, tri_ref) = refs[pos:pos + 5]
    pos += 5
    (rt_ref, kt_ref, at_ref, bt_ref, vb_ref, g_ref, bonus_ref, wl_ref) = refs[pos:pos + 8]
    pos += 8
    if not has_vres:
        vf_ref = refs[pos]
        pos += 1
    carry_rkv, carry_p = refs[pos:pos + 2]
    rw = r_ref.shape[-1]
    n_low = p_ref.shape[-1] // 2

    @pl.when(pl.program_id(1) == 0)
    def _():
        carry_rkv[...] = jnp.zeros_like(carry_rkv)
        carry_p[...] = jnp.zeros_like(carry_p)

    def lerp_prev(ref, idx):
        cur = ref[0].astype(F32)
        prev = _shift_rows(cur, carry_rkv[0:1, idx * rw:(idx + 1) * rw])
        carry_rkv[0:1, idx * rw:(idx + 1) * rw] = cur[tm - 1:tm, :]
        return cur + (prev - cur) * mu_ref[idx:idx + 1, :]

    r = lerp_prev(r_ref, 0)
    k = lerp_prev(k_ref, 1)
    v = lerp_prev(v_ref, 2)

    p = p_ref[0].astype(F32)
    p_b = p[:, n_low:]
    lowr = p[:, :n_low] + _shift_rows(p_b, carry_p[0:1, :])
    carry_p[0:1, :] = p_b[tm - 1:tm, :]
    o_w, o_a, o_g, o_v = low

    def low_slice(o):
        return lowr[:, o[0]:o[0] + o[1]]

    zw = w0_ref[...] + _dot(jnp.tanh(low_slice(o_w)).astype(BF16), w2_ref[...])
    logw = -jnp.exp(jnp.float32(-0.5)) * jax.nn.sigmoid(zw)
    asig = jax.nn.sigmoid(a0_ref[...] + _dot(low_slice(o_a).astype(BF16), a2_ref[...]))
    g_ref[0] = _dot(jax.nn.sigmoid(low_slice(o_g)).astype(BF16), g2_ref[...]).astype(g_ref.dtype)
    if has_vres:
        vgate = jax.nn.sigmoid(v0_ref[...] + _dot(low_slice(o_v).astype(BF16), v2_ref[...]))
        v = v + (vfirst_ref[0] - v) * vgate
    else:
        vf_ref[0] = v

    seg = seg_ref[...]
    kk = k * kk_ref[...]
    kk = kk / jnp.maximum(jnp.sqrt(_head_sums(kk * kk, seg)), 1e-12)
    k = k * (1.0 + (asig - 1.0) * ka_ref[...])
    bonus_ref[0] = (_head_sums(r * k * rk_ref[...], seg) * v).astype(bonus_ref.dtype)

    cum = jnp.dot(tri_ref[...], logw, preferred_element_type=F32, precision=lax.Precision.HIGHEST)
    e_pos = jnp.exp(cum)
    e_neg = jnp.exp(-cum)
    rt_ref[0] = (r * e_pos).astype(BF16)
    kt_ref[0] = (k * e_neg).astype(BF16)
    bt_ref[0] = (kk * asig * e_neg).astype(BF16)
    at_ref[0] = (-kk * jnp.exp(cum - logw)).astype(BF16)
    vb_ref[0] = v.astype(BF16)
    ends = [cum[c * CHUNK + CHUNK - 1:c * CHUNK + CHUNK, :] for c in range(tm // CHUNK)]
    if tm // CHUNK < SUBLANE:
        ends.append(jnp.zeros((SUBLANE - tm // CHUNK, rw), F32))
    wl_ref[0, 0] = jnp.exp(jnp.concatenate(ends, axis=0))


def rwkv_prep(h3, rw, low, low_blk, lp, vfirst, tm):
    bsz, s, cols = h3.shape
    n_low2 = cols // (low_blk + 1)
    assert n_low2 * (low_blk + 1) == cols and tm % CHUNK == 0 and tm // CHUNK <= SUBLANE
    has_vres = vfirst is not None
    blk = lambda j: pl.BlockSpec((1, tm, rw), lambda b, i, j=j: (b, i, j))
    full = lambda a: pl.BlockSpec(a.shape, lambda b, i: (0,) * a.ndim)
    row = lambda a: a.reshape(1, -1)
    args = [h3, h3, h3, h3]
    in_specs = [blk(0), blk(1), blk(2),
                pl.BlockSpec((1, tm, n_low2), lambda b, i: (b, i, low_blk))]
    if has_vres:
        args.append(vfirst)
        in_specs.append(blk(0))
    seg = (lax.broadcasted_iota(jnp.int32, (GROUP_LANES, GROUP_LANES), 0) // HEAD ==
           lax.broadcasted_iota(jnp.int32, (GROUP_LANES, GROUP_LANES), 1) // HEAD).astype(BF16)
    ti = lax.broadcasted_iota(jnp.int32, (tm, tm), 0)
    tj = lax.broadcasted_iota(jnp.int32, (tm, tm), 1)
    tri = ((ti >= tj) & (ti // CHUNK == tj // CHUNK)).astype(F32)
    small = [lp['mu_rkv'].reshape(3, rw), row(lp['w0']), lp['w2p'], row(lp['a0']), lp['a2p'],
             lp['g2p']]
    if has_vres:
        small += [row(lp['v0']), lp['v2p']]
    small += [row(lp['k_k']), row(lp['k_a']), row(lp['r_k']), seg, tri]
    args += small
    in_specs += [full(a) for a in small]
    tok = lambda dt: jax.ShapeDtypeStruct((bsz, s, rw), dt)
    out_shape = [tok(BF16)] * 7 + [jax.ShapeDtypeStruct((bsz, s // tm, SUBLANE, rw), F32)]
    out_specs = [blk(0)] * 7 + [pl.BlockSpec((1, 1, SUBLANE, rw), lambda b, i: (b, i, 0, 0))]
    if not has_vres:
        out_shape.append(tok(F32))
        out_specs.append(blk(0))
    return pl.pallas_call(
        functools.partial(_prep_kernel, has_vres=has_vres, tm=tm, low=low),
        grid=(bsz, s // tm),
        in_specs=in_specs,
        out_specs=out_specs,
        out_shape=out_shape,
        scratch_shapes=[pltpu.VMEM((SUBLANE, 3 * rw), F32), pltpu.VMEM((SUBLANE, n_low2 // 2), F32)],
        compiler_params=_params("arbitrary", "arbitrary"),
        name="rwkv_prep",
    )(*args)


def _expand_heads(x, head_mask):
    return jnp.where(head_mask, jnp.concatenate([x] * HEADS_PER_GROUP, axis=0),
                     jnp.zeros((), x.dtype))


def _scan_kernel(*refs, n_chunks, n_groups, side_blocks):
    n_side = len(side_blocks)
    (rt_ref, kt_ref, at_ref, bt_ref, v_ref, g_ref, bonus_ref, wl_ref, lng_ref, lnb_ref) = refs[:10]
    side_in = refs[10:10 + n_side]
    o_ref = refs[10 + n_side]
    side_out = refs[11 + n_side:11 + 2 * n_side]
    state_ref = refs[11 + 2 * n_side]
    _scan_body(rt_ref, kt_ref, at_ref, bt_ref, v_ref, g_ref, bonus_ref, wl_ref, lng_ref, lnb_ref,
               o_ref, state_ref, n_chunks=n_chunks, n_groups=n_groups)
    step = ((pl.program_id(0) * pl.num_programs(1) + pl.program_id(1)) * pl.num_programs(2)
            + pl.program_id(2))
    _run_side_casts(step, side_in, side_out, side_blocks)


def _scan_body(rt_ref, kt_ref, at_ref, bt_ref, v_ref, g_ref, bonus_ref, wl_ref,
               lng_ref, lnb_ref, o_ref, state_ref, *, n_chunks, n_groups):
    rows = HEADS_PER_GROUP * CHUNK
    t_i = lax.broadcasted_iota(jnp.int32, (CHUNK, rows), 0)
    s_i = lax.broadcasted_iota(jnp.int32, (CHUNK, rows), 1) % CHUNK
    strict = t_i > s_i
    incl = t_i >= s_i
    eye = (t_i == s_i).astype(F32)
    ri = lax.broadcasted_iota(jnp.int32, (rows, GROUP_LANES), 0)
    ci = lax.broadcasted_iota(jnp.int32, (rows, GROUP_LANES), 1)
    head_mask = ri // CHUNK == ci // HEAD
    gi = lax.broadcasted_iota(jnp.int32, (GROUP_LANES, GROUP_LANES), 0)
    gj = lax.broadcasted_iota(jnp.int32, (GROUP_LANES, GROUP_LANES), 1)
    same_head = gi // HEAD == gj // HEAD
    avg = jnp.where(same_head, 1.0 / HEAD, 0.0).astype(BF16)
    nt = (((1,), (1,)), ((), ()))
    tn = (((0,), (0,)), ((), ()))
    bd = lambda x: _expand_heads(x.astype(BF16), head_mask)

    @pl.when(pl.program_id(2) == 0)
    def _():
        state_ref[...] = jnp.zeros_like(state_ref)

    chains = [(c, g) for c in range(n_chunks) for g in range(n_groups)]

    def window(ref, c, g):
        return ref[0, c * CHUNK:(c + 1) * CHUNK, g * GROUP_LANES:(g + 1) * GROUP_LANES]

    ops, a_ab, a_ak, m_rbk, t_inv, pw = {}, {}, {}, {}, {}, {}
    for ch in chains:
        a_t, r_t, b_t, k_t, v = (window(ref, *ch) for ref in (at_ref, rt_ref, bt_ref, kt_ref, v_ref))
        ops[ch] = (a_t, r_t, jnp.concatenate([b_t, k_t], axis=0), v, bd(v))
        p = lax.dot_general(jnp.concatenate([a_t, r_t], axis=0),
                            jnp.concatenate([bd(b_t), bd(k_t)], axis=0), nt,
                            preferred_element_type=F32)
        a_ab[ch] = jnp.where(strict, p[:CHUNK, :rows], 0.0)
        a_ak[ch] = jnp.where(strict, p[:CHUNK, rows:], 0.0).astype(BF16)
        m_rbk[ch] = jnp.where(jnp.concatenate([incl, incl], axis=1), p[CHUNK:], 0.0).astype(BF16)
    for ch in chains:
        t_inv[ch] = eye + a_ab[ch]
        pw[ch] = _dot(a_ab[ch].astype(BF16), bd(a_ab[ch]))
    step = 2
    while 2 * step < CHUNK:
        for ch in chains:
            both = _dot(jnp.concatenate([t_inv[ch], pw[ch]], axis=0).astype(BF16), bd(pw[ch]))
            t_inv[ch] = t_inv[ch] + both[:CHUNK]
            pw[ch] = both[CHUNK:]
        step *= 2
    taw = {}
    for ch in chains:
        t_fin = (t_inv[ch] + _dot(t_inv[ch].astype(BF16), bd(pw[ch]))).astype(BF16)
        a_t, _, _, _, v_bd = ops[ch]
        av = _dot(a_ak[ch], v_bd)
        taw[ch] = _dot(t_fin, jnp.concatenate([bd(a_t), bd(av)], axis=1))

    ys = {}
    for c in range(n_chunks):
        for g in range(n_groups):
            _, r_t, bk, v, v_bd = ops[c, g]
            lanes = slice(g * GROUP_LANES, (g + 1) * GROUP_LANES)
            state = state_ref[g]
            ars = lax.dot_general(
                jnp.concatenate([taw[c, g][:, :GROUP_LANES].astype(BF16), r_t], axis=0),
                state.astype(BF16), nt, preferred_element_type=F32)
            u = (ars[:CHUNK] + taw[c, g][:, GROUP_LANES:]).astype(BF16)
            upd = lax.dot_general(jnp.concatenate([u, v], axis=0), bk, tn,
                                  preferred_element_type=F32)
            state_ref[g] = (state + jnp.where(same_head, upd, 0.0)) * wl_ref[0, 0, c:c + 1, lanes]
            ys[c, g] = ars[CHUNK:] + _dot(m_rbk[c, g], jnp.concatenate([bd(u), v_bd], axis=0))
    for g in range(n_groups):
        lanes = slice(g * GROUP_LANES, (g + 1) * GROUP_LANES)
        y = jnp.concatenate([ys[c, g] for c in range(n_chunks)], axis=0)
        dlt = y - _dot(y.astype(BF16), avg)
        var = _dot((dlt * dlt).astype(BF16), avg)
        yn = dlt * lax.rsqrt(var + GN_EPS) * lng_ref[:, lanes] + lnb_ref[:, lanes]
        o_ref[0, :, lanes] = ((yn + bonus_ref[0, :, lanes].astype(F32))
                              * g_ref[0, :, lanes].astype(F32)).astype(o_ref.dtype)


def wkv_scan(rt, kt, at, bt, vb, g, bonus, wl, lnx_g, lnx_b, tm, side=()):
    bsz, s, rw = rt.shape
    n_groups = SCAN_GROUPS if rw % (SCAN_GROUPS * GROUP_LANES) == 0 else 1
    width = n_groups * GROUP_LANES
    n_hg, n_i = rw // width, s // tm
    blk = pl.BlockSpec((1, tm, width), lambda b, hg, i: (b, i, hg))
    vec = pl.BlockSpec((1, width), lambda b, hg, i: (0, hg))
    plans = [_side_cast_plan(w, bsz * n_hg * n_i, lambda b, hg, i: (b * n_hg + hg) * n_i + i)
             for w in side]
    outs = pl.pallas_call(
        functools.partial(_scan_kernel, n_chunks=tm // CHUNK, n_groups=n_groups,
                          side_blocks=tuple(nb for _, nb in plans)),
        grid=(bsz, n_hg, n_i),
        in_specs=[blk] * 7 + [pl.BlockSpec((1, 1, SUBLANE, width),
                                           lambda b, hg, i: (b, i, 0, hg)), vec, vec]
        + [spec for spec, _ in plans],
        out_specs=[blk] + [spec for spec, _ in plans],
        out_shape=[jax.ShapeDtypeStruct((bsz, s, rw), BF16)]
        + [jax.ShapeDtypeStruct(w.shape, BF16) for w in side],
        scratch_shapes=[pltpu.VMEM((n_groups, GROUP_LANES, GROUP_LANES), F32)],
        compiler_params=_params("arbitrary", "arbitrary", "arbitrary"),
        name="wkv_scan",
    )(rt, kt, at, bt, vb, g, bonus, wl, lnx_g.reshape(1, rw), lnx_b.reshape(1, rw), *side)
    return outs[0], tuple(outs[1:])


CONV_ROWS = 64
CONV_LANES = 256


def _conv_kernel(val_ref, gate_ref, w_ref, b_ref, lg_ref, lb_ref, o_ref, win_ref, y_ref,
                 *, tm, taps, halo):
    keep = halo + SUBLANE

    @pl.when(pl.program_id(1) == 0)
    def _():
        win_ref[:, 0:keep, :] = jnp.zeros((SUBLANE, keep, win_ref.shape[2]), F32)

    @pl.when(pl.program_id(1) != 0)
    def _():
        win_ref[:, 0:keep, :] = win_ref[:, tm:tm + keep, :]

    u = val_ref[0].astype(F32) * jax.nn.sigmoid(gate_ref[0].astype(F32))
    for r in range(SUBLANE):
        win_ref[r, halo + r:halo + r + tm, :] = u
    for c0 in range(0, u.shape[1], CONV_LANES):
        lanes = slice(c0, min(c0 + CONV_LANES, u.shape[1]))
        for r0 in range(0, tm, CONV_ROWS):
            n = min(CONV_ROWS, tm - r0)
            acc = jnp.broadcast_to(b_ref[:, lanes], (n, lanes.stop - lanes.start))
            for j in range(taps):
                q, r = divmod(taps - 1 - j, SUBLANE)
                off = halo - SUBLANE * q + r0
                acc = acc + w_ref[j:j + 1, lanes] * win_ref[r, off:off + n, lanes]
            y_ref[r0:r0 + n, lanes] = acc
    acc = y_ref[...]
    mu = jnp.mean(acc, axis=-1, keepdims=True)
    d = acc - mu
    var = jnp.mean(d * d, axis=-1, keepdims=True)
    z = d * lax.rsqrt(var + LN_EPS) * lg_ref[...] + lb_ref[...]
    o_ref[0] = (z * jax.nn.sigmoid(z)).astype(o_ref.dtype)


def conv_group(h3, rw, cw, conv_w, conv_b, ln_g, ln_b):
    bsz, s, _ = h3.shape
    taps = conv_w.shape[0]
    halo = _round_up(taps - 1, SUBLANE)
    tm = _pick(s, 128)
    assert (3 * rw) % cw == 0 and tm >= halo + SUBLANE
    c0 = (3 * rw) // cw
    row = pl.BlockSpec((1, cw), lambda b, i: (0, 0))
    return pl.pallas_call(
        functools.partial(_conv_kernel, tm=tm, taps=taps, halo=halo),
        grid=(bsz, s // tm),
        in_specs=[pl.BlockSpec((1, tm, cw), lambda b, i: (b, i, c0)),
                  pl.BlockSpec((1, tm, cw), lambda b, i: (b, i, c0 + 1)),
                  pl.BlockSpec((taps, cw), lambda b, i: (0, 0)), row, row, row],
        out_specs=pl.BlockSpec((1, tm, cw), lambda b, i: (b, i, 0)),
        out_shape=jax.ShapeDtypeStruct((bsz, s, cw), BF16),
        scratch_shapes=[pltpu.VMEM((SUBLANE, halo + tm + SUBLANE, cw), F32),
                        pltpu.VMEM((tm, cw), F32)],
        compiler_params=_params("arbitrary", "arbitrary"),
        name="conv_group",
    )(h3, h3, conv_w, conv_b.reshape(1, cw), ln_g.reshape(1, cw), ln_b.reshape(1, cw))


def _pad_cols(w, n):
    return jnp.pad(w, ((0, 0), (0, n - w.shape[1])))


def _pad_rows(w, n):
    return jnp.pad(w, ((0, n - w.shape[0]), (0, 0)))


def _mixer_weights(lp, has_vres):
    names = [('w', 'mu_w', 'w1', 'w2'), ('a', 'mu_a', 'a1', 'a2'), ('g', 'mu_g', 'g1', 'g2')]
    if has_vres:
        names.append(('v', 'mu_v', 'v1', 'v2'))
    cur, prev, low, off = [], [], [], 0
    out = dict(lp)
    for tag, mu, w1, w2 in names:
        rank = lp[w1].shape[1]
        rpad = _round_up(rank, LANE)
        cur.append(_pad_cols((1.0 - lp[mu])[:, None] * lp[w1], rpad))
        prev.append(_pad_cols(lp[mu][:, None] * lp[w1], rpad))
        out[w2 + 'p'] = _pad_rows(lp[w2], rpad).astype(BF16)
        low.append((off, rpad))
        off += rpad
    if not has_vres:
        low.append((0, 0))
    in_cols = lp['w_in'].shape[1]
    low_start = _round_up(in_cols, 2 * off)
    out['w_big'] = jnp.concatenate([_pad_cols(lp['w_in'], low_start)] + cur + prev,
                                   axis=1).astype(BF16)
    return out, tuple(low), low_start // (2 * off)


def _mixer(xm, x, lp, mods, vfirst, nxt, router, side=(), scan_side=(), out_side=()):
    bsz, s, d = x.shape
    rw = lp['w0'].shape[0]
    cw = lp['conv_b'].shape[0]
    has_vres = vfirst is not None
    lw, low, low_blk = _mixer_weights(lp, has_vres)
    n_cols = lw['w_big'].shape[1]
    h, side_bf16 = matmul([xm.reshape(bsz * s, d)], [lw['w_big']], BF16,
                          tn_pref=n_cols // (low_blk + 1), name="in_projection", side=side)
    h3 = h.reshape(bsz, s, n_cols)
    tm = _pick(s, 256)
    outs = rwkv_prep(h3, rw, low, low_blk, lw, vfirst, tm)
    rt, kt, at, bt, vb, g, bonus, wl = outs[:8]
    v_out = vfirst if has_vres else outs[8]
    y_rwkv, scan_bf16 = wkv_scan(rt, kt, at, bt, vb, g, bonus, wl, lp['lnx_g'], lp['lnx_b'], tm,
                                 side=scan_side)
    u = conv_group(h3, rw, cw, lp['conv_w'], lp['conv_b'], lp['conv_ln_g'], lp['conv_ln_b'])
    w_out = lp['w_out'].astype(BF16)
    o, out_bf16 = matmul([y_rwkv.reshape(bsz * s, rw), u.reshape(bsz * s, cw)],
                         [w_out[:rw], w_out[rw:]], BF16, name="out_projection", side=out_side)
    res = sublayer_epilogue(o.reshape(bsz, s, d), x, lp['mix_post_g'], mods['gt_m'], nxt, router)
    return res, v_out, side_bf16 + scan_bf16 + out_bf16


def _split_layer(p, prefix, has_vres):
    keys = ['ada_w', 'ada_b', 'mix_pre_g', 'mix_post_g', 'w_in', 'mu_rkv', 'mu_w', 'mu_a', 'mu_g',
            'w0', 'w1', 'w2', 'a0', 'a1', 'a2', 'g1', 'g2', 'k_k', 'k_a', 'r_k', 'lnx_g', 'lnx_b',
            'conv_w', 'conv_b', 'conv_ln_g', 'conv_ln_b', 'w_out', 'ffn_pre_g', 'ffn_post_g']
    if has_vres:
        keys += ['mu_v', 'v0', 'v1', 'v2']
    return {k: p[prefix + k] for k in keys}


def _forward(p):
    x = p['x']
    c = p['c']
    bsz, s, d = x.shape
    layers = [_split_layer(p, 'l0_', False), _split_layer(p, 'l1_', True)]
    mods = []
    for lp in layers:
        mod = ada_modulation(c, lp['ada_w'], lp['ada_b'])
        mods.append(dict(zip(['sh_m', 'sc_m', 'gt_m', 'sh_f', 'sc_f', 'gt_f'],
                             jnp.split(mod, 6, axis=-1))))

    lp, md = layers[0], mods[0]
    xm = norm_modulate(x, lp['mix_pre_g'], md['sh_m'], md['sc_m'])
    (x, xf), vfirst, cast0 = _mixer(
        xm, x, lp, md, None, (lp['ffn_pre_g'], md['sh_f'], md['sc_f']), None,
        side=(p['l0_ffn_w_gate'], p['l0_ffn_w_up']), scan_side=(p['l1_moe_w_gate'],),
        out_side=(p['l0_ffn_w_down'],))
    ffn_w, moe_w_gate = (cast0[0], cast0[1], cast0[3]), cast0[2]
    y, _ = dense_swiglu(xf.reshape(bsz * s, d), *ffn_w)
    nlp, nmd = layers[1], mods[1]
    x, xm = sublayer_epilogue(y.reshape(bsz, s, d), x, lp['ffn_post_g'], md['gt_f'],
                              (nlp['mix_pre_g'], nmd['sh_m'], nmd['sc_m']))

    lp, md = layers[1], mods[1]
    (x, xf, route), _, (moe_w_down, moe_w_up) = _mixer(
        xm, x, lp, md, vfirst, (lp['ffn_pre_g'], md['sh_f'], md['sc_f']),
        (p['l1_router_w'], p['l1_router_b']),
        side=(p['l1_moe_w_down'],), scan_side=(p['l1_moe_w_up'],))
    return moe_sublayer(xf, route, x, lp['ffn_post_g'], md['gt_f'],
                        moe_w_gate, moe_w_up, moe_w_down)


_ARG_NAMES = (
    'x', 'c',
    'l0_ada_w', 'l0_ada_b', 'l0_mix_pre_g', 'l0_mix_post_g', 'l0_w_in', 'l0_mu_rkv', 'l0_mu_w',
    'l0_mu_a', 'l0_mu_g', 'l0_w0', 'l0_w1', 'l0_w2', 'l0_a0', 'l0_a1', 'l0_a2', 'l0_g1', 'l0_g2',
    'l0_k_k', 'l0_k_a', 'l0_r_k', 'l0_lnx_g', 'l0_lnx_b', 'l0_conv_w', 'l0_conv_b', 'l0_conv_ln_g',
    'l0_conv_ln_b', 'l0_w_out',
    'l0_ffn_pre_g', 'l0_ffn_post_g', 'l0_ffn_w_gate', 'l0_ffn_w_up', 'l0_ffn_w_down',
    'l1_ada_w', 'l1_ada_b', 'l1_mix_pre_g', 'l1_mix_post_g', 'l1_w_in', 'l1_mu_rkv', 'l1_mu_w',
    'l1_mu_a', 'l1_mu_g', 'l1_w0', 'l1_w1', 'l1_w2', 'l1_a0', 'l1_a1', 'l1_a2', 'l1_g1', 'l1_g2',
    'l1_k_k', 'l1_k_a', 'l1_r_k', 'l1_lnx_g', 'l1_lnx_b', 'l1_conv_w', 'l1_conv_b', 'l1_conv_ln_g',
    'l1_conv_ln_b', 'l1_w_out', 'l1_mu_v', 'l1_v0', 'l1_v1', 'l1_v2',
    'l1_ffn_pre_g', 'l1_ffn_post_g', 'l1_router_w', 'l1_router_b', 'l1_moe_w_gate',
    'l1_moe_w_up', 'l1_moe_w_down')


def kernel(x, c, l0_ada_w, l0_ada_b, l0_mix_pre_g, l0_mix_post_g, l0_w_in, l0_mu_rkv, l0_mu_w, l0_mu_a, l0_mu_g, l0_w0, l0_w1, l0_w2, l0_a0, l0_a1, l0_a2, l0_g1, l0_g2, l0_k_k, l0_k_a, l0_r_k, l0_lnx_g, l0_lnx_b, l0_conv_w, l0_conv_b, l0_conv_ln_g, l0_conv_ln_b, l0_w_out, l0_ffn_pre_g, l0_ffn_post_g, l0_ffn_w_gate, l0_ffn_w_up, l0_ffn_w_down, l1_ada_w, l1_ada_b, l1_mix_pre_g, l1_mix_post_g, l1_w_in, l1_mu_rkv, l1_mu_w, l1_mu_a, l1_mu_g, l1_w0, l1_w1, l1_w2, l1_a0, l1_a1, l1_a2, l1_g1, l1_g2, l1_k_k, l1_k_a, l1_r_k, l1_lnx_g, l1_lnx_b, l1_conv_w, l1_conv_b, l1_conv_ln_g, l1_conv_ln_b, l1_w_out, l1_mu_v, l1_v0, l1_v1, l1_v2, l1_ffn_pre_g, l1_ffn_post_g, l1_router_w, l1_router_b, l1_moe_w_gate, l1_moe_w_up, l1_moe_w_down):
    args = (x, c, l0_ada_w, l0_ada_b, l0_mix_pre_g, l0_mix_post_g, l0_w_in, l0_mu_rkv, l0_mu_w, l0_mu_a, l0_mu_g, l0_w0, l0_w1, l0_w2, l0_a0, l0_a1, l0_a2, l0_g1, l0_g2, l0_k_k, l0_k_a, l0_r_k, l0_lnx_g, l0_lnx_b, l0_conv_w, l0_conv_b, l0_conv_ln_g, l0_conv_ln_b, l0_w_out, l0_ffn_pre_g, l0_ffn_post_g, l0_ffn_w_gate, l0_ffn_w_up, l0_ffn_w_down, l1_ada_w, l1_ada_b, l1_mix_pre_g, l1_mix_post_g, l1_w_in, l1_mu_rkv, l1_mu_w, l1_mu_a, l1_mu_g, l1_w0, l1_w1, l1_w2, l1_a0, l1_a1, l1_a2, l1_g1, l1_g2, l1_k_k, l1_k_a, l1_r_k, l1_lnx_g, l1_lnx_b, l1_conv_w, l1_conv_b, l1_conv_ln_g, l1_conv_ln_b, l1_w_out, l1_mu_v, l1_v0, l1_v1, l1_v2, l1_ffn_pre_g, l1_ffn_post_g, l1_router_w, l1_router_b, l1_moe_w_gate, l1_moe_w_up, l1_moe_w_down)
    return _forward(dict(zip(_ARG_NAMES, args)))
```

```python
import functools

import jax
import jax.numpy as jnp
from jax import lax
from jax.experimental import pallas as pl
from jax.experimental.pallas import tpu as pltpu

F32 = jnp.float32
BF16 = jnp.bfloat16

HEAD = 64
HEADS_PER_GROUP = 4
GROUP_LANES = HEAD * HEADS_PER_GROUP
CHUNK = 64
SCAN_GROUPS = 4
LANE = 128
SUBLANE = 8
TOP_K = 2
RMS_EPS = 1e-6
GN_EPS = 64e-5
LN_EPS = 1e-5
VMEM_LIMIT_BYTES = 60 * 1024 * 1024


def _params(*sem):
    return pltpu.CompilerParams(dimension_semantics=sem, vmem_limit_bytes=VMEM_LIMIT_BYTES)


def _round_up(n, m):
    return (n + m - 1) // m * m


def _pick(n, pref):
    if n <= pref:
        return n
    t = pref
    while t >= LANE:
        if n % t == 0:
            return t
        t -= LANE
    return n


def _dot(a, b):
    return jnp.dot(a, b, preferred_element_type=F32)


def _rms(x, g):
    return x * lax.rsqrt(jnp.mean(x * x, axis=-1, keepdims=True) + RMS_EPS) * g


def _ada_kernel(c_ref, w_ref, b_ref, o_ref):
    w = w_ref[...]
    out_rows = []
    for b in range(c_ref.shape[0]):
        cb = c_ref[b]
        s = cb * jax.nn.sigmoid(cb)
        strips = [jnp.sum(w[:, c0:c0 + LANE] * s, axis=0, keepdims=True)
                  for c0 in range(0, w.shape[1], LANE)]
        out_rows.append(jnp.concatenate(strips, axis=1) + b_ref[...])
    pad = o_ref.shape[0] - len(out_rows)
    if pad:
        out_rows.append(jnp.zeros((pad, w.shape[1]), F32))
    o_ref[...] = jnp.concatenate(out_rows, axis=0)


def ada_modulation(c, ada_w, ada_b):
    bsz, d = c.shape
    n = ada_w.shape[1]
    rows = _round_up(bsz, SUBLANE)
    c_lanes = jnp.broadcast_to(c[:, :, None], (bsz, d, LANE))
    tn = _pick(n, 1024)
    out = pl.pallas_call(
        _ada_kernel,
        grid=(n // tn,),
        in_specs=[pl.BlockSpec((bsz, d, LANE), lambda j: (0, 0, 0)),
                  pl.BlockSpec((d, tn), lambda j: (0, j)),
                  pl.BlockSpec((1, tn), lambda j: (0, j))],
        out_specs=pl.BlockSpec((rows, tn), lambda j: (0, j)),
        out_shape=jax.ShapeDtypeStruct((rows, n), F32),
        compiler_params=_params("arbitrary"),
        name="ada_matvec",
    )(c_lanes, ada_w, ada_b.reshape(1, n))
    return out[:bsz]


def _norm_mod_kernel(x_ref, g_ref, sh_ref, sc_ref, o_ref):
    xn = _rms(x_ref[0], g_ref[...])
    o_ref[0] = (xn * (1.0 + sc_ref[0]) + sh_ref[0]).astype(o_ref.dtype)


def norm_modulate(x, g, shift, scale):
    bsz, s, d = x.shape
    tm = _pick(s, 512)
    vec = pl.BlockSpec((1, 1, d), lambda b, i: (b, 0, 0))
    return pl.pallas_call(
        _norm_mod_kernel,
        grid=(bsz, s // tm),
        in_specs=[pl.BlockSpec((1, tm, d), lambda b, i: (b, i, 0)),
                  pl.BlockSpec((1, d), lambda b, i: (0, 0)), vec, vec],
        out_specs=pl.BlockSpec((1, tm, d), lambda b, i: (b, i, 0)),
        out_shape=jax.ShapeDtypeStruct((bsz, s, d), BF16),
        compiler_params=_params("arbitrary", "arbitrary"),
        name="norm_modulate",
    )(x, g.reshape(1, d), shift.reshape(bsz, 1, d), scale.reshape(bsz, 1, d))


def _side_cast_plan(w, n_steps, step_of):
    *lead, r, c = w.shape
    stack = lead[0] if lead else 1
    rb = next((cand for cand in range(2 * SUBLANE, r, 2 * SUBLANE)
               if r % cand == 0 and stack * (r // cand) <= n_steps), r)
    per = r // rb
    n_blocks = stack * per
    assert n_blocks <= n_steps, (w.shape, n_steps)

    def index(*grid):
        s = jnp.minimum(step_of(*grid), n_blocks - 1)
        return (s // per, s % per, 0) if lead else (s, 0)

    return pl.BlockSpec((None, rb, c) if lead else (rb, c), index), n_blocks


def _run_side_casts(step, side_in, side_out, side_blocks):
    for src, dst, n_blocks in zip(side_in, side_out, side_blocks):
        @pl.when(step < n_blocks)
        def _(src=src, dst=dst):
            dst[...] = src[...].astype(dst.dtype)


def _mm_kernel(*refs, n_seg, side_blocks):
    n_side = len(side_blocks)
    side_in = refs[2 * n_seg:2 * n_seg + n_side]
    o_ref = refs[2 * n_seg + n_side]
    side_out = refs[2 * n_seg + n_side + 1:]
    acc = _dot(refs[0][...], refs[n_seg][...])
    for s in range(1, n_seg):
        acc = acc + _dot(refs[s][...], refs[n_seg + s][...])
    o_ref[...] = acc.astype(o_ref.dtype)
    _run_side_casts(pl.program_id(0) * pl.num_programs(1) + pl.program_id(1),
                    side_in, side_out, side_blocks)


def matmul(a_list, w_list, out_dtype, tm_pref=1024, tn_pref=1024, name="matmul", side=()):
    m = a_list[0].shape[0]
    n = w_list[0].shape[1]
    tm = _pick(m, tm_pref)
    tn = _pick(n, tn_pref)
    n_seg = len(a_list)
    n_j = n // tn
    in_specs = [pl.BlockSpec((tm, a.shape[1]), lambda i, j: (i, 0)) for a in a_list]
    in_specs += [pl.BlockSpec((w.shape[0], tn), lambda i, j: (0, j)) for w in w_list]
    plans = [_side_cast_plan(w, (m // tm) * n_j, lambda i, j: i * n_j + j) for w in side]
    outs = pl.pallas_call(
        functools.partial(_mm_kernel, n_seg=n_seg, side_blocks=tuple(nb for _, nb in plans)),
        grid=(m // tm, n_j),
        in_specs=in_specs + [spec for spec, _ in plans],
        out_specs=[pl.BlockSpec((tm, tn), lambda i, j: (i, j))] + [spec for spec, _ in plans],
        out_shape=[jax.ShapeDtypeStruct((m, n), out_dtype)]
        + [jax.ShapeDtypeStruct(w.shape, BF16) for w in side],
        compiler_params=_params("arbitrary", "arbitrary"),
        name=name,
    )(*a_list, *w_list, *side)
    return outs[0], tuple(outs[1:])


ROUTE_E1, ROUTE_E2, ROUTE_W1, ROUTE_W2 = 0, 1, 2, 3


def _top2_route(logits, n_experts):
    lane = lax.broadcasted_iota(jnp.int32, logits.shape, 1)
    neg = jnp.float32(-jnp.inf)
    lg = jnp.where(lane < n_experts, logits, neg)
    v1 = jnp.max(lg, axis=-1, keepdims=True)
    i1 = jnp.min(jnp.where(lg == v1, lane, LANE), axis=-1, keepdims=True)
    lg2 = jnp.where(lane == i1, neg, lg)
    v2 = jnp.max(lg2, axis=-1, keepdims=True)
    i2 = jnp.min(jnp.where(lg2 == v2, lane, LANE), axis=-1, keepdims=True)
    e2 = jnp.exp(v2 - v1)
    w1 = 1.0 / (1.0 + e2)
    w2 = e2 / (1.0 + e2)
    rec = jnp.where(lane == ROUTE_E1, i1.astype(F32), 0.0)
    rec = jnp.where(lane == ROUTE_E2, i2.astype(F32), rec)
    rec = jnp.where(lane == ROUTE_W1, w1, rec)
    return jnp.where(lane == ROUTE_W2, w2, rec)


def _pack_halves(x):
    half = x.shape[1] // 2
    lo = lax.bitcast_convert_type(x[:, :half].astype(BF16).astype(F32), jnp.uint32)
    hi = lax.bitcast_convert_type(x[:, half:].astype(BF16).astype(F32), jnp.uint32)
    return hi | (lo >> 16)


def _unpack_halves(w):
    lo = lax.bitcast_convert_type(w << 16, F32)
    hi = lax.bitcast_convert_type(w & jnp.uint32(0xFFFF0000), F32)
    return lo, hi


def _epilogue_kernel(*refs, has_next, n_experts):
    y_ref, x_ref, post_g_ref, gate_ref = refs[:4]
    pos = 4
    if has_next:
        pre_g_ref, sh_ref, sc_ref = refs[pos:pos + 3]
        pos += 3
    if n_experts:
        rw_ref, rb_ref = refs[pos:pos + 2]
        pos += 2
    outs = refs[pos:]
    x1 = x_ref[0] + gate_ref[0] * _rms(y_ref[0].astype(F32), post_g_ref[...])
    outs[0][0] = x1
    if has_next:
        xf = _rms(x1, pre_g_ref[...]) * (1.0 + sc_ref[0]) + sh_ref[0]
        outs[1][0] = _pack_halves(xf) if n_experts else xf.astype(outs[1].dtype)
        if n_experts:
            logits = jnp.dot(xf, rw_ref[...], preferred_element_type=F32,
                             precision=lax.Precision.HIGHEST) + rb_ref[...]
            outs[2][0] = _top2_route(logits, n_experts)


def sublayer_epilogue(y, x, post_g, gate, nxt=None, router=None):
    bsz, s, d = x.shape
    tm = _pick(s, 256)
    blk = pl.BlockSpec((1, tm, d), lambda b, i: (b, i, 0))
    row = pl.BlockSpec((1, d), lambda b, i: (0, 0))
    vec = pl.BlockSpec((1, 1, d), lambda b, i: (b, 0, 0))
    args = [y, x, post_g.reshape(1, d), gate.reshape(bsz, 1, d)]
    in_specs = [blk, blk, row, vec]
    out_shape = [jax.ShapeDtypeStruct((bsz, s, d), F32)]
    out_specs = [blk]
    n_experts = 0
    if nxt is not None:
        pre_g, shift, scale = nxt
        args += [pre_g.reshape(1, d), shift.reshape(bsz, 1, d), scale.reshape(bsz, 1, d)]
        in_specs += [row, vec, vec]
        if router is None:
            out_shape.append(jax.ShapeDtypeStruct((bsz, s, d), BF16))
            out_specs.append(blk)
        else:
            out_shape.append(jax.ShapeDtypeStruct((bsz, s, d // 2), jnp.uint32))
            out_specs.append(pl.BlockSpec((1, tm, d // 2), lambda b, i: (b, i, 0)))
        if router is not None:
            router_w, router_b = router
            n_experts = router_w.shape[1]
            rw = jnp.zeros((d, LANE), F32).at[:, :n_experts].set(router_w)
            rb = jnp.zeros((1, LANE), F32).at[0, :n_experts].set(router_b)
            args += [rw, rb]
            in_specs += [pl.BlockSpec((d, LANE), lambda b, i: (0, 0)),
                         pl.BlockSpec((1, LANE), lambda b, i: (0, 0))]
            out_shape.append(jax.ShapeDtypeStruct((bsz, s, LANE), F32))
            out_specs.append(pl.BlockSpec((1, tm, LANE), lambda b, i: (b, i, 0)))
    return pl.pallas_call(
        functools.partial(_epilogue_kernel, has_next=nxt is not None, n_experts=n_experts),
        grid=(bsz, s // tm),
        in_specs=in_specs,
        out_specs=out_specs,
        out_shape=out_shape,
        compiler_params=_params("arbitrary", "arbitrary"),
        name="sublayer_epilogue",
    )(*args)


def _ffn_kernel(*refs, side_blocks):
    n_side = len(side_blocks)
    x_ref, wg_ref, wu_ref, wd_ref = refs[:4]
    side_in = refs[4:4 + n_side]
    o_ref = refs[4 + n_side]
    side_out = refs[5 + n_side:5 + 2 * n_side]
    acc_ref = refs[5 + 2 * n_side]
    j = pl.program_id(1)

    @pl.when(j == 0)
    def _():
        acc_ref[...] = jnp.zeros_like(acc_ref)

    x = x_ref[...]
    hg = _dot(x, wg_ref[...])
    hu = _dot(x, wu_ref[...])
    h = (hg * jax.nn.sigmoid(hg) * hu).astype(BF16)
    acc_ref[...] += _dot(h, wd_ref[...])

    @pl.when(j == pl.num_programs(1) - 1)
    def _():
        o_ref[...] = acc_ref[...].astype(o_ref.dtype)

    _run_side_casts(pl.program_id(0) * pl.num_programs(1) + j, side_in, side_out, side_blocks)


def dense_swiglu(x, w_gate, w_up, w_down, side=()):
    m, d = x.shape
    f = w_gate.shape[1]
    tm = _pick(m, 512)
    tf = _pick(f, 512)
    n_j = f // tf
    plans = [_side_cast_plan(w, (m // tm) * n_j, lambda i, j: i * n_j + j) for w in side]
    outs = pl.pallas_call(
        functools.partial(_ffn_kernel, side_blocks=tuple(nb for _, nb in plans)),
        grid=(m // tm, n_j),
        in_specs=[pl.BlockSpec((tm, d), lambda i, j: (i, 0)),
                  pl.BlockSpec((d, tf), lambda i, j: (0, j)),
                  pl.BlockSpec((d, tf), lambda i, j: (0, j)),
                  pl.BlockSpec((tf, d), lambda i, j: (j, 0))] + [spec for spec, _ in plans],
        out_specs=[pl.BlockSpec((tm, d), lambda i, j: (i, 0))] + [spec for spec, _ in plans],
        out_shape=[jax.ShapeDtypeStruct((m, d), BF16)]
        + [jax.ShapeDtypeStruct(w.shape, BF16) for w in side],
        scratch_shapes=[pltpu.VMEM((tm, d), F32)],
        compiler_params=_params("arbitrary", "arbitrary"),
        name="dense_swiglu",
    )(x, w_gate, w_up, w_down, *side)
    return outs[0], tuple(outs[1:])


MOE_ROW_TILE = 512
DMA_ROWS = 256


def _rank_kernel(route_ref, tri_ref, rank_ref, count_ref, carry_ref):
    @pl.when(pl.program_id(0) == 0)
    def _():
        carry_ref[...] = jnp.zeros_like(carry_ref)

    route = route_ref[...]
    lane = lax.broadcasted_iota(jnp.int32, route.shape, 1)
    lane_f = lane.astype(F32)
    oh1 = lane_f == route[:, ROUTE_E1:ROUTE_E1 + 1]
    oh2 = lane_f == route[:, ROUTE_E2:ROUTE_E2 + 1]
    cnt = jnp.where(oh1 | oh2, 1.0, 0.0)
    before = _dot(tri_ref[...], cnt.astype(BF16)) + carry_ref[0:1, :]
    r1 = jnp.sum(jnp.where(oh1, before, 0.0), axis=-1, keepdims=True)
    r2 = jnp.sum(jnp.where(oh2, before, 0.0), axis=-1, keepdims=True)
    rank_ref[...] = jnp.where(lane == ROUTE_E1, r1, jnp.where(lane == ROUTE_E2, r2, 0.0))
    total = carry_ref[0:1, :] + jnp.sum(cnt, axis=0, keepdims=True)
    carry_ref[0:1, :] = total
    count_ref[...] = jnp.broadcast_to(total, count_ref.shape)


def expert_ranks(route):
    t = route.shape[0]
    tm = _pick(t, 256)
    ti = lax.broadcasted_iota(jnp.int32, (tm, tm), 0)
    tj = lax.broadcasted_iota(jnp.int32, (tm, tm), 1)
    tri = (ti > tj).astype(BF16)
    return pl.pallas_call(
        _rank_kernel,
        grid=(t // tm,),
        in_specs=[pl.BlockSpec((tm, LANE), lambda i: (i, 0)),
                  pl.BlockSpec((tm, tm), lambda i: (0, 0))],
        out_specs=[pl.BlockSpec((tm, LANE), lambda i: (i, 0)),
                   pl.BlockSpec((SUBLANE, LANE), lambda i: (0, 0))],
        out_shape=[jax.ShapeDtypeStruct((t, LANE), F32),
                   jax.ShapeDtypeStruct((SUBLANE, LANE), F32)],
        scratch_shapes=[pltpu.VMEM((SUBLANE, LANE), F32)],
        compiler_params=_params("arbitrary"),
        name="expert_ranks",
    )(route, tri)


def _row_copies(n_rows, make_copy):
    def start(r, carry):
        for slot, cp in enumerate(make_copy(r)):
            cp.start(priority=slot % 2)
        return carry

    def wait(r, carry):
        for cp in make_copy(r):
            cp.wait()
        return carry

    lax.fori_loop(0, n_rows, start, 0)
    lax.fori_loop(0, n_rows, wait, 0)


def _dispatch_kernel(pos1_ref, pos2_ref, x_ref, init_ref, o_ref, sem):
    del init_ref

    def make_copy(r):
        src = x_ref.at[pl.ds(r, 1)]
        return [pltpu.make_async_copy(src, o_ref.at[pl.ds(pos_ref[0, r], 1)], sem)
                for pos_ref in (pos1_ref, pos2_ref)]

    _row_copies(x_ref.shape[0], make_copy)


def moe_dispatch(x, pos1, pos2, n_rows):
    t, d = x.shape
    rows = pos1.shape[2]
    smem = pl.BlockSpec((None, 1, rows), lambda i: (i, 0, 0), memory_space=pltpu.SMEM)
    return pl.pallas_call(
        _dispatch_kernel,
        grid=(t // rows,),
        in_specs=[smem, smem, pl.BlockSpec((rows, d), lambda i: (i, 0)),
                  pl.BlockSpec(memory_space=pl.ANY)],
        out_specs=pl.BlockSpec(memory_space=pl.ANY),
        out_shape=jax.ShapeDtypeStruct((n_rows, d), x.dtype),
        scratch_shapes=[pltpu.SemaphoreType.DMA(())],
        input_output_aliases={3: 0},
        compiler_params=_params("arbitrary"),
        name="moe_dispatch",
    )(pos1, pos2, x, jnp.zeros((n_rows, d), x.dtype))


def _moe_kernel(tile_expert_ref, n_used_ref, x_ref, wg_ref, wu_ref, wd_ref, o_ref,
                xlo_ref, xhi_ref, acc_ref):
    del tile_expert_ref
    k = pl.program_id(1)
    half = xlo_ref.shape[1]

    @pl.when(k == 0)
    def _():
        acc_ref[...] = jnp.zeros_like(acc_ref)
        lo, hi = _unpack_halves(x_ref[...])
        xlo_ref[...] = lo.astype(BF16)
        xhi_ref[...] = hi.astype(BF16)

    @pl.when(pl.program_id(0) < n_used_ref[0])
    def _():
        xlo = xlo_ref[...]
        xhi = xhi_ref[...]
        hg = _dot(xlo, wg_ref[:half, :]) + _dot(xhi, wg_ref[half:, :])
        hu = _dot(xlo, wu_ref[:half, :]) + _dot(xhi, wu_ref[half:, :])
        h = (hg * jax.nn.sigmoid(hg) * hu).astype(BF16)
        acc_ref[...] += _dot(h, wd_ref[...])

    @pl.when(k == pl.num_programs(1) - 1)
    def _():
        o_ref[...] = _pack_halves(acc_ref[...])


def moe_grouped_swiglu(xs, tile_expert, n_used, w_gate, w_up, w_down):
    n_rows, half = xs.shape
    d = 2 * half
    f = w_gate.shape[2]
    tf = _pick(f, 512)
    live = lambda j, k, te, nu: jnp.where(j < nu[0], k, 0)
    grid_spec = pltpu.PrefetchScalarGridSpec(
        num_scalar_prefetch=2,
        grid=(n_rows // MOE_ROW_TILE, f // tf),
        in_specs=[pl.BlockSpec((MOE_ROW_TILE, half), lambda j, k, te, nu: (j, 0)),
                  pl.BlockSpec((None, d, tf), lambda j, k, te, nu: (te[j], 0, live(j, k, te, nu))),
                  pl.BlockSpec((None, d, tf), lambda j, k, te, nu: (te[j], 0, live(j, k, te, nu))),
                  pl.BlockSpec((None, tf, d), lambda j, k, te, nu: (te[j], live(j, k, te, nu), 0))],
        out_specs=pl.BlockSpec((MOE_ROW_TILE, half), lambda j, k, te, nu: (j, 0)),
        scratch_shapes=[pltpu.VMEM((MOE_ROW_TILE, half), BF16),
                        pltpu.VMEM((MOE_ROW_TILE, half), BF16),
                        pltpu.VMEM((MOE_ROW_TILE, d), F32)])
    return pl.pallas_call(
        _moe_kernel,
        grid_spec=grid_spec,
        out_shape=jax.ShapeDtypeStruct((n_rows, half), jnp.uint32),
        compiler_params=_params("arbitrary", "arbitrary"),
        name="moe_grouped_swiglu",
    )(tile_expert, n_used, xs, w_gate, w_up, w_down)


def _combine_kernel(pos1_ref, pos2_ref, route_ref, x_ref, post_g_ref, gate_ref, ys_ref, o_ref,
                    buf1, buf2, sem):
    def make_copy(r):
        return [pltpu.make_async_copy(ys_ref.at[pl.ds(pos_ref[0, r], 1)], buf.at[pl.ds(r, 1)], sem)
                for pos_ref, buf in ((pos1_ref, buf1), (pos2_ref, buf2))]

    _row_copies(buf1.shape[0], make_copy)
    route = route_ref[0]
    w1 = route[:, ROUTE_W1:ROUTE_W1 + 1]
    w2 = route[:, ROUTE_W2:ROUTE_W2 + 1]
    lo1, hi1 = _unpack_halves(buf1[...])
    lo2, hi2 = _unpack_halves(buf2[...])
    y = jnp.concatenate([w1 * lo1 + w2 * lo2, w1 * hi1 + w2 * hi2], axis=1)
    o_ref[0] = x_ref[0] + gate_ref[0] * _rms(y, post_g_ref[...])


def moe_combine_epilogue(ys, pos1, pos2, route, x, post_g, gate):
    bsz, s, d = x.shape
    rows = pos1.shape[2]
    n_i = s // rows
    smem = pl.BlockSpec((None, 1, rows), lambda b, i: (b * n_i + i, 0, 0),
                        memory_space=pltpu.SMEM)
    blk = pl.BlockSpec((1, rows, d), lambda b, i: (b, i, 0))
    return pl.pallas_call(
        _combine_kernel,
        grid=(bsz, n_i),
        in_specs=[smem, smem, pl.BlockSpec((1, rows, LANE), lambda b, i: (b, i, 0)), blk,
                  pl.BlockSpec((1, d), lambda b, i: (0, 0)),
                  pl.BlockSpec((1, 1, d), lambda b, i: (b, 0, 0)),
                  pl.BlockSpec(memory_space=pl.ANY)],
        out_specs=blk,
        out_shape=jax.ShapeDtypeStruct((bsz, s, d), F32),
        scratch_shapes=[pltpu.VMEM((rows, d // 2), jnp.uint32),
                        pltpu.VMEM((rows, d // 2), jnp.uint32),
                        pltpu.SemaphoreType.DMA(())],
        compiler_params=_params("arbitrary", "arbitrary"),
        name="moe_combine_epilogue",
    )(pos1, pos2, route, x, post_g.reshape(1, d), gate.reshape(bsz, 1, d), ys)


def moe_sublayer(xf, route, x, post_g, gate, w_gate, w_up, w_down):
    bsz, s, d = x.shape
    t = bsz * s
    n_e = w_gate.shape[0]
    rows = _pick(s, DMA_ROWS)
    assert (TOP_K * t) % MOE_ROW_TILE == 0
    n_tiles = TOP_K * t // MOE_ROW_TILE + n_e
    route2 = route.reshape(t, LANE)
    rank, count = expert_ranks(route2)
    counts = count[0, :n_e].astype(jnp.int32)
    tiles_e = (counts + MOE_ROW_TILE - 1) // MOE_ROW_TILE
    tile_end = jnp.cumsum(tiles_e)
    row_start = (tile_end - tiles_e) * MOE_ROW_TILE
    e1 = route2[:, ROUTE_E1].astype(jnp.int32)
    e2 = route2[:, ROUTE_E2].astype(jnp.int32)
    pos1 = (row_start[e1] + rank[:, ROUTE_E1].astype(jnp.int32)).reshape(t // rows, 1, rows)
    pos2 = (row_start[e2] + rank[:, ROUTE_E2].astype(jnp.int32)).reshape(t // rows, 1, rows)
    tile_expert = jnp.minimum(jnp.searchsorted(tile_end, jnp.arange(n_tiles), side='right'),
                              n_e - 1).astype(jnp.int32)
    n_used = tile_end[n_e - 1:].astype(jnp.int32)
    xs = moe_dispatch(xf.reshape(t, d // 2), pos1, pos2, n_tiles * MOE_ROW_TILE)
    ys = moe_grouped_swiglu(xs, tile_expert, n_used, w_gate, w_up, w_down)
    return moe_combine_epilogue(ys, pos1, pos2, route, x, post_g, gate)


def _shift_rows(x, carry_row):
    rolled = pltpu.roll(x, 1, axis=0)
    row = lax.broadcasted_iota(jnp.int32, x.shape, 0)
    return jnp.where(row == 0, carry_row, rolled)


def _head_sums(x, seg):
    parts = [_dot(x[:, c:c + GROUP_LANES].astype(BF16), seg)
             for c in range(0, x.shape[1], GROUP_LANES)]
    return parts[0] if len(parts) == 1 else jnp.concatenate(parts, axis=1)


def _prep_kernel(*refs, has_vres, tm, low):
    (r_ref, k_ref, v_ref, p_ref) = refs[:4]
    pos = 4
    if has_vres:
        vfirst_ref = refs[pos]
        pos += 1
    (mu_ref, w0_ref, w2_ref, a0_ref, a2_ref, g2_ref) = refs[pos:pos + 6]
    pos += 6
    if has_vres:
        v0_ref, v2_ref = refs[pos:pos + 2]
        pos += 2
    (kk_ref, ka_ref, rk_ref, seg_ref, tri_ref) = refs[pos:pos + 5]
    pos += 5
    (rt_ref, kt_ref, at_ref, bt_ref, vb_ref, g_ref, bonus_ref, wl_ref) = refs[pos:pos + 8]
    pos += 8
    if not has_vres:
        vf_ref = refs[pos]
        pos += 1
    carry_rkv, carry_p = refs[pos:pos + 2]
    rw = r_ref.shape[-1]
    n_low = p_ref.shape[-1] // 2

    @pl.when(pl.program_id(1) == 0)
    def _():
        carry_rkv[...] = jnp.zeros_like(carry_rkv)
        carry_p[...] = jnp.zeros_like(carry_p)

    def lerp_prev(ref, idx):
        cur = ref[0].astype(F32)
        prev = _shift_rows(cur, carry_rkv[0:1, idx * rw:(idx + 1) * rw])
        carry_rkv[0:1, idx * rw:(idx + 1) * rw] = cur[tm - 1:tm, :]
        return cur + (prev - cur) * mu_ref[idx:idx + 1, :]

    r = lerp_prev(r_ref, 0)
    k = lerp_prev(k_ref, 1)
    v = lerp_prev(v_ref, 2)

    p = p_ref[0].astype(F32)
    p_b = p[:, n_low:]
    lowr = p[:, :n_low] + _shift_rows(p_b, carry_p[0:1, :])
    carry_p[0:1, :] = p_b[tm - 1:tm, :]
    o_w, o_a, o_g, o_v = low

    def low_slice(o):
        return lowr[:, o[0]:o[0] + o[1]]

    zw = w0_ref[...] + _dot(jnp.tanh(low_slice(o_w)).astype(BF16), w2_ref[...])
    logw = -jnp.exp(jnp.float32(-0.5)) * jax.nn.sigmoid(zw)
    asig = jax.nn.sigmoid(a0_ref[...] + _dot(low_slice(o_a).astype(BF16), a2_ref[...]))
    g_ref[0] = _dot(jax.nn.sigmoid(low_slice(o_g)).astype(BF16), g2_ref[...]).astype(g_ref.dtype)
    if has_vres:
        vgate = jax.nn.sigmoid(v0_ref[...] + _dot(low_slice(o_v).astype(BF16), v2_ref[...]))
        v = v + (vfirst_ref[0] - v) * vgate
    else:
        vf_ref[0] = v

    seg = seg_ref[...]
    kk = k * kk_ref[...]
    kk = kk / jnp.maximum(jnp.sqrt(_head_sums(kk * kk, seg)), 1e-12)
    k = k * (1.0 + (asig - 1.0) * ka_ref[...])
    bonus_ref[0] = (_head_sums(r * k * rk_ref[...], seg) * v).astype(bonus_ref.dtype)

    cum = jnp.dot(tri_ref[...], logw, preferred_element_type=F32, precision=lax.Precision.HIGHEST)
    e_pos = jnp.exp(cum)
    e_neg = jnp.exp(-cum)
    rt_ref[0] = (r * e_pos).astype(BF16)
    kt_ref[0] = (k * e_neg).astype(BF16)
    bt_ref[0] = (kk * asig * e_neg).astype(BF16)
    at_ref[0] = (-kk * jnp.exp(cum - logw)).astype(BF16)
    vb_ref[0] = v.astype(BF16)
    ends = [cum[c * CHUNK + CHUNK - 1:c * CHUNK + CHUNK, :] for c in range(tm // CHUNK)]
    if tm // CHUNK < SUBLANE:
        ends.append(jnp.zeros((SUBLANE - tm // CHUNK, rw), F32))
    wl_ref[0, 0] = jnp.exp(jnp.concatenate(ends, axis=0))


def rwkv_prep(h3, rw, low, low_blk, lp, vfirst, tm):
    bsz, s, cols = h3.shape
    n_low2 = cols // (low_blk + 1)
    assert n_low2 * (low_blk + 1) == cols and tm % CHUNK == 0 and tm // CHUNK <= SUBLANE
    has_vres = vfirst is not None
    blk = lambda j: pl.BlockSpec((1, tm, rw), lambda b, i, j=j: (b, i, j))
    full = lambda a: pl.BlockSpec(a.shape, lambda b, i: (0,) * a.ndim)
    row = lambda a: a.reshape(1, -1)
    args = [h3, h3, h3, h3]
    in_specs = [blk(0), blk(1), blk(2),
                pl.BlockSpec((1, tm, n_low2), lambda b, i: (b, i, low_blk))]
    if has_vres:
        args.append(vfirst)
        in_specs.append(blk(0))
    seg = (lax.broadcasted_iota(jnp.int32, (GROUP_LANES, GROUP_LANES), 0) // HEAD ==
           lax.broadcasted_iota(jnp.int32, (GROUP_LANES, GROUP_LANES), 1) // HEAD).astype(BF16)
    ti = lax.broadcasted_iota(jnp.int32, (tm, tm), 0)
    tj = lax.broadcasted_iota(jnp.int32, (tm, tm), 1)
    tri = ((ti >= tj) & (ti // CHUNK == tj // CHUNK)).astype(F32)
    small = [lp['mu_rkv'].reshape(3, rw), row(lp['w0']), lp['w2p'], row(lp['a0']), lp['a2p'],
             lp['g2p']]
    if has_vres:
        small += [row(lp['v0']), lp['v2p']]
    small += [row(lp['k_k']), row(lp['k_a']), row(lp['r_k']), seg, tri]
    args += small
    in_specs += [full(a) for a in small]
    tok = lambda dt: jax.ShapeDtypeStruct((bsz, s, rw), dt)
    out_shape = [tok(BF16)] * 7 + [jax.ShapeDtypeStruct((bsz, s // tm, SUBLANE, rw), F32)]
    out_specs = [blk(0)] * 7 + [pl.BlockSpec((1, 1, SUBLANE, rw), lambda b, i: (b, i, 0, 0))]
    if not has_vres:
        out_shape.append(tok(F32))
        out_specs.append(blk(0))
    return pl.pallas_call(
        functools.partial(_prep_kernel, has_vres=has_vres, tm=tm, low=low),
        grid=(bsz, s // tm),
        in_specs=in_specs,
        out_specs=out_specs,
        out_shape=out_shape,
        scratch_shapes=[pltpu.VMEM((SUBLANE, 3 * rw), F32), pltpu.VMEM((SUBLANE, n_low2 // 2), F32)],
        compiler_params=_params("arbitrary", "arbitrary"),
        name="rwkv_prep",
    )(*args)


def _expand_heads(x, head_mask):
    return jnp.where(head_mask, jnp.concatenate([x] * HEADS_PER_GROUP, axis=0),
                     jnp.zeros((), x.dtype))


def _scan_kernel(*refs, n_chunks, n_groups, side_blocks):
    n_side = len(side_blocks)
    (rt_ref, kt_ref, at_ref, bt_ref, v_ref, g_ref, bonus_ref, wl_ref, lng_ref, lnb_ref) = refs[:10]
    side_in = refs[10:10 + n_side]
    o_ref = refs[10 + n_side]
    side_out = refs[11 + n_side:11 + 2 * n_side]
    state_ref = refs[11 + 2 * n_side]
    _scan_body(rt_ref, kt_ref, at_ref, bt_ref, v_ref, g_ref, bonus_ref, wl_ref, lng_ref, lnb_ref,
               o_ref, state_ref, n_chunks=n_chunks, n_groups=n_groups)
    step = ((pl.program_id(0) * pl.num_programs(1) + pl.program_id(1)) * pl.num_programs(2)
            + pl.program_id(2))
    _run_side_casts(step, side_in, side_out, side_blocks)


def _scan_body(rt_ref, kt_ref, at_ref, bt_ref, v_ref, g_ref, bonus_ref, wl_ref,
               lng_ref, lnb_ref, o_ref, state_ref, *, n_chunks, n_groups):
    rows = HEADS_PER_GROUP * CHUNK
    t_i = lax.broadcasted_iota(jnp.int32, (CHUNK, rows), 0)
    s_i = lax.broadcasted_iota(jnp.int32, (CHUNK, rows), 1) % CHUNK
    strict = t_i > s_i
    incl = t_i >= s_i
    eye = (t_i == s_i).astype(F32)
    ri = lax.broadcasted_iota(jnp.int32, (rows, GROUP_LANES), 0)
    ci = lax.broadcasted_iota(jnp.int32, (rows, GROUP_LANES), 1)
    head_mask = ri // CHUNK == ci // HEAD
    gi = lax.broadcasted_iota(jnp.int32, (GROUP_LANES, GROUP_LANES), 0)
    gj = lax.broadcasted_iota(jnp.int32, (GROUP_LANES, GROUP_LANES), 1)
    same_head = gi // HEAD == gj // HEAD
    avg = jnp.where(same_head, 1.0 / HEAD, 0.0).astype(BF16)
    nt = (((1,), (1,)), ((), ()))
    tn = (((0,), (0,)), ((), ()))
    bd = lambda x: _expand_heads(x.astype(BF16), head_mask)

    @pl.when(pl.program_id(2) == 0)
    def _():
        state_ref[...] = jnp.zeros_like(state_ref)

    chains = [(c, g) for c in range(n_chunks) for g in range(n_groups)]

    def window(ref, c, g):
        return ref[0, c * CHUNK:(c + 1) * CHUNK, g * GROUP_LANES:(g + 1) * GROUP_LANES]

    ops, a_ab, a_ak, m_rbk, t_inv, pw = {}, {}, {}, {}, {}, {}
    for ch in chains:
        a_t, r_t, b_t, k_t, v = (window(ref, *ch) for ref in (at_ref, rt_ref, bt_ref, kt_ref, v_ref))
        ops[ch] = (a_t, r_t, jnp.concatenate([b_t, k_t], axis=0), v, bd(v))
        p = lax.dot_general(jnp.concatenate([a_t, r_t], axis=0),
                            jnp.concatenate([bd(b_t), bd(k_t)], axis=0), nt,
                            preferred_element_type=F32)
        a_ab[ch] = jnp.where(strict, p[:CHUNK, :rows], 0.0)
        a_ak[ch] = jnp.where(strict, p[:CHUNK, rows:], 0.0).astype(BF16)
        m_rbk[ch] = jnp.where(jnp.concatenate([incl, incl], axis=1), p[CHUNK:], 0.0).astype(BF16)
    for ch in chains:
        t_inv[ch] = eye + a_ab[ch]
        pw[ch] = _dot(a_ab[ch].astype(BF16), bd(a_ab[ch]))
    step = 2
    while 2 * step < CHUNK:
        for ch in chains:
            both = _dot(jnp.concatenate([t_inv[ch], pw[ch]], axis=0).astype(BF16), bd(pw[ch]))
            t_inv[ch] = t_inv[ch] + both[:CHUNK]
            pw[ch] = both[CHUNK:]
        step *= 2
    taw = {}
    for ch in chains:
        t_fin = (t_inv[ch] + _dot(t_inv[ch].astype(BF16), bd(pw[ch]))).astype(BF16)
        a_t, _, _, _, v_bd = ops[ch]
        av = _dot(a_ak[ch], v_bd)
        taw[ch] = _dot(t_fin, jnp.concatenate([bd(a_t), bd(av)], axis=1))

    ys = {}
    for c in range(n_chunks):
        for g in range(n_groups):
            _, r_t, bk, v, v_bd = ops[c, g]
            lanes = slice(g * GROUP_LANES, (g + 1) * GROUP_LANES)
            state = state_ref[g]
            ars = lax.dot_general(
                jnp.concatenate([taw[c, g][:, :GROUP_LANES].astype(BF16), r_t], axis=0),
                state.astype(BF16), nt, preferred_element_type=F32)
            u = (ars[:CHUNK] + taw[c, g][:, GROUP_LANES:]).astype(BF16)
            upd = lax.dot_general(jnp.concatenate([u, v], axis=0), bk, tn,
                                  preferred_element_type=F32)
            state_ref[g] = (state + jnp.where(same_head, upd, 0.0)) * wl_ref[0, 0, c:c + 1, lanes]
            ys[c, g] = ars[CHUNK:] + _dot(m_rbk[c, g], jnp.concatenate([bd(u), v_bd], axis=0))
    for g in range(n_groups):
        lanes = slice(g * GROUP_LANES, (g + 1) * GROUP_LANES)
        y = jnp.concatenate([ys[c, g] for c in range(n_chunks)], axis=0)
        dlt = y - _dot(y.astype(BF16), avg)
        var = _dot((dlt * dlt).astype(BF16), avg)
        yn = dlt * lax.rsqrt(var + GN_EPS) * lng_ref[:, lanes] + lnb_ref[:, lanes]
        o_ref[0, :, lanes] = ((yn + bonus_ref[0, :, lanes].astype(F32))
                              * g_ref[0, :, lanes].astype(F32)).astype(o_ref.dtype)


def wkv_scan(rt, kt, at, bt, vb, g, bonus, wl, lnx_g, lnx_b, tm, side=()):
    bsz, s, rw = rt.shape
    n_groups = SCAN_GROUPS if rw % (SCAN_GROUPS * GROUP_LANES) == 0 else 1
    width = n_groups * GROUP_LANES
    n_hg, n_i = rw // width, s // tm
    blk = pl.BlockSpec((1, tm, width), lambda b, hg, i: (b, i, hg))
    vec = pl.BlockSpec((1, width), lambda b, hg, i: (0, hg))
    plans = [_side_cast_plan(w, bsz * n_hg * n_i, lambda b, hg, i: (b * n_hg + hg) * n_i + i)
             for w in side]
    outs = pl.pallas_call(
        functools.partial(_scan_kernel, n_chunks=tm // CHUNK, n_groups=n_groups,
                          side_blocks=tuple(nb for _, nb in plans)),
        grid=(bsz, n_hg, n_i),
        in_specs=[blk] * 7 + [pl.BlockSpec((1, 1, SUBLANE, width),
                                           lambda b, hg, i: (b, i, 0, hg)), vec, vec]
        + [spec for spec, _ in plans],
        out_specs=[blk] + [spec for spec, _ in plans],
        out_shape=[jax.ShapeDtypeStruct((bsz, s, rw), BF16)]
        + [jax.ShapeDtypeStruct(w.shape, BF16) for w in side],
        scratch_shapes=[pltpu.VMEM((n_groups, GROUP_LANES, GROUP_LANES), F32)],
        compiler_params=_params("arbitrary", "arbitrary", "arbitrary"),
        name="wkv_scan",
    )(rt, kt, at, bt, vb, g, bonus, wl, lnx_g.reshape(1, rw), lnx_b.reshape(1, rw), *side)
    return outs[0], tuple(outs[1:])


CONV_ROWS = 64
CONV_LANES = 256


def _conv_kernel(val_ref, gate_ref, w_ref, b_ref, lg_ref, lb_ref, o_ref, win_ref, y_ref,
                 *, tm, taps, halo):
    keep = halo + SUBLANE

    @pl.when(pl.program_id(1) == 0)
    def _():
        win_ref[:, 0:keep, :] = jnp.zeros((SUBLANE, keep, win_ref.shape[2]), F32)

    @pl.when(pl.program_id(1) != 0)
    def _():
        win_ref[:, 0:keep, :] = win_ref[:, tm:tm + keep, :]

    u = val_ref[0].astype(F32) * jax.nn.sigmoid(gate_ref[0].astype(F32))
    for r in range(SUBLANE):
        win_ref[r, halo + r:halo + r + tm, :] = u
    for c0 in range(0, u.shape[1], CONV_LANES):
        lanes = slice(c0, min(c0 + CONV_LANES, u.shape[1]))
        for r0 in range(0, tm, CONV_ROWS):
            n = min(CONV_ROWS, tm - r0)
            acc = jnp.broadcast_to(b_ref[:, lanes], (n, lanes.stop - lanes.start))
            for j in range(taps):
                q, r = divmod(taps - 1 - j, SUBLANE)
                off = halo - SUBLANE * q + r0
                acc = acc + w_ref[j:j + 1, lanes] * win_ref[r, off:off + n, lanes]
            y_ref[r0:r0 + n, lanes] = acc
    acc = y_ref[...]
    mu = jnp.mean(acc, axis=-1, keepdims=True)
    d = acc - mu
    var = jnp.mean(d * d, axis=-1, keepdims=True)
    z = d * lax.rsqrt(var + LN_EPS) * lg_ref[...] + lb_ref[...]
    o_ref[0] = (z * jax.nn.sigmoid(z)).astype(o_ref.dtype)


def conv_group(h3, rw, cw, conv_w, conv_b, ln_g, ln_b):
    bsz, s, _ = h3.shape
    taps = conv_w.shape[0]
    halo = _round_up(taps - 1, SUBLANE)
    tm = _pick(s, 128)
    assert (3 * rw) % cw == 0 and tm >= halo + SUBLANE
    c0 = (3 * rw) // cw
    row = pl.BlockSpec((1, cw), lambda b, i: (0, 0))
    return pl.pallas_call(
        functools.partial(_conv_kernel, tm=tm, taps=taps, halo=halo),
        grid=(bsz, s // tm),
        in_specs=[pl.BlockSpec((1, tm, cw), lambda b, i: (b, i, c0)),
                  pl.BlockSpec((1, tm, cw), lambda b, i: (b, i, c0 + 1)),
                  pl.BlockSpec((taps, cw), lambda b, i: (0, 0)), row, row, row],
        out_specs=pl.BlockSpec((1, tm, cw), lambda b, i: (b, i, 0)),
        out_shape=jax.ShapeDtypeStruct((bsz, s, cw), BF16),
        scratch_shapes=[pltpu.VMEM((SUBLANE, halo + tm + SUBLANE, cw), F32),
                        pltpu.VMEM((tm, cw), F32)],
        compiler_params=_params("arbitrary", "arbitrary"),
        name="conv_group",
    )(h3, h3, conv_w, conv_b.reshape(1, cw), ln_g.reshape(1, cw), ln_b.reshape(1, cw))


def _pad_cols(w, n):
    return jnp.pad(w, ((0, 0), (0, n - w.shape[1])))


def _pad_rows(w, n):
    return jnp.pad(w, ((0, n - w.shape[0]), (0, 0)))


def _mixer_weights(lp, has_vres):
    names = [('w', 'mu_w', 'w1', 'w2'), ('a', 'mu_a', 'a1', 'a2'), ('g', 'mu_g', 'g1', 'g2')]
    if has_vres:
        names.append(('v', 'mu_v', 'v1', 'v2'))
    cur, prev, low, off = [], [], [], 0
    out = dict(lp)
    for tag, mu, w1, w2 in names:
        rank = lp[w1].shape[1]
        rpad = _round_up(rank, LANE)
        cur.append(_pad_cols((1.0 - lp[mu])[:, None] * lp[w1], rpad))
        prev.append(_pad_cols(lp[mu][:, None] * lp[w1], rpad))
        out[w2 + 'p'] = _pad_rows(lp[w2], rpad).astype(BF16)
        low.append((off, rpad))
        off += rpad
    if not has_vres:
        low.append((0, 0))
    in_cols = lp['w_in'].shape[1]
    low_start = _round_up(in_cols, 2 * off)
    out['w_big'] = jnp.concatenate([_pad_cols(lp['w_in'], low_start)] + cur + prev,
                                   axis=1).astype(BF16)
    return out, tuple(low), low_start // (2 * off)


def _mixer(xm, x, lp, mods, vfirst, nxt, router, side=(), scan_side=(), out_side=()):
    bsz, s, d = x.shape
    rw = lp['w0'].shape[0]
    cw = lp['conv_b'].shape[0]
    has_vres = vfirst is not None
    lw, low, low_blk = _mixer_weights(lp, has_vres)
    n_cols = lw['w_big'].shape[1]
    h, side_bf16 = matmul([xm.reshape(bsz * s, d)], [lw['w_big']], BF16,
                          tn_pref=n_cols // (low_blk + 1), name="in_projection", side=side)
    h3 = h.reshape(bsz, s, n_cols)
    tm = _pick(s, 256)
    outs = rwkv_prep(h3, rw, low, low_blk, lw, vfirst, tm)
    rt, kt, at, bt, vb, g, bonus, wl = outs[:8]
    v_out = vfirst if has_vres else outs[8]
    y_rwkv, scan_bf16 = wkv_scan(rt, kt, at, bt, vb, g, bonus, wl, lp['lnx_g'], lp['lnx_b'], tm,
                                 side=scan_side)
    u = conv_group(h3, rw, cw, lp['conv_w'], lp['conv_b'], lp['conv_ln_g'], lp['conv_ln_b'])
    w_out = lp['w_out'].astype(BF16)
    o, out_bf16 = matmul([y_rwkv.reshape(bsz * s, rw), u.reshape(bsz * s, cw)],
                         [w_out[:rw], w_out[rw:]], BF16, name="out_projection", side=out_side)
    res = sublayer_epilogue(o.reshape(bsz, s, d), x, lp['mix_post_g'], mods['gt_m'], nxt, router)
    return res, v_out, side_bf16 + scan_bf16 + out_bf16


def _split_layer(p, prefix, has_vres):
    keys = ['ada_w', 'ada_b', 'mix_pre_g', 'mix_post_g', 'w_in', 'mu_rkv', 'mu_w', 'mu_a', 'mu_g',
            'w0', 'w1', 'w2', 'a0', 'a1', 'a2', 'g1', 'g2', 'k_k', 'k_a', 'r_k', 'lnx_g', 'lnx_b',
            'conv_w', 'conv_b', 'conv_ln_g', 'conv_ln_b', 'w_out', 'ffn_pre_g', 'ffn_post_g']
    if has_vres:
        keys += ['mu_v', 'v0', 'v1', 'v2']
    return {k: p[prefix + k] for k in keys}


def _forward(p):
    x = p['x']
    c = p['c']
    bsz, s, d = x.shape
    layers = [_split_layer(p, 'l0_', False), _split_layer(p, 'l1_', True)]
    mods = []
    for lp in layers:
        mod = ada_modulation(c, lp['ada_w'], lp['ada_b'])
        mods.append(dict(zip(['sh_m', 'sc_m', 'gt_m', 'sh_f', 'sc_f', 'gt_f'],
                             jnp.split(mod, 6, axis=-1))))

    lp, md = layers[0], mods[0]
    xm = norm_modulate(x, lp['mix_pre_g'], md['sh_m'], md['sc_m'])
    (x, xf), vfirst, cast0 = _mixer(
        xm, x, lp, md, None, (lp['ffn_pre_g'], md['sh_f'], md['sc_f']), None,
        side=(p['l0_ffn_w_gate'], p['l0_ffn_w_up']), scan_side=(p['l1_moe_w_gate'],),
        out_side=(p['l0_ffn_w_down'],))
    ffn_w, moe_w_gate = (cast0[0], cast0[1], cast0[3]), cast0[2]
    y, _ = dense_swiglu(xf.reshape(bsz * s, d), *ffn_w)
    nlp, nmd = layers[1], mods[1]
    x, xm = sublayer_epilogue(y.reshape(bsz, s, d), x, lp['ffn_post_g'], md['gt_f'],
                              (nlp['mix_pre_g'], nmd['sh_m'], nmd['sc_m']))

    lp, md = layers[1], mods[1]
    (x, xf, route), _, (moe_w_down, moe_w_up) = _mixer(
        xm, x, lp, md, vfirst, (lp['ffn_pre_g'], md['sh_f'], md['sc_f']),
        (p['l1_router_w'], p['l1_router_b']),
        side=(p['l1_moe_w_down'],), scan_side=(p['l1_moe_w_up'],))
    return moe_sublayer(xf, route, x, lp['ffn_post_g'], md['gt_f'],
                        moe_w_gate, moe_w_up, moe_w_down)


_ARG_NAMES = (
    'x', 'c',
    'l0_ada_w', 'l0_ada_b', 'l0_mix_pre_g', 'l0_mix_post_g', 'l0_w_in', 'l0_mu_rkv', 'l0_mu_w',
    'l0_mu_a', 'l0_mu_g', 'l0_w0', 'l0_w1', 'l0_w2', 'l0_a0', 'l0_a1', 'l0_a2', 'l0_g1', 'l0_g2',
    'l0_k_k', 'l0_k_a', 'l0_r_k', 'l0_lnx_g', 'l0_lnx_b', 'l0_conv_w', 'l0_conv_b', 'l0_conv_ln_g',
    'l0_conv_ln_b', 'l0_w_out',
    'l0_ffn_pre_g', 'l0_ffn_post_g', 'l0_ffn_w_gate', 'l0_ffn_w_up', 'l0_ffn_w_down',
    'l1_ada_w', 'l1_ada_b', 'l1_mix_pre_g', 'l1_mix_post_g', 'l1_w_in', 'l1_mu_rkv', 'l1_mu_w',
    'l1_mu_a', 'l1_mu_g', 'l1_w0', 'l1_w1', 'l1_w2', 'l1_a0', 'l1_a1', 'l1_a2', 'l1_g1', 'l1_g2',
    'l1_k_k', 'l1_k_a', 'l1_r_k', 'l1_lnx_g', 'l1_lnx_b', 'l1_conv_w', 'l1_conv_b', 'l1_conv_ln_g',
    'l1_conv_ln_b', 'l1_w_out', 'l1_mu_v', 'l1_v0', 'l1_v1', 'l1_v2',
    'l1_ffn_pre_g', 'l1_ffn_post_g', 'l1_router_w', 'l1_router_b', 'l1_moe_w_gate',
    'l1_moe_w_up', 'l1_moe_w_down')


def kernel(x, c, l0_ada_w, l0_ada_b, l0_mix_pre_g, l0_mix_post_g, l0_w_in, l0_mu_rkv, l0_mu_w, l0_mu_a, l0_mu_g, l0_w0, l0_w1, l0_w2, l0_a0, l0_a1, l0_a2, l0_g1, l0_g2, l0_k_k, l0_k_a, l0_r_k, l0_lnx_g, l0_lnx_b, l0_conv_w, l0_conv_b, l0_conv_ln_g, l0_conv_ln_b, l0_w_out, l0_ffn_pre_g, l0_ffn_post_g, l0_ffn_w_gate, l0_ffn_w_up, l0_ffn_w_down, l1_ada_w, l1_ada_b, l1_mix_pre_g, l1_mix_post_g, l1_w_in, l1_mu_rkv, l1_mu_w, l1_mu_a, l1_mu_g, l1_w0, l1_w1, l1_w2, l1_a0, l1_a1, l1_a2, l1_g1, l1_g2, l1_k_k, l1_k_a, l1_r_k, l1_lnx_g, l1_lnx_b, l1_conv_w, l1_conv_b, l1_conv_ln_g, l1_conv_ln_b, l1_w_out, l1_mu_v, l1_v0, l1_v1, l1_v2, l1_ffn_pre_g, l1_ffn_post_g, l1_router_w, l1_router_b, l1_moe_w_gate, l1_moe_w_up, l1_moe_w_down):
    args = (x, c, l0_ada_w, l0_ada_b, l0_mix_pre_g, l0_mix_post_g, l0_w_in, l0_mu_rkv, l0_mu_w, l0_mu_a, l0_mu_g, l0_w0, l0_w1, l0_w2, l0_a0, l0_a1, l0_a2, l0_g1, l0_g2, l0_k_k, l0_k_a, l0_r_k, l0_lnx_g, l0_lnx_b, l0_conv_w, l0_conv_b, l0_conv_ln_g, l0_conv_ln_b, l0_w_out, l0_ffn_pre_g, l0_ffn_post_g, l0_ffn_w_gate, l0_ffn_w_up, l0_ffn_w_down, l1_ada_w, l1_ada_b, l1_mix_pre_g, l1_mix_post_g, l1_w_in, l1_mu_rkv, l1_mu_w, l1_mu_a, l1_mu_g, l1_w0, l1_w1, l1_w2, l1_a0, l1_a1, l1_a2, l1_g1, l1_g2, l1_k_k, l1_k_a, l1_r_k, l1_lnx_g, l1_lnx_b, l1_conv_w, l1_conv_b, l1_conv_ln_g, l1_conv_ln_b, l1_w_out, l1_mu_v, l1_v0, l1_v1, l1_v2, l1_ffn_pre_g, l1_ffn_post_g, l1_router_w, l1_router_b, l1_moe_w_gate, l1_moe_w_up, l1_moe_w_down)
    return _forward(dict(zip(_ARG_NAMES, args)))
```
